```python
import jax, jax.numpy as jnp
from jax import lax
import numpy as np

D_MODEL = 4096
BATCH = 1
SEQ = 16384
DEPTH = 1
DEC_BATCH = 2
DEC_SEQ = 4096
PAST_LEN = 128

GRID_W = 64
N_MEM = 256
NA_HEADS = 16
NA_HEAD_DIM = 128
NA_WIDTH = NA_HEADS * NA_HEAD_DIM
NA_MAX_KH = 8
NA_KW = 16
POOL_WINDOWS = (2, 4, 8, 16)
POOL_GROUPS = 4
POOL_GROUP_DIM = 256
POOL_WIDTH = POOL_GROUPS * POOL_GROUP_DIM
MEM_HEADS = 4
MEM_HEAD_DIM = 256
MEM_WIDTH = MEM_HEADS * MEM_HEAD_DIM
N_BRANCH = 3
IN_WIDTH = 3 * NA_WIDTH + POOL_WIDTH + MEM_WIDTH + N_BRANCH * D_MODEL
D_FF = 11008
RMS_EPS = 1e-6

kernel_name = "hybrid_natten_pool_memxattn_macaron_encoder"


def _rmsnorm(x, g):
    xf = x.astype(jnp.float32)
    y = xf * lax.rsqrt(jnp.mean(xf * xf, axis=-1, keepdims=True) + RMS_EPS)
    return (y * g.astype(jnp.float32)).astype(x.dtype)


def _swiglu(h, w_gate, w_up, w_down):
    return (jax.nn.silu(h @ w_gate) * (h @ w_up)) @ w_down


def _neighbourhood_attention(q, k, v, rpb):
    b, L, h, dh = q.shape
    rows = L // GRID_W
    kh = min(NA_MAX_KH, rows)
    n_keys = kh * NA_KW
    cols = jnp.arange(GRID_W)
    col_start = jnp.clip(cols - NA_KW // 2, 0, GRID_W - NA_KW)
    col_idx = col_start[:, None] + jnp.arange(NA_KW)[None, :]
    col_off = col_idx - cols[:, None] + (NA_KW - 1)
    scale = dh ** -0.5
    q_rows = q.reshape(b, rows, GRID_W, h, dh)

    def one_row(r):
        row_start = jnp.clip(r - kh // 2, 0, rows - kh)
        row_ids = row_start + jnp.arange(kh)
        key_idx = (row_ids[None, :, None] * GRID_W + col_idx[:, None, :]).reshape(GRID_W, n_keys)
        row_off = row_ids - r + (NA_MAX_KH - 1)
        bias = rpb[:, row_off[None, :, None], col_off[:, None, :]].reshape(h, GRID_W, n_keys)
        k_g = k[:, key_idx]
        v_g = v[:, key_idx]
        q_r = lax.dynamic_index_in_dim(q_rows, r, axis=1, keepdims=False)
        s = jnp.einsum('bchd,bckhd->bhck', q_r, k_g).astype(jnp.float32) * scale
        s = s + bias.astype(jnp.float32)[None]
        p = jax.nn.softmax(s, axis=-1).astype(v.dtype)
        return jnp.einsum('bhck,bckhd->bchd', p, v_g)

    o = lax.map(one_row, jnp.arange(rows))
    return jnp.moveaxis(o, 0, 1).reshape(b, L, h * dh)


def _pool_mixer(u, w_pool, pool_scale):
    b, L, _ = u.shape
    ug = u.astype(jnp.float32).reshape(b, L, POOL_GROUPS, POOL_GROUP_DIM)
    csum = jnp.concatenate(
        [jnp.zeros((b, 1, POOL_GROUPS, POOL_GROUP_DIM), jnp.float32), jnp.cumsum(ug, axis=1)], axis=1)
    t = jnp.arange(L)
    outs = []
    for g, w in enumerate(POOL_WINDOWS):
        lo = jnp.clip(t - w // 2, 0, L)
        hi = jnp.clip(t + w // 2, 0, L)
        window_sum = csum[:, hi, g] - csum[:, lo, g]
        mean = window_sum / (hi - lo).astype(jnp.float32)[None, :, None]
        outs.append(mean - ug[:, :, g])
    pooled = jnp.stack(outs, axis=2).astype(u.dtype)
    mixed = jnp.einsum('blgc,gcd->blgd', pooled, w_pool).reshape(b, L, POOL_WIDTH)
    return mixed * pool_scale


def _memory_xattn(q_c, mem, g_mem, w_mem_kv):
    b, L, _ = q_c.shape
    kv = (_rmsnorm(mem, g_mem) @ w_mem_kv).reshape(b, mem.shape[1], 2, MEM_HEADS, MEM_HEAD_DIM)
    k, v = kv[:, :, 0], kv[:, :, 1]
    q = q_c.reshape(b, L, MEM_HEADS, MEM_HEAD_DIM)
    s = jnp.einsum('blhd,bmhd->bhlm', q, k).astype(jnp.float32) * (MEM_HEAD_DIM ** -0.5)
    p = jax.nn.softmax(s, axis=-1).astype(v.dtype)
    return jnp.einsum('bhlm,bmhd->blhd', p, v).reshape(b, L, MEM_WIDTH)


def _layer(x, mem,
           g_ffn1_pre, w1_gate, w1_up, w1_down, g_ffn1_post,
           g_mix_pre, w_in, rpb, w_pool, pool_scale, g_mem, w_mem_kv,
           w_a_out, w_b_out, w_c_out, b_gate, w_o, g_mix_post,
           g_ffn2_pre, w2_gate, w2_up, w2_down, g_ffn2_post, g_final):
    b, L, d = x.shape
    x = x + 0.5 * _rmsnorm(_swiglu(_rmsnorm(x, g_ffn1_pre), w1_gate, w1_up, w1_down), g_ffn1_post)
    h = _rmsnorm(x, g_mix_pre)
    z = h @ w_in
    o1 = NA_WIDTH
    o2 = 2 * NA_WIDTH
    o3 = 3 * NA_WIDTH
    o4 = o3 + POOL_WIDTH
    o5 = o4 + MEM_WIDTH
    q_a, k_a, v_a, u_b, q_c, gate_logits = jnp.split(z, [o1, o2, o3, o4, o5], axis=-1)
    hs = (b, L, NA_HEADS, NA_HEAD_DIM)
    y_a = _neighbourhood_attention(q_a.reshape(hs), k_a.reshape(hs), v_a.reshape(hs), rpb) @ w_a_out
    y_b = _pool_mixer(u_b, w_pool, pool_scale) @ w_b_out
    y_c = _memory_xattn(q_c, mem, g_mem, w_mem_kv) @ w_c_out
    gates = jax.nn.sigmoid(gate_logits.reshape(b, L, N_BRANCH, d) + b_gate)
    merged = gates[:, :, 0] * y_a + gates[:, :, 1] * y_b + gates[:, :, 2] * y_c
    x = x + _rmsnorm(merged @ w_o, g_mix_post)
    x = x + 0.5 * _rmsnorm(_swiglu(_rmsnorm(x, g_ffn2_pre), w2_gate, w2_up, w2_down), g_ffn2_post)
    return _rmsnorm(x, g_final)


def setup_inputs(seed: int = 0) -> dict:
    key = jax.random.key(seed)
    ks = iter(jax.random.split(key, 40))
    f32 = jnp.float32

    def nrm(shape, scale):
        return jax.random.normal(next(ks), shape, f32) * scale

    def gain(shape):
        return 1.0 + 0.02 * jax.random.normal(next(ks), shape, f32)

    D = D_MODEL
    L_ = DEPTH
    return {
        "x_prompt": nrm((BATCH, SEQ, D), 1.0),
        "x_sample": nrm((DEC_BATCH, DEC_SEQ, D), 1.0),
        "mem_prompt": nrm((BATCH, N_MEM, D), 1.0),
        "mem_sample": nrm((DEC_BATCH, N_MEM, D), 1.0),
        "g_ffn1_pre": gain((L_, D)),
        "w1_gate": nrm((L_, D, D_FF), D ** -0.5),
        "w1_up": nrm((L_, D, D_FF), D ** -0.5),
        "w1_down": nrm((L_, D_FF, D), D_FF ** -0.5),
        "g_ffn1_post": gain((L_, D)),
        "g_mix_pre": gain((L_, D)),
        "w_in": nrm((L_, D, IN_WIDTH), D ** -0.5),
        "rpb": nrm((L_, NA_HEADS, 2 * NA_MAX_KH - 1, 2 * NA_KW - 1), 0.1),
        "w_pool": nrm((L_, POOL_GROUPS, POOL_GROUP_DIM, POOL_GROUP_DIM), POOL_GROUP_DIM ** -0.5),
        "pool_scale": gain((L_, POOL_WIDTH)),
        "g_mem": gain((L_, D)),
        "w_mem_kv": nrm((L_, D, 2 * MEM_WIDTH), D ** -0.5),
        "w_a_out": nrm((L_, NA_WIDTH, D), NA_WIDTH ** -0.5),
        "w_b_out": nrm((L_, POOL_WIDTH, D), POOL_WIDTH ** -0.5),
        "w_c_out": nrm((L_, MEM_WIDTH, D), MEM_WIDTH ** -0.5),
        "b_gate": nrm((L_, N_BRANCH, D), 0.01),
        "w_o": nrm((L_, D, D), D ** -0.5),
        "g_mix_post": gain((L_, D)),
        "g_ffn2_pre": gain((L_, D)),
        "w2_gate": nrm((L_, D, D_FF), D ** -0.5),
        "w2_up": nrm((L_, D, D_FF), D ** -0.5),
        "w2_down": nrm((L_, D_FF, D), D_FF ** -0.5),
        "g_ffn2_post": gain((L_, D)),
        "g_final": gain((L_, D)),
    }


def reference(x_prompt, x_sample, mem_prompt, mem_sample,
              g_ffn1_pre, w1_gate, w1_up, w1_down, g_ffn1_post,
              g_mix_pre, w_in, rpb, w_pool, pool_scale, g_mem, w_mem_kv,
              w_a_out, w_b_out, w_c_out, b_gate, w_o, g_mix_post,
              g_ffn2_pre, w2_gate, w2_up, w2_down, g_ffn2_post, g_final):
    y_prompt = x_prompt
    y_sample = x_sample
    for i in range(DEPTH):
        p = [a[i] for a in (g_ffn1_pre, w1_gate, w1_up, w1_down, g_ffn1_post,
                            g_mix_pre, w_in, rpb, w_pool, pool_scale, g_mem, w_mem_kv,
                            w_a_out, w_b_out, w_c_out, b_gate, w_o, g_mix_post,
                            g_ffn2_pre, w2_gate, w2_up, w2_down, g_ffn2_post, g_final)]
        y_prompt = _layer(y_prompt, mem_prompt, *p)
        y_sample = _layer(y_sample, mem_sample, *p)
    return (y_prompt, y_sample)
```

```python
import functools

import numpy as np
import jax
import jax.numpy as jnp
from jax import lax
from jax.experimental import pallas as pl
from jax.experimental.pallas import tpu as pltpu

GRID_W = 64
POOL_WINDOWS = (2, 4, 8, 16)
MEM_HEADS = 4
RMS_EPS = 1e-6
MASK_VALUE = -1e30

NAT_Q_ROWS = 8
NAT_K_ROWS = 16
NAT_K_CHUNK_ROWS = 4

V7X_VMEM_BYTES = 64 * 1024 * 1024
VMEM_LIMIT = V7X_VMEM_BYTES - 6 * 1024 * 1024

F32 = jnp.float32
BF16 = jnp.bfloat16


def _rms(x, g):
    ms = jnp.mean(x * x, axis=-1, keepdims=True)
    return (x * lax.rsqrt(ms + RMS_EPS)) * g


def _for_row_chunks(n_rows, chunk, fn):
    def body(c, carry):
        fn(pl.ds(pl.multiple_of(c * chunk, chunk), chunk))
        return carry
    lax.fori_loop(0, n_rows // chunk, body, 0)


def _params(sem):
    return pltpu.CompilerParams(dimension_semantics=sem, vmem_limit_bytes=VMEM_LIMIT)


def _ffn_body(x_ref, gpre_ref, wg_ref, wu_ref, wd_ref, gpost_ref, gfin_ref, o_ref, h_ref,
              *, row_chunk, final_norm):
    j = pl.program_id(1)
    tm = x_ref.shape[0]

    @pl.when(j == 0)
    def _():
        def prologue(rows):
            h_ref[rows, :] = _rms(x_ref[rows, :], gpre_ref[...]).astype(BF16)
            o_ref[rows, :] = jnp.zeros((row_chunk, o_ref.shape[1]), F32)
        _for_row_chunks(tm, row_chunk, prologue)

    h = h_ref[...]
    g = jnp.dot(h, wg_ref[...], preferred_element_type=F32)
    u = jnp.dot(h, wu_ref[...], preferred_element_type=F32)
    a = ((g * (1.0 / (1.0 + jnp.exp(-g)))) * u).astype(BF16)
    o_ref[...] += jnp.dot(a, wd_ref[...], preferred_element_type=F32)

    @pl.when(j == pl.num_programs(1) - 1)
    def _():
        def epilogue(rows):
            out = x_ref[rows, :] + 0.5 * _rms(o_ref[rows, :], gpost_ref[...])
            if final_norm:
                out = _rms(out, gfin_ref[...])
            o_ref[rows, :] = out
        _for_row_chunks(tm, row_chunk, epilogue)


def _ffn(x, g_pre, wg, wu, wd, g_post, g_fin, *, final_norm, tm=512, tf=256):
    n_tok, d = x.shape
    d_ff = wg.shape[1]
    tm = min(tm, n_tok)
    tf = min(tf, d_ff)
    assert n_tok % tm == 0 and d_ff % tf == 0
    vec = pl.BlockSpec((1, d), lambda i, j: (0, 0))
    return pl.pallas_call(
        functools.partial(_ffn_body, row_chunk=min(64, tm), final_norm=final_norm),
        grid=(n_tok // tm, d_ff // tf),
        in_specs=[
            pl.BlockSpec((tm, d), lambda i, j: (i, 0)),
            vec,
            pl.BlockSpec((d, tf), lambda i, j: (0, j)),
            pl.BlockSpec((d, tf), lambda i, j: (0, j)),
            pl.BlockSpec((tf, d), lambda i, j: (j, 0)),
            vec,
            vec,
        ],
        out_specs=pl.BlockSpec((tm, d), lambda i, j: (i, 0)),
        out_shape=jax.ShapeDtypeStruct((n_tok, d), F32),
        scratch_shapes=[pltpu.VMEM((tm, d), BF16)],
        compiler_params=_params(("parallel", "arbitrary")),
        name="ffn",
    )(x, g_pre, wg, wu, wd, g_post, g_fin)


def _inproj_body(x_ref, g_ref, w_ref, h_ref, qkv_ref, u_ref, qc_ref, *, row_chunk, n_qkv, n_u):
    j = pl.program_id(1)
    tm = x_ref.shape[0]

    @pl.when(j == 0)
    def _():
        def prologue(rows):
            h_ref[rows, :] = _rms(x_ref[rows, :], g_ref[...]).astype(BF16)
        _for_row_chunks(tm, row_chunk, prologue)

    z = jnp.dot(h_ref[...], w_ref[...], preferred_element_type=F32)

    @pl.when(j < n_qkv)
    def _():
        qkv_ref[...] = z.astype(BF16)

    @pl.when(jnp.logical_and(j >= n_qkv, j < n_qkv + n_u))
    def _():
        u_ref[...] = z

    @pl.when(j >= n_qkv + n_u)
    def _():
        qc_ref[...] = z.astype(BF16)


def _inproj(x, g, w_in, *, na_width, pool_width, mem_width, tm=512, tn=512):
    n_tok, d = x.shape
    tm = min(tm, n_tok)
    tn = min(tn, pool_width, mem_width)
    assert n_tok % tm == 0 and na_width % tn == 0 and pool_width % tn == 0 and mem_width % tn == 0
    n_qkv = 3 * na_width // tn
    n_u = pool_width // tn
    n_qc = mem_width // tn
    return pl.pallas_call(
        functools.partial(_inproj_body, row_chunk=min(64, tm), n_qkv=n_qkv, n_u=n_u),
        grid=(n_tok // tm, n_qkv + n_u + n_qc),
        in_specs=[
            pl.BlockSpec((tm, d), lambda i, j: (i, 0)),
            pl.BlockSpec((1, d), lambda i, j: (0, 0)),
            pl.BlockSpec((d, tn), lambda i, j: (0, j)),
        ],
        out_specs=[
            pl.BlockSpec((tm, d), lambda i, j: (i, 0)),
            pl.BlockSpec((tm, tn), lambda i, j: (i, jnp.minimum(j, n_qkv - 1))),
            pl.BlockSpec((tm, tn), lambda i, j: (i, jnp.clip(j - n_qkv, 0, n_u - 1))),
            pl.BlockSpec((tm, tn), lambda i, j: (i, jnp.clip(j - n_qkv - n_u, 0, n_qc - 1))),
        ],
        out_shape=[
            jax.ShapeDtypeStruct((n_tok, d), BF16),
            jax.ShapeDtypeStruct((n_tok, 3 * na_width), BF16),
            jax.ShapeDtypeStruct((n_tok, pool_width), F32),
            jax.ShapeDtypeStruct((n_tok, mem_width), BF16),
        ],
        compiler_params=_params(("parallel", "arbitrary")),
        name="inproj",
    )(x, g, w_in)


def _natten_bias_tables(rpb):
    n_heads, n_dr, n_dc = rpb.shape
    kh, kw = (n_dr + 1) // 2, (n_dc + 1) // 2
    cols = np.arange(GRID_W)
    col_start = np.clip(cols - kw // 2, 0, GRID_W - kw)
    kc = np.arange(GRID_W)
    col_valid = (kc[None, :] >= col_start[:, None]) & (kc[None, :] < col_start[:, None] + kw)
    col_off = np.clip(kc[None, :] - cols[:, None] + (kw - 1), 0, n_dc - 1)
    tables = []
    for t in range(3):
        q_rows = np.arange(NAT_Q_ROWS) + (NAT_K_ROWS - NAT_Q_ROWS) // 2 * t
        k_rows = np.arange(NAT_K_ROWS)
        row_start = np.clip(q_rows - kh // 2, 0, NAT_K_ROWS - kh)
        row_valid = (k_rows[None, :] >= row_start[:, None]) & (k_rows[None, :] < row_start[:, None] + kh)
        row_off = np.clip(k_rows[None, :] - q_rows[:, None] + (kh - 1), 0, n_dr - 1)
        idx_r = np.broadcast_to(row_off[:, None, :, None], (NAT_Q_ROWS, GRID_W, NAT_K_ROWS, GRID_W))
        idx_c = np.broadcast_to(col_off[None, :, None, :], (NAT_Q_ROWS, GRID_W, NAT_K_ROWS, GRID_W))
        valid = row_valid[:, None, :, None] & col_valid[None, :, None, :]
        flat = (idx_r * n_dc + idx_c).reshape(NAT_Q_ROWS * GRID_W, NAT_K_ROWS * GRID_W)
        bias = jnp.take(rpb.reshape(n_heads, n_dr * n_dc).astype(F32), jnp.asarray(flat, jnp.int32), axis=1)
        tables.append(jnp.where(jnp.asarray(valid.reshape(flat.shape))[None], bias, MASK_VALUE))
    return jnp.stack(tables)


def _natten_body(q_ref, k0_ref, k1_ref, k2_ref, k3_ref, v0_ref, v1_ref, v2_ref, v3_ref, bias_ref, o_ref, *, scale):
    q = q_ref[0]
    k = jnp.concatenate([k0_ref[0], k1_ref[0], k2_ref[0], k3_ref[0]], axis=0)
    v = jnp.concatenate([v0_ref[0], v1_ref[0], v2_ref[0], v3_ref[0]], axis=0)
    s = lax.dot_general(q, k, (((1,), (1,)), ((), ())), preferred_element_type=F32)
    s = s * scale + bias_ref[0, 0]
    m = jnp.max(s, axis=-1, keepdims=True)
    p = jnp.exp(s - m)
    l = jnp.sum(p, axis=-1, keepdims=True)
    o = jnp.dot(p.astype(BF16), v, preferred_element_type=F32)
    o_ref[0] = (o / l).astype(o_ref.dtype)


def _natten(qkv, bias, *, n_heads, head_dim):
    b, seq, _ = qkv.shape
    tq = NAT_Q_ROWS * GRID_W
    tk = NAT_K_CHUNK_ROWS * GRID_W
    n_kc = NAT_K_ROWS // NAT_K_CHUNK_ROWS
    assert n_kc == 4 and seq % (NAT_K_ROWS * GRID_W) == 0
    n_qb = seq // tq
    n_kb = seq // tk
    lead = (NAT_K_ROWS - NAT_Q_ROWS) // 2 // NAT_K_CHUNK_ROWS

    def kv_spec(which, c):
        def index(bi, h, r):
            start = jnp.clip(r * (tq // tk) - lead, 0, n_kb - n_kc)
            return (bi, start + c, which * n_heads + h)
        return pl.BlockSpec((1, tk, head_dim), index)

    def bias_index(bi, h, r):
        t = jnp.where(r == 0, 0, jnp.where(r == n_qb - 1, 2, 1))
        return (t, h, 0, 0)

    return pl.pallas_call(
        functools.partial(_natten_body, scale=head_dim ** -0.5),
        grid=(b, n_heads, n_qb),
        in_specs=[pl.BlockSpec((1, tq, head_dim), lambda bi, h, r: (bi, r, h))]
        + [kv_spec(1, c) for c in range(n_kc)]
        + [kv_spec(2, c) for c in range(n_kc)]
        + [pl.BlockSpec((1, 1, tq, NAT_K_ROWS * GRID_W), bias_index)],
        out_specs=pl.BlockSpec((1, tq, head_dim), lambda bi, h, r: (bi, r, h)),
        out_shape=jax.ShapeDtypeStruct((b, seq, n_heads * head_dim), BF16),
        compiler_params=_params(("parallel", "parallel", "arbitrary")),
        name="natten",
    )(qkv, *([qkv] * (2 * n_kc)), bias)


def _pool_body(u_ref, prev_ref, next_ref, w_ref, scale_ref, o_ref, ext_ref, *, seq, halo, group_dim):
    t = pl.program_id(1)
    tt = u_ref.shape[1]
    u = u_ref[0]
    ext_ref[pl.ds(halo, tt), :] = u
    ext_ref[pl.ds(0, halo), :] = jnp.where(t > 0, prev_ref[0], 0.0)
    ext_ref[pl.ds(halo + tt, halo), :] = jnp.where(t < pl.num_programs(1) - 1, next_ref[0], 0.0)
    pos = t * tt + lax.broadcasted_iota(jnp.int32, (tt, 1), 0)
    for g, w in enumerate(POOL_WINDOWS):
        cols = pl.ds(g * group_dim, group_dim)
        total = ext_ref[pl.ds(halo - w // 2, tt), cols]
        for d in range(-w // 2 + 1, w // 2):
            total = total + ext_ref[pl.ds(halo + d, tt), cols]
        count = jnp.minimum(pos + w // 2, seq) - jnp.maximum(pos - w // 2, 0)
        pooled = total / count.astype(F32) - u[:, g * group_dim:(g + 1) * group_dim]
        mixed = jnp.dot(pooled.astype(BF16), w_ref[g], preferred_element_type=F32)
        o_ref[0, :, cols] = (mixed * scale_ref[:, cols]).astype(o_ref.dtype)


def _pool(u, w_pool, pool_scale, *, tt=512):
    b, seq, width = u.shape
    n_groups, group_dim, _ = w_pool.shape
    assert n_groups == len(POOL_WINDOWS)
    halo = 8
    assert max(POOL_WINDOWS) // 2 <= halo
    tt = min(tt, seq)
    assert seq % tt == 0 and tt % halo == 0
    hb = tt // halo
    return pl.pallas_call(
        functools.partial(_pool_body, seq=seq, halo=halo, group_dim=group_dim),
        grid=(b, seq // tt),
        in_specs=[
            pl.BlockSpec((1, tt, width), lambda bi, t: (bi, t, 0)),
            pl.BlockSpec((1, halo, width), lambda bi, t: (bi, jnp.maximum(t * hb - 1, 0), 0)),
            pl.BlockSpec((1, halo, width), lambda bi, t: (bi, jnp.minimum((t + 1) * hb, seq // halo - 1), 0)),
            pl.BlockSpec((n_groups, group_dim, group_dim), lambda bi, t: (0, 0, 0)),
            pl.BlockSpec((1, width), lambda bi, t: (0, 0)),
        ],
        out_specs=pl.BlockSpec((1, tt, width), lambda bi, t: (bi, t, 0)),
        out_shape=jax.ShapeDtypeStruct((b, seq, width), BF16),
        scratch_shapes=[pltpu.VMEM((tt + 2 * halo, width), F32)],
        compiler_params=_params(("parallel", "arbitrary")),
        name="pool",
    )(u, u, u, w_pool, pool_scale)


def _memkv_body(m_ref, g_ref, w_ref, o_ref):
    h = _rms(m_ref[...], g_ref[...]).astype(BF16)
    o_ref[...] = jnp.dot(h, w_ref[...], preferred_element_type=F32).astype(o_ref.dtype)


def _memkv(mem, g_mem, w_kv, *, tr=256, tn=512):
    n_rows, d = mem.shape
    width = w_kv.shape[1]
    tr = min(tr, n_rows)
    tn = min(tn, width)
    assert n_rows % tr == 0 and width % tn == 0
    return pl.pallas_call(
        _memkv_body,
        grid=(n_rows // tr, width // tn),
        in_specs=[
            pl.BlockSpec((tr, d), lambda i, j: (i, 0)),
            pl.BlockSpec((1, d), lambda i, j: (0, 0)),
            pl.BlockSpec((d, tn), lambda i, j: (0, j)),
        ],
        out_specs=pl.BlockSpec((tr, tn), lambda i, j: (i, j)),
        out_shape=jax.ShapeDtypeStruct((n_rows, width), BF16),
        compiler_params=_params(("parallel", "arbitrary")),
        name="memkv",
    )(mem, g_mem, w_kv)


def _memattn_body(q_ref, kv_ref, o_ref, *, n_heads, head_dim):
    scale = head_dim ** -0.5
    width = n_heads * head_dim
    for h in range(n_heads):
        cols = slice(h * head_dim, (h + 1) * head_dim)
        q = q_ref[0, :, cols]
        k = kv_ref[0, :, cols]
        v = kv_ref[0, :, width + h * head_dim: width + (h + 1) * head_dim]
        s = lax.dot_general(q, k, (((1,), (1,)), ((), ())), preferred_element_type=F32) * scale
        m = jnp.max(s, axis=-1, keepdims=True)
        p = jnp.exp(s - m)
        l = jnp.sum(p, axis=-1, keepdims=True)
        o = jnp.dot(p.astype(BF16), v, preferred_element_type=F32)
        o_ref[0, :, cols] = (o / l).astype(o_ref.dtype)


def _memattn(q, kv, *, n_heads, tq=512):
    b, seq, width = q.shape
    n_mem = kv.shape[1]
    tq = min(tq, seq)
    assert seq % tq == 0
    return pl.pallas_call(
        functools.partial(_memattn_body, n_heads=n_heads, head_dim=width // n_heads),
        grid=(b, seq // tq),
        in_specs=[
            pl.BlockSpec((1, tq, width), lambda bi, t: (bi, t, 0)),
            pl.BlockSpec((1, n_mem, 2 * width), lambda bi, t: (bi, 0, 0)),
        ],
        out_specs=pl.BlockSpec((1, tq, width), lambda bi, t: (bi, t, 0)),
        out_shape=jax.ShapeDtypeStruct((b, seq, width), BF16),
        compiler_params=_params(("parallel", "arbitrary")),
        name="memattn",
    )(q, kv)


def _merge_body(h_ref, a_ref, b_ref, c_ref, wga_ref, wgb_ref, wgc_ref, bg_ref, wa_ref, wb_ref, wc_ref, o_ref):
    h = h_ref[...]
    merged = None
    for i, (x_ref, wg_ref, w_ref) in enumerate(
            ((a_ref, wga_ref, wa_ref), (b_ref, wgb_ref, wb_ref), (c_ref, wgc_ref, wc_ref))):
        logits = jnp.dot(h, wg_ref[...], preferred_element_type=F32) + bg_ref[i:i + 1, :]
        gate = 1.0 / (1.0 + jnp.exp(-logits))
        y = jnp.dot(x_ref[...], w_ref[...], preferred_element_type=F32)
        merged = gate * y if merged is None else merged + gate * y
    o_ref[...] = merged.astype(o_ref.dtype)


def _merge(h, att_a, mix_b, att_c, w_in, gate_col0, b_gate, w_a, w_b, w_c, *, tm=512, tn=512):
    n_tok, d = h.shape
    tm = min(tm, n_tok)
    tn = min(tn, d)
    assert n_tok % tm == 0 and d % tn == 0 and gate_col0 % tn == 0
    nj = d // tn
    g0 = gate_col0 // tn

    def act(x):
        return pl.BlockSpec((tm, x.shape[1]), lambda i, j: (i, 0))

    def gate_w(branch):
        return pl.BlockSpec((d, tn), lambda i, j: (0, g0 + branch * nj + j))

    def out_w(w):
        return pl.BlockSpec((w.shape[0], tn), lambda i, j: (0, j))

    return pl.pallas_call(
        _merge_body,
        grid=(n_tok // tm, nj),
        in_specs=[act(h), act(att_a), act(mix_b), act(att_c),
                  gate_w(0), gate_w(1), gate_w(2),
                  pl.BlockSpec((b_gate.shape[0], tn), lambda i, j: (0, j)),
                  out_w(w_a), out_w(w_b), out_w(w_c)],
        out_specs=pl.BlockSpec((tm, tn), lambda i, j: (i, j)),
        out_shape=jax.ShapeDtypeStruct((n_tok, d), BF16),
        compiler_params=_params(("parallel", "arbitrary")),
        name="merge",
    )(h, att_a, mix_b, att_c, w_in, w_in, w_in, b_gate, w_a, w_b, w_c)


def _wo_body(x_ref, m_ref, w_ref, g_ref, o_ref, *, row_chunk):
    j = pl.program_id(1)
    tm = x_ref.shape[0]

    @pl.when(j == 0)
    def _():
        def prologue(rows):
            o_ref[rows, :] = jnp.zeros((row_chunk, o_ref.shape[1]), F32)
        _for_row_chunks(tm, row_chunk, prologue)

    o_ref[...] += jnp.dot(m_ref[...], w_ref[...], preferred_element_type=F32)

    @pl.when(j == pl.num_programs(1) - 1)
    def _():
        def epilogue(rows):
            o_ref[rows, :] = x_ref[rows, :] + _rms(o_ref[rows, :], g_ref[...])
        _for_row_chunks(tm, row_chunk, epilogue)


def _wo(x, merged, w_o, g_post, *, tm=512, tk=512):
    n_tok, d = x.shape
    tm = min(tm, n_tok)
    tk = min(tk, d)
    assert n_tok % tm == 0 and d % tk == 0
    return pl.pallas_call(
        functools.partial(_wo_body, row_chunk=min(64, tm)),
        grid=(n_tok // tm, d // tk),
        in_specs=[
            pl.BlockSpec((tm, d), lambda i, j: (i, 0)),
            pl.BlockSpec((tm, tk), lambda i, j: (i, j)),
            pl.BlockSpec((tk, d), lambda i, j: (j, 0)),
            pl.BlockSpec((1, d), lambda i, j: (0, 0)),
        ],
        out_specs=pl.BlockSpec((tm, d), lambda i, j: (i, 0)),
        out_shape=jax.ShapeDtypeStruct((n_tok, d), F32),
        compiler_params=_params(("parallel", "arbitrary")),
        name="wo",
    )(x, merged, w_o, g_post)


def _layer(x, mem, p, bias_tables):
    b, seq, d = x.shape
    n_tok = b * seq
    na_heads = p["rpb"].shape[0]
    n_groups, group_dim, _ = p["w_pool"].shape
    pool_width = n_groups * group_dim
    mem_width = p["w_mem_kv"].shape[1] // 2
    na_width = p["w_a_out"].shape[0]
    gate_col0 = 3 * na_width + pool_width + mem_width

    x0 = x.reshape(n_tok, d)
    x1 = _ffn(x0, p["g_ffn1_pre"], p["w1_gate"], p["w1_up"], p["w1_down"], p["g_ffn1_post"],
              p["g_final"], final_norm=False)
    h, qkv, u, qc = _inproj(x1, p["g_mix_pre"], p["w_in"],
                            na_width=na_width, pool_width=pool_width, mem_width=mem_width)
    att_a = _natten(qkv.reshape(b, seq, 3 * na_width), bias_tables,
                    n_heads=na_heads, head_dim=na_width // na_heads)
    mix_b = _pool(u.reshape(b, seq, pool_width), p["w_pool"], p["pool_scale"])
    kv = _memkv(mem.reshape(b * mem.shape[1], d), p["g_mem"], p["w_mem_kv"])
    att_c = _memattn(qc.reshape(b, seq, mem_width), kv.reshape(b, mem.shape[1], 2 * mem_width),
                     n_heads=MEM_HEADS)
    merged = _merge(h, att_a.reshape(n_tok, na_width), mix_b.reshape(n_tok, pool_width),
                    att_c.reshape(n_tok, mem_width), p["w_in"], gate_col0, p["b_gate"],
                    p["w_a_out"], p["w_b_out"], p["w_c_out"])
    x2 = _wo(x1, merged, p["w_o"], p["g_mix_post"])
    y = _ffn(x2, p["g_ffn2_pre"], p["w2_gate"], p["w2_up"], p["w2_down"], p["g_ffn2_post"],
             p["g_final"], final_norm=True)
    return y.reshape(b, seq, d)


_MATMUL_WEIGHTS = ("w1_gate", "w1_up", "w1_down", "w_in", "w_pool", "w_mem_kv",
                   "w_a_out", "w_b_out", "w_c_out", "w_o", "w2_gate", "w2_up", "w2_down")
_ROW_VECTORS = ("g_ffn1_pre", "g_ffn1_post", "g_mix_pre", "pool_scale", "g_mem", "g_mix_post",
                "g_ffn2_pre", "g_ffn2_post", "g_final")


def kernel(x_prompt, x_sample, mem_prompt, mem_sample, g_ffn1_pre, w1_gate, w1_up, w1_down, g_ffn1_post, g_mix_pre, w_in, rpb, w_pool, pool_scale, g_mem, w_mem_kv, w_a_out, w_b_out, w_c_out, b_gate, w_o, g_mix_post, g_ffn2_pre, w2_gate, w2_up, w2_down, g_ffn2_post, g_final):
    stacked = dict(g_ffn1_pre=g_ffn1_pre, w1_gate=w1_gate, w1_up=w1_up, w1_down=w1_down,
                   g_ffn1_post=g_ffn1_post, g_mix_pre=g_mix_pre, w_in=w_in, rpb=rpb, w_pool=w_pool,
                   pool_scale=pool_scale, g_mem=g_mem, w_mem_kv=w_mem_kv, w_a_out=w_a_out,
                   w_b_out=w_b_out, w_c_out=w_c_out, b_gate=b_gate, w_o=w_o, g_mix_post=g_mix_post,
                   g_ffn2_pre=g_ffn2_pre, w2_gate=w2_gate, w2_up=w2_up, w2_down=w2_down,
                   g_ffn2_post=g_ffn2_post, g_final=g_final)
    y_prompt, y_sample = x_prompt, x_sample
    for layer in range(g_final.shape[0]):
        p = {name: a[layer] for name, a in stacked.items()}
        for name in _MATMUL_WEIGHTS:
            p[name] = p[name].astype(BF16)
        for name in _ROW_VECTORS:
            p[name] = p[name].reshape(1, -1)
        bias_tables = _natten_bias_tables(p["rpb"])
        y_prompt = _layer(y_prompt, mem_prompt, p, bias_tables)
        y_sample = _layer(y_sample, mem_sample, p, bias_tables)
    return (y_prompt, y_sample)
```

```python
import functools

import numpy as np
import jax
import jax.numpy as jnp
from jax import lax
from jax.experimental import pallas as pl
from jax.experimental.pallas import tpu as pltpu

GRID_W = 64
POOL_WINDOWS = (2, 4, 8, 16)
MEM_HEADS = 4
RMS_EPS = 1e-6
MASK_VALUE = -1e30

NAT_Q_ROWS = 8
NAT_K_ROWS = 16
NAT_K_CHUNK_ROWS = 4

V7X_VMEM_BYTES = 64 * 1024 * 1024
VMEM_LIMIT = V7X_VMEM_BYTES - 6 * 1024 * 1024

F32 = jnp.float32
BF16 = jnp.bfloat16


def _rms(x, g):
    ms = jnp.mean(x * x, axis=-1, keepdims=True)
    return (x * lax.rsqrt(ms + RMS_EPS)) * g


def _for_row_chunks(n_rows, chunk, fn):
    def body(c, carry):
        fn(pl.ds(pl.multiple_of(c * chunk, chunk), chunk))
        return carry
    lax.fori_loop(0, n_rows // chunk, body, 0)


def _params(sem):
    return pltpu.CompilerParams(dimension_semantics=sem, vmem_limit_bytes=VMEM_LIMIT)


def _ffn_body(x_ref, gpre_ref, wg_ref, wu_ref, wd_ref, gpost_ref, gfin_ref, o_ref, h_ref,
              *, row_chunk, final_norm):
    j = pl.program_id(1)
    tm = x_ref.shape[0]

    @pl.when(j == 0)
    def _():
        def prologue(rows):
            h_ref[rows, :] = _rms(x_ref[rows, :], gpre_ref[...]).astype(BF16)
            o_ref[rows, :] = jnp.zeros((row_chunk, o_ref.shape[1]), F32)
        _for_row_chunks(tm, row_chunk, prologue)

    h = h_ref[...]
    g = jnp.dot(h, wg_ref[...], preferred_element_type=F32)
    u = jnp.dot(h, wu_ref[...], preferred_element_type=F32)
    a = ((g * (1.0 / (1.0 + jnp.exp(-g)))) * u).astype(BF16)
    o_ref[...] += jnp.dot(a, wd_ref[...], preferred_element_type=F32)

    @pl.when(j == pl.num_programs(1) - 1)
    def _():
        def epilogue(rows):
            out = x_ref[rows, :] + 0.5 * _rms(o_ref[rows, :], gpost_ref[...])
            if final_norm:
                out = _rms(out, gfin_ref[...])
            o_ref[rows, :] = out
        _for_row_chunks(tm, row_chunk, epilogue)


def _ffn(x, g_pre, wg, wu, wd, g_post, g_fin, *, final_norm, tm=512, tf=256):
    n_tok, d = x.shape
    d_ff = wg.shape[1]
    tm = min(tm, n_tok)
    tf = min(tf, d_ff)
    assert n_tok % tm == 0 and d_ff % tf == 0
    vec = pl.BlockSpec((1, d), lambda i, j: (0, 0))
    return pl.pallas_call(
        functools.partial(_ffn_body, row_chunk=min(64, tm), final_norm=final_norm),
        grid=(n_tok // tm, d_ff // tf),
        in_specs=[
            pl.BlockSpec((tm, d), lambda i, j: (i, 0)),
            vec,
            pl.BlockSpec((d, tf), lambda i, j: (0, j)),
            pl.BlockSpec((d, tf), lambda i, j: (0, j)),
            pl.BlockSpec((tf, d), lambda i, j: (j, 0)),
            vec,
            vec,
        ],
        out_specs=pl.BlockSpec((tm, d), lambda i, j: (i, 0)),
        out_shape=jax.ShapeDtypeStruct((n_tok, d), F32),
        scratch_shapes=[pltpu.VMEM((tm, d), BF16)],
        compiler_params=_params(("parallel", "arbitrary")),
        name="ffn",
    )(x, g_pre, wg, wu, wd, g_post, g_fin)


def _inproj_body(x_ref, g_ref, w_ref, h_ref, qkv_ref, u_ref, qc_ref, *, row_chunk, n_qkv, n_u):
    j = pl.program_id(1)
    tm = x_ref.shape[0]

    @pl.when(j == 0)
    def _():
        def prologue(rows):
            h_ref[rows, :] = _rms(x_ref[rows, :], g_ref[...]).astype(BF16)
        _for_row_chunks(tm, row_chunk, prologue)

    z = jnp.dot(h_ref[...], w_ref[...], preferred_element_type=F32)

    @pl.when(j < n_qkv)
    def _():
        qkv_ref[...] = z.astype(BF16)

    @pl.when(jnp.logical_and(j >= n_qkv, j < n_qkv + n_u))
    def _():
        u_ref[...] = z

    @pl.when(j >= n_qkv + n_u)
    def _():
        qc_ref[...] = z.astype(BF16)


def _inproj(x, g, w_in, *, na_width, pool_width, mem_width, tm=512, tn=512):
    n_tok, d = x.shape
    tm = min(tm, n_tok)
    tn = min(tn, pool_width, mem_width)
    assert n_tok % tm == 0 and na_width % tn == 0 and pool_width % tn == 0 and mem_width % tn == 0
    n_qkv = 3 * na_width // tn
    n_u = pool_width // tn
    n_qc = mem_width // tn
    return pl.pallas_call(
        functools.partial(_inproj_body, row_chunk=min(64, tm), n_qkv=n_qkv, n_u=n_u),
        grid=(n_tok // tm, n_qkv + n_u + n_qc),
        in_specs=[
            pl.BlockSpec((tm, d), lambda i, j: (i, 0)),
            pl.BlockSpec((1, d), lambda i, j: (0, 0)),
            pl.BlockSpec((d, tn), lambda i, j: (0, j)),
        ],
        out_specs=[
            pl.BlockSpec((tm, d), lambda i, j: (i, 0)),
            pl.BlockSpec((tm, tn), lambda i, j: (i, jnp.minimum(j, n_qkv - 1))),
            pl.BlockSpec((tm, tn), lambda i, j: (i, jnp.clip(j - n_qkv, 0, n_u - 1))),
            pl.BlockSpec((tm, tn), lambda i, j: (i, jnp.clip(j - n_qkv - n_u, 0, n_qc - 1))),
        ],
        out_shape=[
            jax.ShapeDtypeStruct((n_tok, d), BF16),
            jax.ShapeDtypeStruct((n_tok, 3 * na_width), BF16),
            jax.ShapeDtypeStruct((n_tok, pool_width), F32),
            jax.ShapeDtypeStruct((n_tok, mem_width), BF16),
        ],
        compiler_params=_params(("parallel", "arbitrary")),
        name="inproj",
    )(x, g, w_in)


def _natten_bias_tables(rpb):
    n_heads, n_dr, n_dc = rpb.shape
    kh, kw = (n_dr + 1) // 2, (n_dc + 1) // 2
    cols = np.arange(GRID_W)
    col_start = np.clip(cols - kw // 2, 0, GRID_W - kw)
    col_valid = (cols[None, :] >= col_start[:, None]) & (cols[None, :] < col_start[:, None] + kw)
    lpad = GRID_W - kw
    padded = jnp.pad(rpb.astype(F32), ((0, 0), (0, 0), (lpad, 2 * GRID_W - 1 - n_dc - lpad)))
    by_col = jnp.stack([padded[:, :, GRID_W - 1 - c: 2 * GRID_W - 1 - c] for c in range(GRID_W)], axis=1)
    by_col = jnp.where(jnp.asarray(col_valid)[None, :, None, :], by_col, MASK_VALUE)
    rpad = NAT_K_ROWS - kh
    by_col = jnp.pad(by_col, ((0, 0), (0, 0), (rpad, 2 * NAT_K_ROWS - 1 - n_dr - rpad), (0, 0)),
                     constant_values=MASK_VALUE)
    k_rows = np.arange(NAT_K_ROWS)
    tables = []
    for t in range(3):
        blocks = []
        for ri in range(NAT_Q_ROWS):
            r = (NAT_K_ROWS - NAT_Q_ROWS) // 2 * t + ri
            row_start = min(max(r - kh // 2, 0), NAT_K_ROWS - kh)
            row_valid = (k_rows >= row_start) & (k_rows < row_start + kh)
            blk = by_col[:, :, NAT_K_ROWS - 1 - r: 2 * NAT_K_ROWS - 1 - r, :]
            blk = jnp.where(jnp.asarray(row_valid)[None, None, :, None], blk, MASK_VALUE)
            blocks.append(blk.reshape(n_heads, GRID_W, NAT_K_ROWS * GRID_W))
        tables.append(jnp.stack(blocks, axis=1).reshape(n_heads, NAT_Q_ROWS * GRID_W, NAT_K_ROWS * GRID_W))
    return jnp.stack(tables)


def _natten_body(q_ref, k0_ref, k1_ref, k2_ref, k3_ref, v0_ref, v1_ref, v2_ref, v3_ref, bias_ref, o_ref, *, scale):
    q = q_ref[0]
    k = jnp.concatenate([k0_ref[0], k1_ref[0], k2_ref[0], k3_ref[0]], axis=0)
    v = jnp.concatenate([v0_ref[0], v1_ref[0], v2_ref[0], v3_ref[0]], axis=0)
    s = lax.dot_general(q, k, (((1,), (1,)), ((), ())), preferred_element_type=F32)
    s = s * scale + bias_ref[0, 0]
    m = jnp.max(s, axis=-1, keepdims=True)
    p = jnp.exp(s - m)
    l = jnp.sum(p, axis=-1, keepdims=True)
    o = jnp.dot(p.astype(BF16), v, preferred_element_type=F32)
    o_ref[0] = (o / l).astype(o_ref.dtype)


def _natten(qkv, bias, *, n_heads, head_dim):
    b, seq, _ = qkv.shape
    tq = NAT_Q_ROWS * GRID_W
    tk = NAT_K_CHUNK_ROWS * GRID_W
    n_kc = NAT_K_ROWS // NAT_K_CHUNK_ROWS
    assert n_kc == 4 and seq % (NAT_K_ROWS * GRID_W) == 0
    n_qb = seq // tq
    n_kb = seq // tk
    lead = (NAT_K_ROWS - NAT_Q_ROWS) // 2 // NAT_K_CHUNK_ROWS

    def kv_spec(which, c):
        def index(bi, h, r):
            start = jnp.clip(r * (tq // tk) - lead, 0, n_kb - n_kc)
            return (bi, start + c, which * n_heads + h)
        return pl.BlockSpec((1, tk, head_dim), index)

    def bias_index(bi, h, r):
        t = jnp.where(r == 0, 0, jnp.where(r == n_qb - 1, 2, 1))
        return (t, h, 0, 0)

    return pl.pallas_call(
        functools.partial(_natten_body, scale=head_dim ** -0.5),
        grid=(b, n_heads, n_qb),
        in_specs=[pl.BlockSpec((1, tq, head_dim), lambda bi, h, r: (bi, r, h))]
        + [kv_spec(1, c) for c in range(n_kc)]
        + [kv_spec(2, c) for c in range(n_kc)]
        + [pl.BlockSpec((1, 1, tq, NAT_K_ROWS * GRID_W), bias_index)],
        out_specs=pl.BlockSpec((1, tq, head_dim), lambda bi, h, r: (bi, r, h)),
        out_shape=jax.ShapeDtypeStruct((b, seq, n_heads * head_dim), BF16),
        compiler_params=_params(("parallel", "parallel", "arbitrary")),
        name="natten",
    )(qkv, *([qkv] * (2 * n_kc)), bias)


def _pool_body(u_ref, prev_ref, next_ref, w_ref, scale_ref, o_ref, ext_ref, *, seq, halo, group_dim):
    t = pl.program_id(1)
    tt = u_ref.shape[1]
    u = u_ref[0]
    ext_ref[pl.ds(halo, tt), :] = u
    ext_ref[pl.ds(0, halo), :] = jnp.where(t > 0, prev_ref[0], 0.0)
    ext_ref[pl.ds(halo + tt, halo), :] = jnp.where(t < pl.num_programs(1) - 1, next_ref[0], 0.0)
    pos = t * tt + lax.broadcasted_iota(jnp.int32, (tt, 1), 0)
    for g, w in enumerate(POOL_WINDOWS):
        cols = pl.ds(g * group_dim, group_dim)
        total = ext_ref[pl.ds(halo - w // 2, tt), cols]
        for d in range(-w // 2 + 1, w // 2):
            total = total + ext_ref[pl.ds(halo + d, tt), cols]
        count = jnp.minimum(pos + w // 2, seq) - jnp.maximum(pos - w // 2, 0)
        pooled = total / count.astype(F32) - u[:, g * group_dim:(g + 1) * group_dim]
        mixed = jnp.dot(pooled.astype(BF16), w_ref[g], preferred_element_type=F32)
        o_ref[0, :, cols] = (mixed * scale_ref[:, cols]).astype(o_ref.dtype)


def _pool(u, w_pool, pool_scale, *, tt=512):
    b, seq, width = u.shape
    n_groups, group_dim, _ = w_pool.shape
    assert n_groups == len(POOL_WINDOWS)
    halo = 8
    assert max(POOL_WINDOWS) // 2 <= halo
    tt = min(tt, seq)
    assert seq % tt == 0 and tt % halo == 0
    hb = tt // halo
    return pl.pallas_call(
        functools.partial(_pool_body, seq=seq, halo=halo, group_dim=group_dim),
        grid=(b, seq // tt),
        in_specs=[
            pl.BlockSpec((1, tt, width), lambda bi, t: (bi, t, 0)),
            pl.BlockSpec((1, halo, width), lambda bi, t: (bi, jnp.maximum(t * hb - 1, 0), 0)),
            pl.BlockSpec((1, halo, width), lambda bi, t: (bi, jnp.minimum((t + 1) * hb, seq // halo - 1), 0)),
            pl.BlockSpec((n_groups, group_dim, group_dim), lambda bi, t: (0, 0, 0)),
            pl.BlockSpec((1, width), lambda bi, t: (0, 0)),
        ],
        out_specs=pl.BlockSpec((1, tt, width), lambda bi, t: (bi, t, 0)),
        out_shape=jax.ShapeDtypeStruct((b, seq, width), BF16),
        scratch_shapes=[pltpu.VMEM((tt + 2 * halo, width), F32)],
        compiler_params=_params(("parallel", "arbitrary")),
        name="pool",
    )(u, u, u, w_pool, pool_scale)


def _memkv_body(m_ref, g_ref, w_ref, o_ref):
    h = _rms(m_ref[...], g_ref[...]).astype(BF16)
    o_ref[...] = jnp.dot(h, w_ref[...], preferred_element_type=F32).astype(o_ref.dtype)


def _memkv(mem, g_mem, w_kv, *, tr=256, tn=512):
    n_rows, d = mem.shape
    width = w_kv.shape[1]
    tr = min(tr, n_rows)
    tn = min(tn, width)
    assert n_rows % tr == 0 and width % tn == 0
    return pl.pallas_call(
        _memkv_body,
        grid=(n_rows // tr, width // tn),
        in_specs=[
            pl.BlockSpec((tr, d), lambda i, j: (i, 0)),
            pl.BlockSpec((1, d), lambda i, j: (0, 0)),
            pl.BlockSpec((d, tn), lambda i, j: (0, j)),
        ],
        out_specs=pl.BlockSpec((tr, tn), lambda i, j: (i, j)),
        out_shape=jax.ShapeDtypeStruct((n_rows, width), BF16),
        compiler_params=_params(("parallel", "arbitrary")),
        name="memkv",
    )(mem, g_mem, w_kv)


def _memattn_body(q_ref, kv_ref, o_ref, *, n_heads, head_dim):
    scale = head_dim ** -0.5
    width = n_heads * head_dim
    for h in range(n_heads):
        cols = slice(h * head_dim, (h + 1) * head_dim)
        q = q_ref[0, :, cols]
        k = kv_ref[0, :, cols]
        v = kv_ref[0, :, width + h * head_dim: width + (h + 1) * head_dim]
        s = lax.dot_general(q, k, (((1,), (1,)), ((), ())), preferred_element_type=F32) * scale
        m = jnp.max(s, axis=-1, keepdims=True)
        p = jnp.exp(s - m)
        l = jnp.sum(p, axis=-1, keepdims=True)
        o = jnp.dot(p.astype(BF16), v, preferred_element_type=F32)
        o_ref[0, :, cols] = (o / l).astype(o_ref.dtype)


def _memattn(q, kv, *, n_heads, tq=512):
    b, seq, width = q.shape
    n_mem = kv.shape[1]
    tq = min(tq, seq)
    assert seq % tq == 0
    return pl.pallas_call(
        functools.partial(_memattn_body, n_heads=n_heads, head_dim=width // n_heads),
        grid=(b, seq // tq),
        in_specs=[
            pl.BlockSpec((1, tq, width), lambda bi, t: (bi, t, 0)),
            pl.BlockSpec((1, n_mem, 2 * width), lambda bi, t: (bi, 0, 0)),
        ],
        out_specs=pl.BlockSpec((1, tq, width), lambda bi, t: (bi, t, 0)),
        out_shape=jax.ShapeDtypeStruct((b, seq, width), BF16),
        compiler_params=_params(("parallel", "arbitrary")),
        name="memattn",
    )(q, kv)


def _merge_body(h_ref, a_ref, b_ref, c_ref, wga_ref, wgb_ref, wgc_ref, bg_ref, wa_ref, wb_ref, wc_ref, o_ref):
    h = h_ref[...]
    merged = None
    for i, (x_ref, wg_ref, w_ref) in enumerate(
            ((a_ref, wga_ref, wa_ref), (b_ref, wgb_ref, wb_ref), (c_ref, wgc_ref, wc_ref))):
        logits = jnp.dot(h, wg_ref[...], preferred_element_type=F32) + bg_ref[i:i + 1, :]
        gate = 1.0 / (1.0 + jnp.exp(-logits))
        y = jnp.dot(x_ref[...], w_ref[...], preferred_element_type=F32)
        merged = gate * y if merged is None else merged + gate * y
    o_ref[...] = merged.astype(o_ref.dtype)


def _merge(h, att_a, mix_b, att_c, w_in, gate_col0, b_gate, w_a, w_b, w_c, *, tm=512, tn=512):
    n_tok, d = h.shape
    tm = min(tm, n_tok)
    tn = min(tn, d)
    assert n_tok % tm == 0 and d % tn == 0 and gate_col0 % tn == 0
    nj = d // tn
    g0 = gate_col0 // tn

    def act(x):
        return pl.BlockSpec((tm, x.shape[1]), lambda i, j: (i, 0))

    def gate_w(branch):
        return pl.BlockSpec((d, tn), lambda i, j: (0, g0 + branch * nj + j))

    def out_w(w):
        return pl.BlockSpec((w.shape[0], tn), lambda i, j: (0, j))

    return pl.pallas_call(
        _merge_body,
        grid=(n_tok // tm, nj),
        in_specs=[act(h), act(att_a), act(mix_b), act(att_c),
                  gate_w(0), gate_w(1), gate_w(2),
                  pl.BlockSpec((b_gate.shape[0], tn), lambda i, j: (0, j)),
                  out_w(w_a), out_w(w_b), out_w(w_c)],
        out_specs=pl.BlockSpec((tm, tn), lambda i, j: (i, j)),
        out_shape=jax.ShapeDtypeStruct((n_tok, d), BF16),
        compiler_params=_params(("parallel", "arbitrary")),
        name="merge",
    )(h, att_a, mix_b, att_c, w_in, w_in, w_in, b_gate, w_a, w_b, w_c)


def _wo_body(x_ref, m_ref, w_ref, g_ref, o_ref, *, row_chunk):
    j = pl.program_id(1)
    tm = x_ref.shape[0]

    @pl.when(j == 0)
    def _():
        def prologue(rows):
            o_ref[rows, :] = jnp.zeros((row_chunk, o_ref.shape[1]), F32)
        _for_row_chunks(tm, row_chunk, prologue)

    o_ref[...] += jnp.dot(m_ref[...], w_ref[...], preferred_element_type=F32)

    @pl.when(j == pl.num_programs(1) - 1)
    def _():
        def epilogue(rows):
            o_ref[rows, :] = x_ref[rows, :] + _rms(o_ref[rows, :], g_ref[...])
        _for_row_chunks(tm, row_chunk, epilogue)


def _wo(x, merged, w_o, g_post, *, tm=512, tk=512):
    n_tok, d = x.shape
    tm = min(tm, n_tok)
    tk = min(tk, d)
    assert n_tok % tm == 0 and d % tk == 0
    return pl.pallas_call(
        functools.partial(_wo_body, row_chunk=min(64, tm)),
        grid=(n_tok // tm, d // tk),
        in_specs=[
            pl.BlockSpec((tm, d), lambda i, j: (i, 0)),
            pl.BlockSpec((tm, tk), lambda i, j: (i, j)),
            pl.BlockSpec((tk, d), lambda i, j: (j, 0)),
            pl.BlockSpec((1, d), lambda i, j: (0, 0)),
        ],
        out_specs=pl.BlockSpec((tm, d), lambda i, j: (i, 0)),
        out_shape=jax.ShapeDtypeStruct((n_tok, d), F32),
        compiler_params=_params(("parallel", "arbitrary")),
        name="wo",
    )(x, merged, w_o, g_post)


def _layer(x, mem, p, bias_tables):
    b, seq, d = x.shape
    n_tok = b * seq
    na_heads = p["rpb"].shape[0]
    n_groups, group_dim, _ = p["w_pool"].shape
    pool_width = n_groups * group_dim
    mem_width = p["w_mem_kv"].shape[1] // 2
    na_width = p["w_a_out"].shape[0]
    gate_col0 = 3 * na_width + pool_width + mem_width

    x0 = x.reshape(n_tok, d)
    x1 = _ffn(x0, p["g_ffn1_pre"], p["w1_gate"], p["w1_up"], p["w1_down"], p["g_ffn1_post"],
              p["g_final"], final_norm=False)
    h, qkv, u, qc = _inproj(x1, p["g_mix_pre"], p["w_in"],
                            na_width=na_width, pool_width=pool_width, mem_width=mem_width)
    att_a = _natten(qkv.reshape(b, seq, 3 * na_width), bias_tables,
                    n_heads=na_heads, head_dim=na_width // na_heads)
    mix_b = _pool(u.reshape(b, seq, pool_width), p["w_pool"], p["pool_scale"])
    kv = _memkv(mem.reshape(b * mem.shape[1], d), p["g_mem"], p["w_mem_kv"])
    att_c = _memattn(qc.reshape(b, seq, mem_width), kv.reshape(b, mem.shape[1], 2 * mem_width),
                     n_heads=MEM_HEADS)
    merged = _merge(h, att_a.reshape(n_tok, na_width), mix_b.reshape(n_tok, pool_width),
                    att_c.reshape(n_tok, mem_width), p["w_in"], gate_col0, p["b_gate"],
                    p["w_a_out"], p["w_b_out"], p["w_c_out"])
    x2 = _wo(x1, merged, p["w_o"], p["g_mix_post"])
    y = _ffn(x2, p["g_ffn2_pre"], p["w2_gate"], p["w2_up"], p["w2_down"], p["g_ffn2_post"],
             p["g_final"], final_norm=True)
    return y.reshape(b, seq, d)


_MATMUL_WEIGHTS = ("w1_gate", "w1_up", "w1_down", "w_in", "w_pool", "w_mem_kv",
                   "w_a_out", "w_b_out", "w_c_out", "w_o", "w2_gate", "w2_up", "w2_down")
_ROW_VECTORS = ("g_ffn1_pre", "g_ffn1_post", "g_mix_pre", "pool_scale", "g_mem", "g_mix_post",
                "g_ffn2_pre", "g_ffn2_post", "g_final")


def kernel(x_prompt, x_sample, mem_prompt, mem_sample, g_ffn1_pre, w1_gate, w1_up, w1_down, g_ffn1_post, g_mix_pre, w_in, rpb, w_pool, pool_scale, g_mem, w_mem_kv, w_a_out, w_b_out, w_c_out, b_gate, w_o, g_mix_post, g_ffn2_pre, w2_gate, w2_up, w2_down, g_ffn2_post, g_final):
    stacked = dict(g_ffn1_pre=g_ffn1_pre, w1_gate=w1_gate, w1_up=w1_up, w1_down=w1_down,
                   g_ffn1_post=g_ffn1_post, g_mix_pre=g_mix_pre, w_in=w_in, rpb=rpb, w_pool=w_pool,
                   pool_scale=pool_scale, g_mem=g_mem, w_mem_kv=w_mem_kv, w_a_out=w_a_out,
                   w_b_out=w_b_out, w_c_out=w_c_out, b_gate=b_gate, w_o=w_o, g_mix_post=g_mix_post,
                   g_ffn2_pre=g_ffn2_pre, w2_gate=w2_gate, w2_up=w2_up, w2_down=w2_down,
                   g_ffn2_post=g_ffn2_post, g_final=g_final)
    y_prompt, y_sample = x_prompt, x_sample
    for layer in range(g_final.shape[0]):
        p = {name: a[layer] for name, a in stacked.items()}
        for name in _MATMUL_WEIGHTS:
            p[name] = p[name].astype(BF16)
        for name in _ROW_VECTORS:
            p[name] = p[name].reshape(1, -1)
        bias_tables = _natten_bias_tables(p["rpb"])
        y_prompt = _layer(y_prompt, mem_prompt, p, bias_tables)
        y_sample = _layer(y_sample, mem_sample, p, bias_tables)
    return (y_prompt, y_sample)
```

```python
import functools

import numpy as np
import jax
import jax.numpy as jnp
from jax import lax
from jax.experimental import pallas as pl
from jax.experimental.pallas import tpu as pltpu

GRID_W = 64
POOL_WINDOWS = (2, 4, 8, 16)
MEM_HEADS = 4
RMS_EPS = 1e-6
MASK_VALUE = -1e30

NAT_Q_ROWS = 8
NAT_K_ROWS = 16
NAT_K_CHUNK_ROWS = 4

V7X_VMEM_BYTES = 64 * 1024 * 1024
VMEM_LIMIT = V7X_VMEM_BYTES - 6 * 1024 * 1024

F32 = jnp.float32
BF16 = jnp.bfloat16


def _rms(x, g):
    ms = jnp.mean(x * x, axis=-1, keepdims=True)
    return (x * lax.rsqrt(ms + RMS_EPS)) * g


def _for_row_chunks(n_rows, chunk, fn):
    def body(c, carry):
        fn(pl.ds(pl.multiple_of(c * chunk, chunk), chunk))
        return carry
    lax.fori_loop(0, n_rows // chunk, body, 0)


def _params(sem):
    return pltpu.CompilerParams(dimension_semantics=sem, vmem_limit_bytes=VMEM_LIMIT)


def _silu_mul(g, u):
    return (g * (1.0 / (1.0 + jnp.exp(-g)))) * u


def _ffn_body(x_hbm, gpre_ref, wg_hbm, wu_hbm, wd_hbm, gpost_ref, gfin_ref, o_hbm,
              hbuf, accbuf, abuf, wgbuf, wubuf, wdbuf, xpbuf, xebuf, obuf, xedge, oedge,
              wsem, xpsem, xesem, osem, edge_in_sem, edge_out_sem,
              *, tm, tf, rc, edge_rows, n_tiles, n_ff, final_norm):
    n_chunks = tm // rc
    n_steps = n_tiles * n_ff

    def coords(s):
        t = s // n_ff
        return t, s - t * n_ff

    def up_weight_copies(j_up, slot):
        up_cols = pl.ds(pl.multiple_of(j_up * tf, tf), tf)
        return [pltpu.make_async_copy(wg_hbm.at[:, up_cols], wgbuf.at[slot], wsem.at[slot, 0]),
                pltpu.make_async_copy(wu_hbm.at[:, up_cols], wubuf.at[slot], wsem.at[slot, 1])]

    def weight_copies(j_down, j_up, slot):
        down_rows = pl.ds(pl.multiple_of(j_down * tf, tf), tf)
        return up_weight_copies(j_up, slot) + [
            pltpu.make_async_copy(wd_hbm.at[down_rows, :], wdbuf.at[slot], wsem.at[slot, 2])]

    def rows_of(tile, chunk, n_rows):
        return pl.ds(pl.multiple_of(tile * tm + chunk * n_rows, n_rows), n_rows)

    def x_copy(buf, sem, tile, chunk, slot):
        return pltpu.make_async_copy(x_hbm.at[rows_of(tile, chunk, rc), :], buf.at[slot], sem.at[slot])

    def out_copy(tile, chunk, slot):
        return pltpu.make_async_copy(obuf.at[slot], o_hbm.at[rows_of(tile, chunk, rc), :], osem.at[slot])

    def edge_in(tile, chunk, slot):
        return pltpu.make_async_copy(x_hbm.at[rows_of(tile, chunk, edge_rows), :], xedge.at[slot],
                                     edge_in_sem.at[slot])

    def edge_out(tile, chunk, slot):
        return pltpu.make_async_copy(oedge.at[slot], o_hbm.at[rows_of(tile, chunk, edge_rows), :],
                                     edge_out_sem.at[slot])

    def stage_inputs(s):
        t, j = coords(s)
        j_up = jnp.where(j == n_ff - 1, 0, j + 1)
        return (j, j_up, jnp.minimum(t + 1, n_tiles - 1), jnp.clip(t - 1, 0, n_tiles - 1),
                jnp.minimum(j, n_chunks - 1))

    def start_inputs(s, slot):
        j_down, j_up, t_pre, t_epi, chunk = stage_inputs(s)
        for c in weight_copies(j_down, j_up, slot):
            c.start()
        x_copy(xpbuf, xpsem, t_pre, chunk, slot).start()
        x_copy(xebuf, xesem, t_epi, chunk, slot).start()

    def wait_inputs(s, slot):
        j_down, j_up, t_pre, t_epi, chunk = stage_inputs(s)
        for c in weight_copies(j_down, j_up, slot):
            c.wait()
        x_copy(xpbuf, xpsem, t_pre, chunk, slot).wait()
        x_copy(xebuf, xesem, t_epi, chunk, slot).wait()

    def finish_rows(acc_rows, x_rows):
        out = x_rows + 0.5 * _rms(acc_rows, gpost_ref[...])
        if final_norm:
            out = _rms(out, gfin_ref[...])
        return out

    def up_chunk(h, slot):
        g = jnp.dot(h, wgbuf[slot], preferred_element_type=F32)
        u = jnp.dot(h, wubuf[slot], preferred_element_type=F32)
        return _silu_mul(g, u).astype(BF16)

    n_edge = tm // edge_rows
    edge_in(0, 0, 0).start()
    for c in range(n_edge):
        edge_in(0, c, c % 2).wait()
        if c + 1 < n_edge:
            edge_in(0, c + 1, (c + 1) % 2).start()
        hbuf[0, pl.ds(c * edge_rows, edge_rows), :] = _rms(xedge[c % 2], gpre_ref[...]).astype(BF16)
        accbuf[1, pl.ds(c * edge_rows, edge_rows), :] = jnp.zeros((edge_rows, accbuf.shape[2]), F32)
    for c in up_weight_copies(0, 2):
        c.start()
    for c in up_weight_copies(0, 2):
        c.wait()
    abuf[0] = up_chunk(hbuf[0], 2)
    start_inputs(0, 0)
    start_inputs(1, 1)

    def step(s, slot, a_slot):
        t, j = coords(s)
        chunk = stage_inputs(s)[4]
        cur = t % 2
        other = 1 - cur
        rows = pl.ds(pl.multiple_of(chunk * rc, rc), rc)
        hbuf[other, rows, :] = _rms(xpbuf[slot], gpre_ref[...]).astype(BF16)
        obuf[slot] = finish_rows(accbuf[other, rows, :], xebuf[slot])
        down = jnp.dot(abuf[a_slot], wdbuf[slot], preferred_element_type=F32)
        accbuf[cur] = jnp.where(j > 0, accbuf[cur], 0.0) + down
        h_sel = jnp.where(j == n_ff - 1, other, cur)
        abuf[1 - a_slot] = up_chunk(hbuf[h_sel], slot)

    def for_out_copies(i, base, fn):
        for k in range(2):
            t, j = coords(2 * i + k)
            _, _, _, t_epi, chunk = stage_inputs(2 * i + k)
            copy = out_copy(t_epi, chunk, base + k)
            pl.when(jnp.logical_and(jnp.logical_and(i >= 0, t >= 1), j < n_chunks))(functools.partial(fn, copy))

    def two_steps(i, carry):
        base = 2 * (i % 2)
        other_base = 2 - base
        s = 2 * i
        wait_inputs(s, base)
        wait_inputs(s + 1, base + 1)
        start_inputs(s + 2, other_base)
        start_inputs(s + 3, other_base + 1)
        for_out_copies(i - 2, base, lambda c: c.wait())
        for_out_copies(i - 1, other_base, lambda c: c.start())
        step(s, base, 0)
        step(s + 1, base + 1, 1)
        return carry

    n_iters = n_steps // 2
    lax.fori_loop(0, n_iters, two_steps, 0)

    end_base = 2 * (n_iters % 2)
    wait_inputs(n_steps, end_base)
    wait_inputs(n_steps + 1, end_base + 1)
    for_out_copies(jnp.int32(n_iters - 2), end_base, lambda c: c.wait())
    for_out_copies(jnp.int32(n_iters - 1), 2 - end_base, lambda c: c.start())
    for_out_copies(jnp.int32(n_iters - 1), 2 - end_base, lambda c: c.wait())
    last = n_tiles - 1
    edge_in(last, 0, 0).start()
    for c in range(n_edge):
        edge_in(last, c, c % 2).wait()
        if c + 1 < n_edge:
            edge_in(last, c + 1, (c + 1) % 2).start()
        if c >= 2:
            edge_out(last, c - 2, c % 2).wait()
        oedge[c % 2] = finish_rows(accbuf[last % 2, pl.ds(c * edge_rows, edge_rows), :], xedge[c % 2])
        edge_out(last, c, c % 2).start()
    for c in range(max(n_edge - 2, 0), n_edge):
        edge_out(last, c, c % 2).wait()


def _ffn(x, g_pre, wg, wu, wd, g_post, g_fin, *, final_norm, tm=512, tf=256, rc=16, edge_rows=64):
    n_tok, d = x.shape
    d_ff = wg.shape[1]
    tm = min(tm, n_tok // 2)
    tf = min(tf, d_ff)
    edge_rows = min(edge_rows, tm)
    n_tiles, n_ff = n_tok // tm, d_ff // tf
    assert n_tok % tm == 0 and d_ff % tf == 0 and tm % rc == 0 and tm % edge_rows == 0
    assert n_tiles % 2 == 0 and tm // rc <= n_ff and tm // edge_rows >= 2
    hbm = pl.BlockSpec(memory_space=pl.ANY)
    vec = pl.BlockSpec(memory_space=pltpu.VMEM)
    return pl.pallas_call(
        functools.partial(_ffn_body, tm=tm, tf=tf, rc=rc, edge_rows=edge_rows, n_tiles=n_tiles, n_ff=n_ff,
                          final_norm=final_norm),
        in_specs=[hbm, vec, hbm, hbm, hbm, vec, vec],
        out_specs=hbm,
        out_shape=jax.ShapeDtypeStruct((n_tok, d), F32),
        scratch_shapes=[
            pltpu.VMEM((2, tm, d), BF16),
            pltpu.VMEM((2, tm, d), F32),
            pltpu.VMEM((2, tm, tf), BF16),
            pltpu.VMEM((4, d, tf), BF16),
            pltpu.VMEM((4, d, tf), BF16),
            pltpu.VMEM((4, tf, d), BF16),
            pltpu.VMEM((4, rc, d), F32),
            pltpu.VMEM((4, rc, d), F32),
            pltpu.VMEM((4, rc, d), F32),
            pltpu.VMEM((2, edge_rows, d), F32),
            pltpu.VMEM((2, edge_rows, d), F32),
            pltpu.SemaphoreType.DMA((4, 3)),
            pltpu.SemaphoreType.DMA((4,)),
            pltpu.SemaphoreType.DMA((4,)),
            pltpu.SemaphoreType.DMA((4,)),
            pltpu.SemaphoreType.DMA((2,)),
            pltpu.SemaphoreType.DMA((2,)),
        ],
        compiler_params=pltpu.CompilerParams(vmem_limit_bytes=VMEM_LIMIT),
        name="ffn",
    )(x, g_pre, wg, wu, wd, g_post, g_fin)


def _inproj_body(x_ref, g_ref, w_ref, h_ref, qkv_ref, u_ref, qc_ref, *, row_chunk, n_qkv, n_u):
    j = pl.program_id(1)
    tm = x_ref.shape[0]

    @pl.when(j == 0)
    def _():
        def prologue(rows):
            h_ref[rows, :] = _rms(x_ref[rows, :], g_ref[...]).astype(BF16)
        _for_row_chunks(tm, row_chunk, prologue)

    z = jnp.dot(h_ref[...], w_ref[...], preferred_element_type=F32)

    @pl.when(j < n_qkv)
    def _():
        qkv_ref[...] = z.astype(BF16)

    @pl.when(jnp.logical_and(j >= n_qkv, j < n_qkv + n_u))
    def _():
        u_ref[...] = z

    @pl.when(j >= n_qkv + n_u)
    def _():
        qc_ref[...] = z.astype(BF16)


def _inproj(x, g, w_in, *, na_width, pool_width, mem_width, tm=512, tn=1024):
    n_tok, d = x.shape
    tm = min(tm, n_tok)
    tn = min(tn, pool_width, mem_width)
    assert n_tok % tm == 0 and na_width % tn == 0 and pool_width % tn == 0 and mem_width % tn == 0
    n_qkv = 3 * na_width // tn
    n_u = pool_width // tn
    n_qc = mem_width // tn
    return pl.pallas_call(
        functools.partial(_inproj_body, row_chunk=min(64, tm), n_qkv=n_qkv, n_u=n_u),
        grid=(n_tok // tm, n_qkv + n_u + n_qc),
        in_specs=[
            pl.BlockSpec((tm, d), lambda i, j: (i, 0)),
            pl.BlockSpec((1, d), lambda i, j: (0, 0)),
            pl.BlockSpec((d, tn), lambda i, j: (0, j)),
        ],
        out_specs=[
            pl.BlockSpec((tm, d), lambda i, j: (i, 0)),
            pl.BlockSpec((tm, tn), lambda i, j: (i, jnp.minimum(j, n_qkv - 1))),
            pl.BlockSpec((tm, tn), lambda i, j: (i, jnp.clip(j - n_qkv, 0, n_u - 1))),
            pl.BlockSpec((tm, tn), lambda i, j: (i, jnp.clip(j - n_qkv - n_u, 0, n_qc - 1))),
        ],
        out_shape=[
            jax.ShapeDtypeStruct((n_tok, d), BF16),
            jax.ShapeDtypeStruct((n_tok, 3 * na_width), BF16),
            jax.ShapeDtypeStruct((n_tok, pool_width), F32),
            jax.ShapeDtypeStruct((n_tok, mem_width), BF16),
        ],
        compiler_params=_params(("parallel", "arbitrary")),
        name="inproj",
    )(x, g, w_in)


def _natten_bias_tables(rpb):
    n_heads, n_dr, n_dc = rpb.shape
    kh, kw = (n_dr + 1) // 2, (n_dc + 1) // 2
    cols = np.arange(GRID_W)
    col_start = np.clip(cols - kw // 2, 0, GRID_W - kw)
    col_valid = (cols[None, :] >= col_start[:, None]) & (cols[None, :] < col_start[:, None] + kw)
    lpad = GRID_W - kw
    padded = jnp.pad(rpb.astype(F32), ((0, 0), (0, 0), (lpad, 2 * GRID_W - 1 - n_dc - lpad)))
    by_col = jnp.stack([padded[:, :, GRID_W - 1 - c: 2 * GRID_W - 1 - c] for c in range(GRID_W)], axis=1)
    by_col = jnp.where(jnp.asarray(col_valid)[None, :, None, :], by_col, MASK_VALUE)
    rpad = NAT_K_ROWS - kh
    by_col = jnp.pad(by_col, ((0, 0), (0, 0), (rpad, 2 * NAT_K_ROWS - 1 - n_dr - rpad), (0, 0)),
                     constant_values=MASK_VALUE)
    k_rows = np.arange(NAT_K_ROWS)
    tables = []
    for t in range(3):
        blocks = []
        for ri in range(NAT_Q_ROWS):
            r = (NAT_K_ROWS - NAT_Q_ROWS) // 2 * t + ri
            row_start = min(max(r - kh // 2, 0), NAT_K_ROWS - kh)
            row_valid = (k_rows >= row_start) & (k_rows < row_start + kh)
            blk = by_col[:, :, NAT_K_ROWS - 1 - r: 2 * NAT_K_ROWS - 1 - r, :]
            blk = jnp.where(jnp.asarray(row_valid)[None, None, :, None], blk, MASK_VALUE)
            blocks.append(blk.reshape(n_heads, GRID_W, NAT_K_ROWS * GRID_W))
        tables.append(jnp.stack(blocks, axis=1).reshape(n_heads, NAT_Q_ROWS * GRID_W, NAT_K_ROWS * GRID_W))
    return jnp.stack(tables)


def _natten_body(q_ref, k0_ref, k1_ref, k2_ref, k3_ref, v0_ref, v1_ref, v2_ref, v3_ref, bias_ref, o_ref, *, scale):
    q = q_ref[0]
    k = jnp.concatenate([k0_ref[0], k1_ref[0], k2_ref[0], k3_ref[0]], axis=0)
    v = jnp.concatenate([v0_ref[0], v1_ref[0], v2_ref[0], v3_ref[0]], axis=0)
    s = lax.dot_general(q, k, (((1,), (1,)), ((), ())), preferred_element_type=F32)
    s = s * scale + bias_ref[0, 0]
    m = jnp.max(s, axis=-1, keepdims=True)
    p = jnp.exp(s - m)
    l = jnp.sum(p, axis=-1, keepdims=True)
    o = jnp.dot(p.astype(BF16), v, preferred_element_type=F32)
    o_ref[0] = (o / l).astype(o_ref.dtype)


def _natten(qkv, bias, *, n_heads, head_dim):
    b, seq, _ = qkv.shape
    tq = NAT_Q_ROWS * GRID_W
    tk = NAT_K_CHUNK_ROWS * GRID_W
    n_kc = NAT_K_ROWS // NAT_K_CHUNK_ROWS
    assert n_kc == 4 and seq % (NAT_K_ROWS * GRID_W) == 0
    n_qb = seq // tq
    n_kb = seq // tk
    lead = (NAT_K_ROWS - NAT_Q_ROWS) // 2 // NAT_K_CHUNK_ROWS

    def kv_spec(which, c):
        def index(bi, h, r):
            start = jnp.clip(r * (tq // tk) - lead, 0, n_kb - n_kc)
            return (bi, start + c, which * n_heads + h)
        return pl.BlockSpec((1, tk, head_dim), index)

    def bias_index(bi, h, r):
        t = jnp.where(r == 0, 0, jnp.where(r == n_qb - 1, 2, 1))
        return (t, h, 0, 0)

    return pl.pallas_call(
        functools.partial(_natten_body, scale=head_dim ** -0.5),
        grid=(b, n_heads, n_qb),
        in_specs=[pl.BlockSpec((1, tq, head_dim), lambda bi, h, r: (bi, r, h))]
        + [kv_spec(1, c) for c in range(n_kc)]
        + [kv_spec(2, c) for c in range(n_kc)]
        + [pl.BlockSpec((1, 1, tq, NAT_K_ROWS * GRID_W), bias_index)],
        out_specs=pl.BlockSpec((1, tq, head_dim), lambda bi, h, r: (bi, r, h)),
        out_shape=jax.ShapeDtypeStruct((b, seq, n_heads * head_dim), BF16),
        compiler_params=_params(("parallel", "parallel", "arbitrary")),
        name="natten",
    )(qkv, *([qkv] * (2 * n_kc)), bias)


def _pool_body(u_ref, prev_ref, next_ref, w_ref, scale_ref, o_ref, ext_ref, *, seq, halo, group_dim):
    t = pl.program_id(1)
    tt = u_ref.shape[1]
    u = u_ref[0]
    ext_ref[pl.ds(halo, tt), :] = u
    ext_ref[pl.ds(0, halo), :] = jnp.where(t > 0, prev_ref[0], 0.0)
    ext_ref[pl.ds(halo + tt, halo), :] = jnp.where(t < pl.num_programs(1) - 1, next_ref[0], 0.0)
    pos = t * tt + lax.broadcasted_iota(jnp.int32, (tt, 1), 0)
    for g, w in enumerate(POOL_WINDOWS):
        cols = pl.ds(g * group_dim, group_dim)
        total = ext_ref[pl.ds(halo - w // 2, tt), cols]
        for d in range(-w // 2 + 1, w // 2):
            total = total + ext_ref[pl.ds(halo + d, tt), cols]
        count = jnp.minimum(pos + w // 2, seq) - jnp.maximum(pos - w // 2, 0)
        pooled = total / count.astype(F32) - u[:, g * group_dim:(g + 1) * group_dim]
        mixed = jnp.dot(pooled.astype(BF16), w_ref[g], preferred_element_type=F32)
        o_ref[0, :, cols] = (mixed * scale_ref[:, cols]).astype(o_ref.dtype)


def _pool(u, w_pool, pool_scale, *, tt=512):
    b, seq, width = u.shape
    n_groups, group_dim, _ = w_pool.shape
    assert n_groups == len(POOL_WINDOWS)
    halo = 8
    assert max(POOL_WINDOWS) // 2 <= halo
    tt = min(tt, seq)
    assert seq % tt == 0 and tt % halo == 0
    hb = tt // halo
    return pl.pallas_call(
        functools.partial(_pool_body, seq=seq, halo=halo, group_dim=group_dim),
        grid=(b, seq // tt),
        in_specs=[
            pl.BlockSpec((1, tt, width), lambda bi, t: (bi, t, 0)),
            pl.BlockSpec((1, halo, width), lambda bi, t: (bi, jnp.maximum(t * hb - 1, 0), 0)),
            pl.BlockSpec((1, halo, width), lambda bi, t: (bi, jnp.minimum((t + 1) * hb, seq // halo - 1), 0)),
            pl.BlockSpec((n_groups, group_dim, group_dim), lambda bi, t: (0, 0, 0)),
            pl.BlockSpec((1, width), lambda bi, t: (0, 0)),
        ],
        out_specs=pl.BlockSpec((1, tt, width), lambda bi, t: (bi, t, 0)),
        out_shape=jax.ShapeDtypeStruct((b, seq, width), BF16),
        scratch_shapes=[pltpu.VMEM((tt + 2 * halo, width), F32)],
        compiler_params=_params(("parallel", "arbitrary")),
        name="pool",
    )(u, u, u, w_pool, pool_scale)


def _memkv_body(m_ref, g_ref, w_ref, o_ref):
    h = _rms(m_ref[...], g_ref[...]).astype(BF16)
    o_ref[...] = jnp.dot(h, w_ref[...], preferred_element_type=F32).astype(o_ref.dtype)


def _memkv(mem, g_mem, w_kv, *, tr=256, tn=512):
    n_rows, d = mem.shape
    width = w_kv.shape[1]
    tr = min(tr, n_rows)
    tn = min(tn, width)
    assert n_rows % tr == 0 and width % tn == 0
    return pl.pallas_call(
        _memkv_body,
        grid=(n_rows // tr, width // tn),
        in_specs=[
            pl.BlockSpec((tr, d), lambda i, j: (i, 0)),
            pl.BlockSpec((1, d), lambda i, j: (0, 0)),
            pl.BlockSpec((d, tn), lambda i, j: (0, j)),
        ],
        out_specs=pl.BlockSpec((tr, tn), lambda i, j: (i, j)),
        out_shape=jax.ShapeDtypeStruct((n_rows, width), BF16),
        compiler_params=_params(("parallel", "arbitrary")),
        name="memkv",
    )(mem, g_mem, w_kv)


def _memattn_body(q_ref, kv_ref, o_ref, *, n_heads, head_dim):
    scale = head_dim ** -0.5
    width = n_heads * head_dim
    for h in range(n_heads):
        cols = slice(h * head_dim, (h + 1) * head_dim)
        q = q_ref[0, :, cols]
        k = kv_ref[0, :, cols]
        v = kv_ref[0, :, width + h * head_dim: width + (h + 1) * head_dim]
        s = lax.dot_general(q, k, (((1,), (1,)), ((), ())), preferred_element_type=F32) * scale
        m = jnp.max(s, axis=-1, keepdims=True)
        p = jnp.exp(s - m)
        l = jnp.sum(p, axis=-1, keepdims=True)
        o = jnp.dot(p.astype(BF16), v, preferred_element_type=F32)
        o_ref[0, :, cols] = (o / l).astype(o_ref.dtype)


def _memattn(q, kv, *, n_heads, tq=512):
    b, seq, width = q.shape
    n_mem = kv.shape[1]
    tq = min(tq, seq)
    assert seq % tq == 0
    return pl.pallas_call(
        functools.partial(_memattn_body, n_heads=n_heads, head_dim=width // n_heads),
        grid=(b, seq // tq),
        in_specs=[
            pl.BlockSpec((1, tq, width), lambda bi, t: (bi, t, 0)),
            pl.BlockSpec((1, n_mem, 2 * width), lambda bi, t: (bi, 0, 0)),
        ],
        out_specs=pl.BlockSpec((1, tq, width), lambda bi, t: (bi, t, 0)),
        out_shape=jax.ShapeDtypeStruct((b, seq, width), BF16),
        compiler_params=_params(("parallel", "arbitrary")),
        name="memattn",
    )(q, kv)


def _merge_body(h_ref, a_ref, b_ref, c_ref, wga_ref, wgb_ref, wgc_ref, bg_ref, wa_ref, wb_ref, wc_ref, o_ref):
    h = h_ref[...]
    merged = None
    for i, (x_ref, wg_ref, w_ref) in enumerate(
            ((a_ref, wga_ref, wa_ref), (b_ref, wgb_ref, wb_ref), (c_ref, wgc_ref, wc_ref))):
        logits = jnp.dot(h, wg_ref[...], preferred_element_type=F32) + bg_ref[i:i + 1, :]
        gate = 1.0 / (1.0 + jnp.exp(-logits))
        y = jnp.dot(x_ref[...], w_ref[...], preferred_element_type=F32)
        merged = gate * y if merged is None else merged + gate * y
    o_ref[...] = merged.astype(o_ref.dtype)


def _merge(h, att_a, mix_b, att_c, w_in, gate_col0, b_gate, w_a, w_b, w_c, *, tm=512, tn=512):
    n_tok, d = h.shape
    tm = min(tm, n_tok)
    tn = min(tn, d)
    assert n_tok % tm == 0 and d % tn == 0 and gate_col0 % tn == 0
    nj = d // tn
    g0 = gate_col0 // tn

    def act(x):
        return pl.BlockSpec((tm, x.shape[1]), lambda i, j: (i, 0))

    def gate_w(branch):
        return pl.BlockSpec((d, tn), lambda i, j: (0, g0 + branch * nj + j))

    def out_w(w):
        return pl.BlockSpec((w.shape[0], tn), lambda i, j: (0, j))

    return pl.pallas_call(
        _merge_body,
        grid=(n_tok // tm, nj),
        in_specs=[act(h), act(att_a), act(mix_b), act(att_c),
                  gate_w(0), gate_w(1), gate_w(2),
                  pl.BlockSpec((b_gate.shape[0], tn), lambda i, j: (0, j)),
                  out_w(w_a), out_w(w_b), out_w(w_c)],
        out_specs=pl.BlockSpec((tm, tn), lambda i, j: (i, j)),
        out_shape=jax.ShapeDtypeStruct((n_tok, d), BF16),
        compiler_params=_params(("parallel", "arbitrary")),
        name="merge",
    )(h, att_a, mix_b, att_c, w_in, w_in, w_in, b_gate, w_a, w_b, w_c)


def _wo_body(x_ref, m_ref, w_ref, g_ref, o_ref, *, row_chunk):
    j = pl.program_id(1)
    tm = x_ref.shape[0]

    @pl.when(j == 0)
    def _():
        def prologue(rows):
            o_ref[rows, :] = jnp.zeros((row_chunk, o_ref.shape[1]), F32)
        _for_row_chunks(tm, row_chunk, prologue)

    o_ref[...] += jnp.dot(m_ref[...], w_ref[...], preferred_element_type=F32)

    @pl.when(j == pl.num_programs(1) - 1)
    def _():
        def epilogue(rows):
            o_ref[rows, :] = x_ref[rows, :] + _rms(o_ref[rows, :], g_ref[...])
        _for_row_chunks(tm, row_chunk, epilogue)


def _wo(x, merged, w_o, g_post, *, tm=512, tk=1024):
    n_tok, d = x.shape
    tm = min(tm, n_tok)
    tk = min(tk, d)
    assert n_tok % tm == 0 and d % tk == 0
    return pl.pallas_call(
        functools.partial(_wo_body, row_chunk=min(64, tm)),
        grid=(n_tok // tm, d // tk),
        in_specs=[
            pl.BlockSpec((tm, d), lambda i, j: (i, 0)),
            pl.BlockSpec((tm, tk), lambda i, j: (i, j)),
            pl.BlockSpec((tk, d), lambda i, j: (j, 0)),
            pl.BlockSpec((1, d), lambda i, j: (0, 0)),
        ],
        out_specs=pl.BlockSpec((tm, d), lambda i, j: (i, 0)),
        out_shape=jax.ShapeDtypeStruct((n_tok, d), F32),
        compiler_params=_params(("parallel", "arbitrary")),
        name="wo",
    )(x, merged, w_o, g_post)


def _layer(x, mem, p, bias_tables):
    b, seq, d = x.shape
    n_tok = b * seq
    na_heads = p["rpb"].shape[0]
    n_groups, group_dim, _ = p["w_pool"].shape
    pool_width = n_groups * group_dim
    mem_width = p["w_mem_kv"].shape[1] // 2
    na_width = p["w_a_out"].shape[0]
    gate_col0 = 3 * na_width + pool_width + mem_width

    x0 = x.reshape(n_tok, d)
    x1 = _ffn(x0, p["g_ffn1_pre"], p["w1_gate"], p["w1_up"], p["w1_down"], p["g_ffn1_post"],
              p["g_final"], final_norm=False)
    h, qkv, u, qc = _inproj(x1, p["g_mix_pre"], p["w_in"],
                            na_width=na_width, pool_width=pool_width, mem_width=mem_width)
    att_a = _natten(qkv.reshape(b, seq, 3 * na_width), bias_tables,
                    n_heads=na_heads, head_dim=na_width // na_heads)
    mix_b = _pool(u.reshape(b, seq, pool_width), p["w_pool"], p["pool_scale"])
    kv = _memkv(mem.reshape(b * mem.shape[1], d), p["g_mem"], p["w_mem_kv"])
    att_c = _memattn(qc.reshape(b, seq, mem_width), kv.reshape(b, mem.shape[1], 2 * mem_width),
                     n_heads=MEM_HEADS)
    merged = _merge(h, att_a.reshape(n_tok, na_width), mix_b.reshape(n_tok, pool_width),
                    att_c.reshape(n_tok, mem_width), p["w_in"], gate_col0, p["b_gate"],
                    p["w_a_out"], p["w_b_out"], p["w_c_out"])
    x2 = _wo(x1, merged, p["w_o"], p["g_mix_post"])
    y = _ffn(x2, p["g_ffn2_pre"], p["w2_gate"], p["w2_up"], p["w2_down"], p["g_ffn2_post"],
             p["g_final"], final_norm=True)
    return y.reshape(b, seq, d)


_MATMUL_WEIGHTS = ("w1_gate", "w1_up", "w1_down", "w_in", "w_pool", "w_mem_kv",
                   "w_a_out", "w_b_out", "w_c_out", "w_o", "w2_gate", "w2_up", "w2_down")
_ROW_VECTORS = ("g_ffn1_pre", "g_ffn1_post", "g_mix_pre", "pool_scale", "g_mem", "g_mix_post",
                "g_ffn2_pre", "g_ffn2_post", "g_final")


def kernel(x_prompt, x_sample, mem_prompt, mem_sample, g_ffn1_pre, w1_gate, w1_up, w1_down, g_ffn1_post, g_mix_pre, w_in, rpb, w_pool, pool_scale, g_mem, w_mem_kv, w_a_out, w_b_out, w_c_out, b_gate, w_o, g_mix_post, g_ffn2_pre, w2_gate, w2_up, w2_down, g_ffn2_post, g_final):
    stacked = dict(g_ffn1_pre=g_ffn1_pre, w1_gate=w1_gate, w1_up=w1_up, w1_down=w1_down,
                   g_ffn1_post=g_ffn1_post, g_mix_pre=g_mix_pre, w_in=w_in, rpb=rpb, w_pool=w_pool,
                   pool_scale=pool_scale, g_mem=g_mem, w_mem_kv=w_mem_kv, w_a_out=w_a_out,
                   w_b_out=w_b_out, w_c_out=w_c_out, b_gate=b_gate, w_o=w_o, g_mix_post=g_mix_post,
                   g_ffn2_pre=g_ffn2_pre, w2_gate=w2_gate, w2_up=w2_up, w2_down=w2_down,
                   g_ffn2_post=g_ffn2_post, g_final=g_final)
    y_prompt, y_sample = x_prompt, x_sample
    for layer in range(g_final.shape[0]):
        p = {name: a[layer] for name, a in stacked.items()}
        for name in _MATMUL_WEIGHTS:
            p[name] = p[name].astype(BF16)
        for name in _ROW_VECTORS:
            p[name] = p[name].reshape(1, -1)
        bias_tables = _natten_bias_tables(p["rpb"])
        y_prompt = _layer(y_prompt, mem_prompt, p, bias_tables)
        y_sample = _layer(y_sample, mem_sample, p, bias_tables)
    return (y_prompt, y_sample)
```

```python
import functools

import numpy as np
import jax
import jax.numpy as jnp
from jax import lax
from jax.experimental import pallas as pl
from jax.experimental.pallas import tpu as pltpu

GRID_W = 64
POOL_WINDOWS = (2, 4, 8, 16)
MEM_HEADS = 4
RMS_EPS = 1e-6
MASK_VALUE = -1e30
LOG2_E = 1.4426950408889634

FFN_CHUNK = 256
NAT_Q_ROWS = 8
NAT_K_ROWS = 16
NAT_K_CHUNK_ROWS = 4

V7X_VMEM_BYTES = 64 * 1024 * 1024
VMEM_LIMIT = V7X_VMEM_BYTES - 6 * 1024 * 1024

F32 = jnp.float32
BF16 = jnp.bfloat16


def _rms(x, g):
    ms = jnp.mean(x * x, axis=-1, keepdims=True)
    return (x * lax.rsqrt(ms + RMS_EPS)) * g


def _for_row_chunks(n_rows, chunk, fn):
    def body(c, carry):
        fn(pl.ds(pl.multiple_of(c * chunk, chunk), chunk))
        return carry
    lax.fori_loop(0, n_rows // chunk, body, 0)


def _params(sem):
    return pltpu.CompilerParams(dimension_semantics=sem, vmem_limit_bytes=VMEM_LIMIT)


def _silu_mul(g, u):
    return (g * (1.0 / (1.0 + jnp.exp(-g)))) * u


def _ffn_body(x_hbm, gpre_ref, wg_hbm, wu_hbm, wd_hbm, gpost_ref, gfin_ref, o_hbm,
              hbuf, accbuf, abuf, wgbuf, wubuf, wdbuf, xpbuf, xebuf, obuf, xedge, oedge,
              wsem, xpsem, xesem, osem, edge_in_sem, edge_out_sem,
              *, tm, tf, rc, edge_rows, n_tiles, n_ff, final_norm):
    n_chunks = tm // rc
    n_steps = n_tiles * n_ff

    def coords(s):
        t = s // n_ff
        return t, s - t * n_ff

    def up_weight_copies(j_up, slot):
        return [pltpu.make_async_copy(wg_hbm.at[j_up], wgbuf.at[slot], wsem.at[slot, 0]),
                pltpu.make_async_copy(wu_hbm.at[j_up], wubuf.at[slot], wsem.at[slot, 1])]

    def weight_copies(j_down, j_up, slot):
        down_rows = pl.ds(pl.multiple_of(j_down * tf, tf), tf)
        return up_weight_copies(j_up, slot) + [
            pltpu.make_async_copy(wd_hbm.at[down_rows, :], wdbuf.at[slot], wsem.at[slot, 2])]

    def rows_of(tile, chunk, n_rows):
        return pl.ds(pl.multiple_of(tile * tm + chunk * n_rows, n_rows), n_rows)

    def x_copy(buf, sem, tile, chunk, slot):
        return pltpu.make_async_copy(x_hbm.at[rows_of(tile, chunk, rc), :], buf.at[slot], sem.at[slot])

    def out_copy(tile, chunk, slot):
        return pltpu.make_async_copy(obuf.at[slot], o_hbm.at[rows_of(tile, chunk, rc), :], osem.at[slot])

    def edge_in(tile, chunk, slot):
        return pltpu.make_async_copy(x_hbm.at[rows_of(tile, chunk, edge_rows), :], xedge.at[slot],
                                     edge_in_sem.at[slot])

    def edge_out(tile, chunk, slot):
        return pltpu.make_async_copy(oedge.at[slot], o_hbm.at[rows_of(tile, chunk, edge_rows), :],
                                     edge_out_sem.at[slot])

    def stage_inputs(s):
        t, j = coords(s)
        j_up = jnp.where(j == n_ff - 1, 0, j + 1)
        return (j, j_up, jnp.minimum(t + 1, n_tiles - 1), jnp.clip(t - 1, 0, n_tiles - 1),
                jnp.minimum(j, n_chunks - 1))

    def start_inputs(s, slot):
        j_down, j_up, t_pre, t_epi, chunk = stage_inputs(s)
        for c in weight_copies(j_down, j_up, slot):
            c.start()
        x_copy(xpbuf, xpsem, t_pre, chunk, slot).start()
        x_copy(xebuf, xesem, t_epi, chunk, slot).start()

    def wait_inputs(s, slot):
        j_down, j_up, t_pre, t_epi, chunk = stage_inputs(s)
        for c in weight_copies(j_down, j_up, slot):
            c.wait()
        x_copy(xpbuf, xpsem, t_pre, chunk, slot).wait()
        x_copy(xebuf, xesem, t_epi, chunk, slot).wait()

    def finish_rows(acc_rows, x_rows):
        out = x_rows + 0.5 * _rms(acc_rows, gpost_ref[...])
        if final_norm:
            out = _rms(out, gfin_ref[...])
        return out

    def up_chunk(h, slot):
        g = jnp.dot(h, wgbuf[slot], preferred_element_type=F32)
        u = jnp.dot(h, wubuf[slot], preferred_element_type=F32)
        return _silu_mul(g, u).astype(BF16)

    n_edge = tm // edge_rows
    edge_in(0, 0, 0).start()
    for c in range(n_edge):
        edge_in(0, c, c % 2).wait()
        if c + 1 < n_edge:
            edge_in(0, c + 1, (c + 1) % 2).start()
        hbuf[0, pl.ds(c * edge_rows, edge_rows), :] = _rms(xedge[c % 2], gpre_ref[...]).astype(BF16)
        accbuf[1, pl.ds(c * edge_rows, edge_rows), :] = jnp.zeros((edge_rows, accbuf.shape[2]), F32)
    for c in up_weight_copies(0, 2):
        c.start()
    for c in up_weight_copies(0, 2):
        c.wait()
    abuf[0] = up_chunk(hbuf[0], 2)
    start_inputs(0, 0)
    start_inputs(1, 1)

    def step(s, slot, a_slot):
        t, j = coords(s)
        chunk = stage_inputs(s)[4]
        cur = t % 2
        other = 1 - cur
        rows = pl.ds(pl.multiple_of(chunk * rc, rc), rc)
        hbuf[other, rows, :] = _rms(xpbuf[slot], gpre_ref[...]).astype(BF16)
        obuf[slot] = finish_rows(accbuf[other, rows, :], xebuf[slot])
        down = jnp.dot(abuf[a_slot], wdbuf[slot], preferred_element_type=F32)
        accbuf[cur] = jnp.where(j > 0, accbuf[cur], 0.0) + down
        h_sel = jnp.where(j == n_ff - 1, other, cur)
        abuf[1 - a_slot] = up_chunk(hbuf[h_sel], slot)

    def for_out_copies(i, base, fn):
        for k in range(2):
            t, j = coords(2 * i + k)
            _, _, _, t_epi, chunk = stage_inputs(2 * i + k)
            copy = out_copy(t_epi, chunk, base + k)
            pl.when(jnp.logical_and(jnp.logical_and(i >= 0, t >= 1), j < n_chunks))(functools.partial(fn, copy))

    def two_steps(i, carry):
        base = 2 * (i % 2)
        other_base = 2 - base
        s = 2 * i
        wait_inputs(s, base)
        wait_inputs(s + 1, base + 1)
        start_inputs(s + 2, other_base)
        start_inputs(s + 3, other_base + 1)
        for_out_copies(i - 2, base, lambda c: c.wait())
        for_out_copies(i - 1, other_base, lambda c: c.start())
        step(s, base, 0)
        step(s + 1, base + 1, 1)
        return carry

    n_iters = n_steps // 2
    lax.fori_loop(0, n_iters, two_steps, 0)

    end_base = 2 * (n_iters % 2)
    wait_inputs(n_steps, end_base)
    wait_inputs(n_steps + 1, end_base + 1)
    for_out_copies(jnp.int32(n_iters - 2), end_base, lambda c: c.wait())
    for_out_copies(jnp.int32(n_iters - 1), 2 - end_base, lambda c: c.start())
    for_out_copies(jnp.int32(n_iters - 1), 2 - end_base, lambda c: c.wait())
    last = n_tiles - 1
    edge_in(last, 0, 0).start()
    for c in range(n_edge):
        edge_in(last, c, c % 2).wait()
        if c + 1 < n_edge:
            edge_in(last, c + 1, (c + 1) % 2).start()
        if c >= 2:
            edge_out(last, c - 2, c % 2).wait()
        oedge[c % 2] = finish_rows(accbuf[last % 2, pl.ds(c * edge_rows, edge_rows), :], xedge[c % 2])
        edge_out(last, c, c % 2).start()
    for c in range(max(n_edge - 2, 0), n_edge):
        edge_out(last, c, c % 2).wait()


def _ffn(x, g_pre, wg, wu, wd, g_post, g_fin, *, final_norm, tm=512, rc=16, edge_rows=64):
    n_tok, d = x.shape
    n_ff, _, tf = wg.shape
    tm = min(tm, n_tok // 2)
    edge_rows = min(edge_rows, tm)
    n_tiles = n_tok // tm
    assert n_tok % tm == 0 and wd.shape[0] == n_ff * tf and tm % rc == 0 and tm % edge_rows == 0
    assert n_tiles % 2 == 0 and tm // rc <= n_ff and tm // edge_rows >= 2
    hbm = pl.BlockSpec(memory_space=pl.ANY)
    vec = pl.BlockSpec(memory_space=pltpu.VMEM)
    return pl.pallas_call(
        functools.partial(_ffn_body, tm=tm, tf=tf, rc=rc, edge_rows=edge_rows, n_tiles=n_tiles, n_ff=n_ff,
                          final_norm=final_norm),
        in_specs=[hbm, vec, hbm, hbm, hbm, vec, vec],
        out_specs=hbm,
        out_shape=jax.ShapeDtypeStruct((n_tok, d), F32),
        scratch_shapes=[
            pltpu.VMEM((2, tm, d), BF16),
            pltpu.VMEM((2, tm, d), F32),
            pltpu.VMEM((2, tm, tf), BF16),
            pltpu.VMEM((4, d, tf), BF16),
            pltpu.VMEM((4, d, tf), BF16),
            pltpu.VMEM((4, tf, d), BF16),
            pltpu.VMEM((4, rc, d), F32),
            pltpu.VMEM((4, rc, d), F32),
            pltpu.VMEM((4, rc, d), F32),
            pltpu.VMEM((2, edge_rows, d), F32),
            pltpu.VMEM((2, edge_rows, d), F32),
            pltpu.SemaphoreType.DMA((4, 3)),
            pltpu.SemaphoreType.DMA((4,)),
            pltpu.SemaphoreType.DMA((4,)),
            pltpu.SemaphoreType.DMA((4,)),
            pltpu.SemaphoreType.DMA((2,)),
            pltpu.SemaphoreType.DMA((2,)),
        ],
        compiler_params=pltpu.CompilerParams(vmem_limit_bytes=VMEM_LIMIT),
        name="ffn_stream",
    )(x, g_pre, wg, wu, wd, g_post, g_fin)


def _ffn_grid_body(x_ref, gpre_ref, wg_ref, wu_ref, wd_ref, gpost_ref, gfin_ref, o_ref, h_ref,
                   *, row_chunk, final_norm):
    j = pl.program_id(1)
    tm = x_ref.shape[0]

    @pl.when(j == 0)
    def _():
        def prologue(rows):
            h_ref[rows, :] = _rms(x_ref[rows, :], gpre_ref[...]).astype(BF16)
            o_ref[rows, :] = jnp.zeros((row_chunk, o_ref.shape[1]), F32)
        _for_row_chunks(tm, row_chunk, prologue)

    h = h_ref[...]
    g = jnp.dot(h, wg_ref[0], preferred_element_type=F32)
    u = jnp.dot(h, wu_ref[0], preferred_element_type=F32)
    o_ref[...] += jnp.dot(_silu_mul(g, u).astype(BF16), wd_ref[...], preferred_element_type=F32)

    @pl.when(j == pl.num_programs(1) - 1)
    def _():
        def epilogue(rows):
            out = x_ref[rows, :] + 0.5 * _rms(o_ref[rows, :], gpost_ref[...])
            if final_norm:
                out = _rms(out, gfin_ref[...])
            o_ref[rows, :] = out
        _for_row_chunks(tm, row_chunk, epilogue)


def _ffn_grid(x, g_pre, wg, wu, wd, g_post, g_fin, *, final_norm, tm=512):
    n_tok, d = x.shape
    n_ff, _, tf = wg.shape
    tm = min(tm, n_tok)
    assert n_tok % tm == 0
    vec = pl.BlockSpec((1, d), lambda i, j: (0, 0))
    return pl.pallas_call(
        functools.partial(_ffn_grid_body, row_chunk=min(64, tm), final_norm=final_norm),
        grid=(n_tok // tm, n_ff),
        in_specs=[
            pl.BlockSpec((tm, d), lambda i, j: (i, 0)),
            vec,
            pl.BlockSpec((1, d, tf), lambda i, j: (j, 0, 0)),
            pl.BlockSpec((1, d, tf), lambda i, j: (j, 0, 0)),
            pl.BlockSpec((tf, d), lambda i, j: (j, 0)),
            vec,
            vec,
        ],
        out_specs=pl.BlockSpec((tm, d), lambda i, j: (i, 0)),
        out_shape=jax.ShapeDtypeStruct((n_tok, d), F32),
        scratch_shapes=[pltpu.VMEM((tm, d), BF16)],
        compiler_params=_params(("parallel", "arbitrary")),
        name="ffn_grid",
    )(x, g_pre, wg, wu, wd, g_post, g_fin)


def _inproj_body(x_ref, g_ref, w_ref, h_ref, qkv_ref, u_ref, qc_ref, *, row_chunk, n_qkv, n_u):
    j = pl.program_id(1)
    tm = x_ref.shape[0]

    @pl.when(j == 0)
    def _():
        def prologue(rows):
            h_ref[rows, :] = _rms(x_ref[rows, :], g_ref[...]).astype(BF16)
        _for_row_chunks(tm, row_chunk, prologue)

    z = jnp.dot(h_ref[...], w_ref[...], preferred_element_type=F32)

    @pl.when(j < n_qkv)
    def _():
        qkv_ref[...] = z.astype(BF16)

    @pl.when(jnp.logical_and(j >= n_qkv, j < n_qkv + n_u))
    def _():
        u_ref[...] = z

    @pl.when(j >= n_qkv + n_u)
    def _():
        qc_ref[...] = z.astype(BF16)


def _inproj(x, g, w_in, *, na_width, pool_width, mem_width, tm=512, tn=1024):
    n_tok, d = x.shape
    tm = min(tm, n_tok)
    tn = min(tn, pool_width, mem_width)
    assert n_tok % tm == 0 and na_width % tn == 0 and pool_width % tn == 0 and mem_width % tn == 0
    n_qkv = 3 * na_width // tn
    n_u = pool_width // tn
    n_qc = mem_width // tn
    return pl.pallas_call(
        functools.partial(_inproj_body, row_chunk=min(64, tm), n_qkv=n_qkv, n_u=n_u),
        grid=(n_tok // tm, n_qkv + n_u + n_qc),
        in_specs=[
            pl.BlockSpec((tm, d), lambda i, j: (i, 0)),
            pl.BlockSpec((1, d), lambda i, j: (0, 0)),
            pl.BlockSpec((d, tn), lambda i, j: (0, j)),
        ],
        out_specs=[
            pl.BlockSpec((tm, d), lambda i, j: (i, 0)),
            pl.BlockSpec((tm, tn), lambda i, j: (i, jnp.minimum(j, n_qkv - 1))),
            pl.BlockSpec((tm, tn), lambda i, j: (i, jnp.clip(j - n_qkv, 0, n_u - 1))),
            pl.BlockSpec((tm, tn), lambda i, j: (i, jnp.clip(j - n_qkv - n_u, 0, n_qc - 1))),
        ],
        out_shape=[
            jax.ShapeDtypeStruct((n_tok, d), BF16),
            jax.ShapeDtypeStruct((n_tok, 3 * na_width), BF16),
            jax.ShapeDtypeStruct((n_tok, pool_width), F32),
            jax.ShapeDtypeStruct((n_tok, mem_width), BF16),
        ],
        compiler_params=_params(("parallel", "arbitrary")),
        name="inproj",
    )(x, g, w_in)


def _natten_bias_tables(rpb):
    n_heads, n_dr, n_dc = rpb.shape
    kh, kw = (n_dr + 1) // 2, (n_dc + 1) // 2
    cols = np.arange(GRID_W)
    col_start = np.clip(cols - kw // 2, 0, GRID_W - kw)
    col_valid = (cols[None, :] >= col_start[:, None]) & (cols[None, :] < col_start[:, None] + kw)
    lpad = GRID_W - kw
    padded = jnp.pad(rpb.astype(F32) * LOG2_E, ((0, 0), (0, 0), (lpad, 2 * GRID_W - 1 - n_dc - lpad)))
    by_col = jnp.stack([padded[:, :, GRID_W - 1 - c: 2 * GRID_W - 1 - c] for c in range(GRID_W)], axis=1)
    by_col = jnp.where(jnp.asarray(col_valid)[None, :, None, :], by_col, MASK_VALUE)
    rpad = NAT_K_ROWS - kh
    by_col = jnp.pad(by_col, ((0, 0), (0, 0), (rpad, 2 * NAT_K_ROWS - 1 - n_dr - rpad), (0, 0)),
                     constant_values=MASK_VALUE)
    k_rows = np.arange(NAT_K_ROWS)
    tables = []
    for t in range(3):
        blocks = []
        for ri in range(NAT_Q_ROWS):
            r = (NAT_K_ROWS - NAT_Q_ROWS) // 2 * t + ri
            row_start = min(max(r - kh // 2, 0), NAT_K_ROWS - kh)
            row_valid = (k_rows >= row_start) & (k_rows < row_start + kh)
            blk = by_col[:, :, NAT_K_ROWS - 1 - r: 2 * NAT_K_ROWS - 1 - r, :]
            blk = jnp.where(jnp.asarray(row_valid)[None, None, :, None], blk, MASK_VALUE)
            blocks.append(blk.reshape(n_heads, GRID_W, NAT_K_ROWS * GRID_W))
        tables.append(jnp.stack(blocks, axis=1).reshape(n_heads, NAT_Q_ROWS * GRID_W, NAT_K_ROWS * GRID_W))
    return jnp.stack(tables)


def _natten_body(q_ref, k0_ref, k1_ref, k2_ref, k3_ref, v0_ref, v1_ref, v2_ref, v3_ref, bias_ref, o_ref,
                 *, log2_scale, head_dim, n_sub):
    k_all = jnp.concatenate([k0_ref[0], k1_ref[0], k2_ref[0], k3_ref[0]], axis=0)
    v_all = jnp.concatenate([v0_ref[0], v1_ref[0], v2_ref[0], v3_ref[0]], axis=0)
    rows = q_ref.shape[1] // n_sub
    for h in range(q_ref.shape[2] // head_dim):
        cols = slice(h * head_dim, (h + 1) * head_dim)
        k = k_all[:, cols]
        v = v_all[:, cols]
        for i in range(n_sub):
            r = slice(i * rows, (i + 1) * rows)
            s = lax.dot_general(q_ref[0, r, cols], k, (((1,), (1,)), ((), ())), preferred_element_type=F32)
            s = s * log2_scale + bias_ref[0, h, r, :]
            m = jnp.max(s, axis=-1, keepdims=True)
            p = jnp.exp2(s - m)
            l = jnp.sum(p, axis=-1, keepdims=True)
            o = jnp.dot(p.astype(BF16), v, preferred_element_type=F32)
            o_ref[0, r, cols] = (o / l).astype(o_ref.dtype)


def _natten(qkv, bias, *, n_heads, head_dim, heads_per_step=2, n_sub=2):
    b, seq, _ = qkv.shape
    tq = NAT_Q_ROWS * GRID_W
    tk = NAT_K_CHUNK_ROWS * GRID_W
    n_kc = NAT_K_ROWS // NAT_K_CHUNK_ROWS
    hp = min(heads_per_step, n_heads)
    assert n_kc == 4 and seq % (NAT_K_ROWS * GRID_W) == 0 and n_heads % hp == 0
    n_hg = n_heads // hp
    n_qb = seq // tq
    n_kb = seq // tk
    lead = (NAT_K_ROWS - NAT_Q_ROWS) // 2 // NAT_K_CHUNK_ROWS

    def kv_spec(which, c):
        def index(bi, h, r):
            start = jnp.clip(r * (tq // tk) - lead, 0, n_kb - n_kc)
            return (bi, start + c, which * n_hg + h)
        return pl.BlockSpec((1, tk, hp * head_dim), index)

    def bias_index(bi, h, r):
        t = jnp.where(r == 0, 0, jnp.where(r == n_qb - 1, 2, 1))
        return (t, h, 0, 0)

    return pl.pallas_call(
        functools.partial(_natten_body, log2_scale=head_dim ** -0.5 * LOG2_E, head_dim=head_dim, n_sub=n_sub),
        grid=(b, n_hg, n_qb),
        in_specs=[pl.BlockSpec((1, tq, hp * head_dim), lambda bi, h, r: (bi, r, h))]
        + [kv_spec(1, c) for c in range(n_kc)]
        + [kv_spec(2, c) for c in range(n_kc)]
        + [pl.BlockSpec((1, hp, tq, NAT_K_ROWS * GRID_W), bias_index)],
        out_specs=pl.BlockSpec((1, tq, hp * head_dim), lambda bi, h, r: (bi, r, h)),
        out_shape=jax.ShapeDtypeStruct((b, seq, n_heads * head_dim), BF16),
        compiler_params=_params(("parallel", "parallel", "arbitrary")),
        name="natten",
    )(qkv, *([qkv] * (2 * n_kc)), bias)


def _pool_body(u_ref, prev_ref, next_ref, w_ref, scale_ref, o_ref, ext_ref, *, seq, halo, group_dim):
    t = pl.program_id(1)
    tt = u_ref.shape[1]
    u = u_ref[0]
    ext_ref[pl.ds(halo, tt), :] = u
    ext_ref[pl.ds(0, halo), :] = jnp.where(t > 0, prev_ref[0], 0.0)
    ext_ref[pl.ds(halo + tt, halo), :] = jnp.where(t < pl.num_programs(1) - 1, next_ref[0], 0.0)
    pos = t * tt + lax.broadcasted_iota(jnp.int32, (tt, 1), 0)
    for g, w in enumerate(POOL_WINDOWS):
        cols = pl.ds(g * group_dim, group_dim)
        total = ext_ref[pl.ds(halo - w // 2, tt), cols]
        for d in range(-w // 2 + 1, w // 2):
            total = total + ext_ref[pl.ds(halo + d, tt), cols]
        count = jnp.minimum(pos + w // 2, seq) - jnp.maximum(pos - w // 2, 0)
        pooled = total / count.astype(F32) - u[:, g * group_dim:(g + 1) * group_dim]
        mixed = jnp.dot(pooled.astype(BF16), w_ref[g], preferred_element_type=F32)
        o_ref[0, :, cols] = (mixed * scale_ref[:, cols]).astype(o_ref.dtype)


def _pool(u, w_pool, pool_scale, *, tt=512):
    b, seq, width = u.shape
    n_groups, group_dim, _ = w_pool.shape
    assert n_groups == len(POOL_WINDOWS)
    halo = 8
    assert max(POOL_WINDOWS) // 2 <= halo
    tt = min(tt, seq)
    assert seq % tt == 0 and tt % halo == 0
    hb = tt // halo
    return pl.pallas_call(
        functools.partial(_pool_body, seq=seq, halo=halo, group_dim=group_dim),
        grid=(b, seq // tt),
        in_specs=[
            pl.BlockSpec((1, tt, width), lambda bi, t: (bi, t, 0)),
            pl.BlockSpec((1, halo, width), lambda bi, t: (bi, jnp.maximum(t * hb - 1, 0), 0)),
            pl.BlockSpec((1, halo, width), lambda bi, t: (bi, jnp.minimum((t + 1) * hb, seq // halo - 1), 0)),
            pl.BlockSpec((n_groups, group_dim, group_dim), lambda bi, t: (0, 0, 0)),
            pl.BlockSpec((1, width), lambda bi, t: (0, 0)),
        ],
        out_specs=pl.BlockSpec((1, tt, width), lambda bi, t: (bi, t, 0)),
        out_shape=jax.ShapeDtypeStruct((b, seq, width), BF16),
        scratch_shapes=[pltpu.VMEM((tt + 2 * halo, width), F32)],
        compiler_params=_params(("parallel", "arbitrary")),
        name="pool",
    )(u, u, u, w_pool, pool_scale)


def _memkv_body(m_ref, g_ref, w_ref, o_ref):
    h = _rms(m_ref[...], g_ref[...]).astype(BF16)
    o_ref[...] = jnp.dot(h, w_ref[...], preferred_element_type=F32).astype(o_ref.dtype)


def _memkv(mem, g_mem, w_kv, *, tr=256, tn=512):
    n_rows, d = mem.shape
    width = w_kv.shape[1]
    tr = min(tr, n_rows)
    tn = min(tn, width)
    assert n_rows % tr == 0 and width % tn == 0
    return pl.pallas_call(
        _memkv_body,
        grid=(n_rows // tr, width // tn),
        in_specs=[
            pl.BlockSpec((tr, d), lambda i, j: (i, 0)),
            pl.BlockSpec((1, d), lambda i, j: (0, 0)),
            pl.BlockSpec((d, tn), lambda i, j: (0, j)),
        ],
        out_specs=pl.BlockSpec((tr, tn), lambda i, j: (i, j)),
        out_shape=jax.ShapeDtypeStruct((n_rows, width), BF16),
        compiler_params=_params(("parallel", "arbitrary")),
        name="memkv",
    )(mem, g_mem, w_kv)


def _memattn_body(q_ref, kv_ref, o_ref, *, n_heads, head_dim):
    scale = head_dim ** -0.5
    width = n_heads * head_dim
    for h in range(n_heads):
        cols = slice(h * head_dim, (h + 1) * head_dim)
        q = q_ref[0, :, cols]
        k = kv_ref[0, :, cols]
        v = kv_ref[0, :, width + h * head_dim: width + (h + 1) * head_dim]
        s = lax.dot_general(q, k, (((1,), (1,)), ((), ())), preferred_element_type=F32) * scale
        m = jnp.max(s, axis=-1, keepdims=True)
        p = jnp.exp(s - m)
        l = jnp.sum(p, axis=-1, keepdims=True)
        o = jnp.dot(p.astype(BF16), v, preferred_element_type=F32)
        o_ref[0, :, cols] = (o / l).astype(o_ref.dtype)


def _memattn(q, kv, *, n_heads, tq=512):
    b, seq, width = q.shape
    n_mem = kv.shape[1]
    tq = min(tq, seq)
    assert seq % tq == 0
    return pl.pallas_call(
        functools.partial(_memattn_body, n_heads=n_heads, head_dim=width // n_heads),
        grid=(b, seq // tq),
        in_specs=[
            pl.BlockSpec((1, tq, width), lambda bi, t: (bi, t, 0)),
            pl.BlockSpec((1, n_mem, 2 * width), lambda bi, t: (bi, 0, 0)),
        ],
        out_specs=pl.BlockSpec((1, tq, width), lambda bi, t: (bi, t, 0)),
        out_shape=jax.ShapeDtypeStruct((b, seq, width), BF16),
        compiler_params=_params(("parallel", "arbitrary")),
        name="memattn",
    )(q, kv)


def _merge_body(h_ref, a_ref, b_ref, c_ref, wga_ref, wgb_ref, wgc_ref, bg_ref, wa_ref, wb_ref, wc_ref, o_ref):
    h = h_ref[...]
    merged = None
    for i, (x_ref, wg_ref, w_ref) in enumerate(
            ((a_ref, wga_ref, wa_ref), (b_ref, wgb_ref, wb_ref), (c_ref, wgc_ref, wc_ref))):
        logits = jnp.dot(h, wg_ref[...], preferred_element_type=F32) + bg_ref[i:i + 1, :]
        gate = 1.0 / (1.0 + jnp.exp(-logits))
        y = jnp.dot(x_ref[...], w_ref[...], preferred_element_type=F32)
        merged = gate * y if merged is None else merged + gate * y
    o_ref[...] = merged.astype(o_ref.dtype)


def _merge(h, att_a, mix_b, att_c, w_in, gate_col0, b_gate, w_a, w_b, w_c, *, tm=512, tn=512):
    n_tok, d = h.shape
    tm = min(tm, n_tok)
    tn = min(tn, d)
    assert n_tok % tm == 0 and d % tn == 0 and gate_col0 % tn == 0
    nj = d // tn
    g0 = gate_col0 // tn

    def act(x):
        return pl.BlockSpec((tm, x.shape[1]), lambda i, j: (i, 0))

    def gate_w(branch):
        return pl.BlockSpec((d, tn), lambda i, j: (0, g0 + branch * nj + j))

    def out_w(w):
        return pl.BlockSpec((w.shape[0], tn), lambda i, j: (0, j))

    return pl.pallas_call(
        _merge_body,
        grid=(n_tok // tm, nj),
        in_specs=[act(h), act(att_a), act(mix_b), act(att_c),
                  gate_w(0), gate_w(1), gate_w(2),
                  pl.BlockSpec((b_gate.shape[0], tn), lambda i, j: (0, j)),
                  out_w(w_a), out_w(w_b), out_w(w_c)],
        out_specs=pl.BlockSpec((tm, tn), lambda i, j: (i, j)),
        out_shape=jax.ShapeDtypeStruct((n_tok, d), BF16),
        compiler_params=_params(("parallel", "arbitrary")),
        name="merge",
    )(h, att_a, mix_b, att_c, w_in, w_in, w_in, b_gate, w_a, w_b, w_c)


def _wo_body(x_ref, m_ref, w_ref, g_ref, o_ref, *, row_chunk):
    j = pl.program_id(1)
    tm = x_ref.shape[0]

    @pl.when(j == 0)
    def _():
        def prologue(rows):
            o_ref[rows, :] = jnp.zeros((row_chunk, o_ref.shape[1]), F32)
        _for_row_chunks(tm, row_chunk, prologue)

    o_ref[...] += jnp.dot(m_ref[...], w_ref[...], preferred_element_type=F32)

    @pl.when(j == pl.num_programs(1) - 1)
    def _():
        def epilogue(rows):
            o_ref[rows, :] = x_ref[rows, :] + _rms(o_ref[rows, :], g_ref[...])
        _for_row_chunks(tm, row_chunk, epilogue)


def _wo(x, merged, w_o, g_post, *, tm=512, tk=1024):
    n_tok, d = x.shape
    tm = min(tm, n_tok)
    tk = min(tk, d)
    assert n_tok % tm == 0 and d % tk == 0
    return pl.pallas_call(
        functools.partial(_wo_body, row_chunk=min(64, tm)),
        grid=(n_tok // tm, d // tk),
        in_specs=[
            pl.BlockSpec((tm, d), lambda i, j: (i, 0)),
            pl.BlockSpec((tm, tk), lambda i, j: (i, j)),
            pl.BlockSpec((tk, d), lambda i, j: (j, 0)),
            pl.BlockSpec((1, d), lambda i, j: (0, 0)),
        ],
        out_specs=pl.BlockSpec((tm, d), lambda i, j: (i, 0)),
        out_shape=jax.ShapeDtypeStruct((n_tok, d), F32),
        compiler_params=_params(("parallel", "arbitrary")),
        name="wo",
    )(x, merged, w_o, g_post)


def _layer(x, mem, p, bias_tables):
    b, seq, d = x.shape
    n_tok = b * seq
    na_heads = p["rpb"].shape[0]
    n_groups, group_dim, _ = p["w_pool"].shape
    pool_width = n_groups * group_dim
    mem_width = p["w_mem_kv"].shape[1] // 2
    na_width = p["w_a_out"].shape[0]
    gate_col0 = 3 * na_width + pool_width + mem_width

    x0 = x.reshape(n_tok, d)
    x1 = _ffn(x0, p["g_ffn1_pre"], p["w1_gate"], p["w1_up"], p["w1_down"], p["g_ffn1_post"],
              p["g_final"], final_norm=False)
    h, qkv, u, qc = _inproj(x1, p["g_mix_pre"], p["w_in"],
                            na_width=na_width, pool_width=pool_width, mem_width=mem_width)
    att_a = _natten(qkv.reshape(b, seq, 3 * na_width), bias_tables,
                    n_heads=na_heads, head_dim=na_width // na_heads)
    mix_b = _pool(u.reshape(b, seq, pool_width), p["w_pool"], p["pool_scale"])
    kv = _memkv(mem.reshape(b * mem.shape[1], d), p["g_mem"], p["w_mem_kv"])
    att_c = _memattn(qc.reshape(b, seq, mem_width), kv.reshape(b, mem.shape[1], 2 * mem_width),
                     n_heads=MEM_HEADS)
    merged = _merge(h, att_a.reshape(n_tok, na_width), mix_b.reshape(n_tok, pool_width),
                    att_c.reshape(n_tok, mem_width), p["w_in"], gate_col0, p["b_gate"],
                    p["w_a_out"], p["w_b_out"], p["w_c_out"])
    x2 = _wo(x1, merged, p["w_o"], p["g_mix_post"])
    y = _ffn_grid(x2, p["g_ffn2_pre"], p["w2_gate"], p["w2_up"], p["w2_down"], p["g_ffn2_post"],
             p["g_final"], final_norm=True)
    return y.reshape(b, seq, d)


_MATMUL_WEIGHTS = ("w1_gate", "w1_up", "w1_down", "w_in", "w_pool", "w_mem_kv",
                   "w_a_out", "w_b_out", "w_c_out", "w_o", "w2_gate", "w2_up", "w2_down")
_FFN_UP_WEIGHTS = ("w1_gate", "w1_up", "w2_gate", "w2_up")
_ROW_VECTORS = ("g_ffn1_pre", "g_ffn1_post", "g_mix_pre", "pool_scale", "g_mem", "g_mix_post",
                "g_ffn2_pre", "g_ffn2_post", "g_final")


def kernel(x_prompt, x_sample, mem_prompt, mem_sample, g_ffn1_pre, w1_gate, w1_up, w1_down, g_ffn1_post, g_mix_pre, w_in, rpb, w_pool, pool_scale, g_mem, w_mem_kv, w_a_out, w_b_out, w_c_out, b_gate, w_o, g_mix_post, g_ffn2_pre, w2_gate, w2_up, w2_down, g_ffn2_post, g_final):
    stacked = dict(g_ffn1_pre=g_ffn1_pre, w1_gate=w1_gate, w1_up=w1_up, w1_down=w1_down,
                   g_ffn1_post=g_ffn1_post, g_mix_pre=g_mix_pre, w_in=w_in, rpb=rpb, w_pool=w_pool,
                   pool_scale=pool_scale, g_mem=g_mem, w_mem_kv=w_mem_kv, w_a_out=w_a_out,
                   w_b_out=w_b_out, w_c_out=w_c_out, b_gate=b_gate, w_o=w_o, g_mix_post=g_mix_post,
                   g_ffn2_pre=g_ffn2_pre, w2_gate=w2_gate, w2_up=w2_up, w2_down=w2_down,
                   g_ffn2_post=g_ffn2_post, g_final=g_final)
    y_prompt, y_sample = x_prompt, x_sample
    for layer in range(g_final.shape[0]):
        p = {name: a[layer] for name, a in stacked.items()}
        for name in _MATMUL_WEIGHTS:
            p[name] = p[name].astype(BF16)
        for name in _FFN_UP_WEIGHTS:
            w = p[name]
            p[name] = w.reshape(w.shape[0], w.shape[1] // FFN_CHUNK, FFN_CHUNK).transpose(1, 0, 2)
        for name in _ROW_VECTORS:
            p[name] = p[name].reshape(1, -1)
        bias_tables = _natten_bias_tables(p["rpb"])
        y_prompt = _layer(y_prompt, mem_prompt, p, bias_tables)
        y_sample = _layer(y_sample, mem_sample, p, bias_tables)
    return (y_prompt, y_sample)
```

```python
import functools

import numpy as np
import jax
import jax.numpy as jnp
from jax import lax
from jax.experimental import pallas as pl
from jax.experimental.pallas import tpu as pltpu

GRID_W = 64
POOL_WINDOWS = (2, 4, 8, 16)
MEM_HEADS = 4
RMS_EPS = 1e-6
MASK_VALUE = -1e30
LOG2_E = 1.4426950408889634

FFN_CHUNK = 256
NAT_Q_ROWS = 8
NAT_K_ROWS = 16
NAT_K_CHUNK_ROWS = 4

V7X_VMEM_BYTES = 64 * 1024 * 1024
VMEM_LIMIT = V7X_VMEM_BYTES - 6 * 1024 * 1024

F32 = jnp.float32
BF16 = jnp.bfloat16


def _rms(x, g):
    ms = jnp.mean(x * x, axis=-1, keepdims=True)
    return (x * lax.rsqrt(ms + RMS_EPS)) * g


def _for_row_chunks(n_rows, chunk, fn):
    def body(c, carry):
        fn(pl.ds(pl.multiple_of(c * chunk, chunk), chunk))
        return carry
    lax.fori_loop(0, n_rows // chunk, body, 0)


def _params(sem):
    return pltpu.CompilerParams(dimension_semantics=sem, vmem_limit_bytes=VMEM_LIMIT)


def _silu_mul(g, u):
    return (g * (1.0 / (1.0 + jnp.exp(-g)))) * u


def _ffn_body(x_hbm, gpre_ref, wg_hbm, wu_hbm, wd_hbm, gpost_ref, gfin_ref, o_hbm,
              hbuf, accbuf, abuf, wgbuf, wubuf, wdbuf, xpbuf, xebuf, obuf, xedge, oedge,
              wsem, xpsem, xesem, osem, edge_in_sem, edge_out_sem,
              *, tm, tf, rc, edge_rows, n_tiles, n_ff, final_norm):
    n_chunks = tm // rc
    n_steps = n_tiles * n_ff

    def coords(s):
        t = s // n_ff
        return t, s - t * n_ff

    def up_weight_copies(j_up, slot):
        return [pltpu.make_async_copy(wg_hbm.at[j_up], wgbuf.at[slot], wsem.at[slot, 0]),
                pltpu.make_async_copy(wu_hbm.at[j_up], wubuf.at[slot], wsem.at[slot, 1])]

    def weight_copies(j_down, j_up, slot):
        down_rows = pl.ds(pl.multiple_of(j_down * tf, tf), tf)
        return up_weight_copies(j_up, slot) + [
            pltpu.make_async_copy(wd_hbm.at[down_rows, :], wdbuf.at[slot], wsem.at[slot, 2])]

    def rows_of(tile, chunk, n_rows):
        return pl.ds(pl.multiple_of(tile * tm + chunk * n_rows, n_rows), n_rows)

    def x_copy(buf, sem, tile, chunk, slot):
        return pltpu.make_async_copy(x_hbm.at[rows_of(tile, chunk, rc), :], buf.at[slot], sem.at[slot])

    def out_copy(tile, chunk, slot):
        return pltpu.make_async_copy(obuf.at[slot], o_hbm.at[rows_of(tile, chunk, rc), :], osem.at[slot])

    def edge_in(tile, chunk, slot):
        return pltpu.make_async_copy(x_hbm.at[rows_of(tile, chunk, edge_rows), :], xedge.at[slot],
                                     edge_in_sem.at[slot])

    def edge_out(tile, chunk, slot):
        return pltpu.make_async_copy(oedge.at[slot], o_hbm.at[rows_of(tile, chunk, edge_rows), :],
                                     edge_out_sem.at[slot])

    def stage_inputs(s):
        t, j = coords(s)
        j_up = jnp.where(j == n_ff - 1, 0, j + 1)
        return (j, j_up, jnp.minimum(t + 1, n_tiles - 1), jnp.clip(t - 1, 0, n_tiles - 1),
                jnp.minimum(j, n_chunks - 1))

    def start_inputs(s, slot):
        j_down, j_up, t_pre, t_epi, chunk = stage_inputs(s)
        for c in weight_copies(j_down, j_up, slot):
            c.start()
        x_copy(xpbuf, xpsem, t_pre, chunk, slot).start()
        x_copy(xebuf, xesem, t_epi, chunk, slot).start()

    def wait_inputs(s, slot):
        j_down, j_up, t_pre, t_epi, chunk = stage_inputs(s)
        for c in weight_copies(j_down, j_up, slot):
            c.wait()
        x_copy(xpbuf, xpsem, t_pre, chunk, slot).wait()
        x_copy(xebuf, xesem, t_epi, chunk, slot).wait()

    def finish_rows(acc_rows, x_rows):
        out = x_rows + 0.5 * _rms(acc_rows, gpost_ref[...])
        if final_norm:
            out = _rms(out, gfin_ref[...])
        return out

    def up_chunk(h, slot):
        g = jnp.dot(h, wgbuf[slot], preferred_element_type=F32)
        u = jnp.dot(h, wubuf[slot], preferred_element_type=F32)
        return _silu_mul(g, u).astype(BF16)

    n_edge = tm // edge_rows
    edge_in(0, 0, 0).start()
    for c in range(n_edge):
        edge_in(0, c, c % 2).wait()
        if c + 1 < n_edge:
            edge_in(0, c + 1, (c + 1) % 2).start()
        hbuf[0, pl.ds(c * edge_rows, edge_rows), :] = _rms(xedge[c % 2], gpre_ref[...]).astype(BF16)
        accbuf[1, pl.ds(c * edge_rows, edge_rows), :] = jnp.zeros((edge_rows, accbuf.shape[2]), F32)
    for c in up_weight_copies(0, 2):
        c.start()
    for c in up_weight_copies(0, 2):
        c.wait()
    abuf[0] = up_chunk(hbuf[0], 2)
    start_inputs(0, 0)
    start_inputs(1, 1)

    def step(s, slot, a_slot):
        t, j = coords(s)
        chunk = stage_inputs(s)[4]
        cur = t % 2
        other = 1 - cur
        rows = pl.ds(pl.multiple_of(chunk * rc, rc), rc)
        hbuf[other, rows, :] = _rms(xpbuf[slot], gpre_ref[...]).astype(BF16)
        obuf[slot] = finish_rows(accbuf[other, rows, :], xebuf[slot])
        down = jnp.dot(abuf[a_slot], wdbuf[slot], preferred_element_type=F32)
        accbuf[cur] = jnp.where(j > 0, accbuf[cur], 0.0) + down
        h_sel = jnp.where(j == n_ff - 1, other, cur)
        abuf[1 - a_slot] = up_chunk(hbuf[h_sel], slot)

    def for_out_copies(i, base, fn):
        for k in range(2):
            t, j = coords(2 * i + k)
            _, _, _, t_epi, chunk = stage_inputs(2 * i + k)
            copy = out_copy(t_epi, chunk, base + k)
            pl.when(jnp.logical_and(jnp.logical_and(i >= 0, t >= 1), j < n_chunks))(functools.partial(fn, copy))

    def two_steps(i, carry):
        base = 2 * (i % 2)
        other_base = 2 - base
        s = 2 * i
        wait_inputs(s, base)
        wait_inputs(s + 1, base + 1)
        start_inputs(s + 2, other_base)
        start_inputs(s + 3, other_base + 1)
        for_out_copies(i - 2, base, lambda c: c.wait())
        for_out_copies(i - 1, other_base, lambda c: c.start())
        step(s, base, 0)
        step(s + 1, base + 1, 1)
        return carry

    n_iters = n_steps // 2
    lax.fori_loop(0, n_iters, two_steps, 0)

    end_base = 2 * (n_iters % 2)
    wait_inputs(n_steps, end_base)
    wait_inputs(n_steps + 1, end_base + 1)
    for_out_copies(jnp.int32(n_iters - 2), end_base, lambda c: c.wait())
    for_out_copies(jnp.int32(n_iters - 1), 2 - end_base, lambda c: c.start())
    for_out_copies(jnp.int32(n_iters - 1), 2 - end_base, lambda c: c.wait())
    last = n_tiles - 1
    edge_in(last, 0, 0).start()
    for c in range(n_edge):
        edge_in(last, c, c % 2).wait()
        if c + 1 < n_edge:
            edge_in(last, c + 1, (c + 1) % 2).start()
        if c >= 2:
            edge_out(last, c - 2, c % 2).wait()
        oedge[c % 2] = finish_rows(accbuf[last % 2, pl.ds(c * edge_rows, edge_rows), :], xedge[c % 2])
        edge_out(last, c, c % 2).start()
    for c in range(max(n_edge - 2, 0), n_edge):
        edge_out(last, c, c % 2).wait()


def _ffn(x, g_pre, wg, wu, wd, g_post, g_fin, *, final_norm, tm=512, rc=16, edge_rows=64):
    n_tok, d = x.shape
    n_ff, _, tf = wg.shape
    tm = min(tm, n_tok // 2)
    edge_rows = min(edge_rows, tm)
    n_tiles = n_tok // tm
    assert n_tok % tm == 0 and wd.shape[0] == n_ff * tf and tm % rc == 0 and tm % edge_rows == 0
    assert n_tiles % 2 == 0 and tm // rc <= n_ff and tm // edge_rows >= 2
    hbm = pl.BlockSpec(memory_space=pl.ANY)
    vec = pl.BlockSpec(memory_space=pltpu.VMEM)
    return pl.pallas_call(
        functools.partial(_ffn_body, tm=tm, tf=tf, rc=rc, edge_rows=edge_rows, n_tiles=n_tiles, n_ff=n_ff,
                          final_norm=final_norm),
        in_specs=[hbm, vec, hbm, hbm, hbm, vec, vec],
        out_specs=hbm,
        out_shape=jax.ShapeDtypeStruct((n_tok, d), F32),
        scratch_shapes=[
            pltpu.VMEM((2, tm, d), BF16),
            pltpu.VMEM((2, tm, d), F32),
            pltpu.VMEM((2, tm, tf), BF16),
            pltpu.VMEM((4, d, tf), BF16),
            pltpu.VMEM((4, d, tf), BF16),
            pltpu.VMEM((4, tf, d), BF16),
            pltpu.VMEM((4, rc, d), F32),
            pltpu.VMEM((4, rc, d), F32),
            pltpu.VMEM((4, rc, d), F32),
            pltpu.VMEM((2, edge_rows, d), F32),
            pltpu.VMEM((2, edge_rows, d), F32),
            pltpu.SemaphoreType.DMA((4, 3)),
            pltpu.SemaphoreType.DMA((4,)),
            pltpu.SemaphoreType.DMA((4,)),
            pltpu.SemaphoreType.DMA((4,)),
            pltpu.SemaphoreType.DMA((2,)),
            pltpu.SemaphoreType.DMA((2,)),
        ],
        compiler_params=pltpu.CompilerParams(vmem_limit_bytes=VMEM_LIMIT),
        name="ffn_stream",
    )(x, g_pre, wg, wu, wd, g_post, g_fin)


def _ffn_grid_body(x_ref, gpre_ref, wg_ref, wu_ref, wd_ref, gpost_ref, gfin_ref, o_ref, h_ref,
                   *, row_chunk, final_norm):
    j = pl.program_id(1)
    tm = x_ref.shape[0]

    @pl.when(j == 0)
    def _():
        def prologue(rows):
            h_ref[rows, :] = _rms(x_ref[rows, :], gpre_ref[...]).astype(BF16)
            o_ref[rows, :] = jnp.zeros((row_chunk, o_ref.shape[1]), F32)
        _for_row_chunks(tm, row_chunk, prologue)

    h = h_ref[...]
    g = jnp.dot(h, wg_ref[0], preferred_element_type=F32)
    u = jnp.dot(h, wu_ref[0], preferred_element_type=F32)
    o_ref[...] += jnp.dot(_silu_mul(g, u).astype(BF16), wd_ref[...], preferred_element_type=F32)

    @pl.when(j == pl.num_programs(1) - 1)
    def _():
        def epilogue(rows):
            out = x_ref[rows, :] + 0.5 * _rms(o_ref[rows, :], gpost_ref[...])
            if final_norm:
                out = _rms(out, gfin_ref[...])
            o_ref[rows, :] = out
        _for_row_chunks(tm, row_chunk, epilogue)


def _ffn_grid(x, g_pre, wg, wu, wd, g_post, g_fin, *, final_norm, tm=512):
    n_tok, d = x.shape
    n_ff, _, tf = wg.shape
    tm = min(tm, n_tok)
    assert n_tok % tm == 0
    vec = pl.BlockSpec((1, d), lambda i, j: (0, 0))
    return pl.pallas_call(
        functools.partial(_ffn_grid_body, row_chunk=min(64, tm), final_norm=final_norm),
        grid=(n_tok // tm, n_ff),
        in_specs=[
            pl.BlockSpec((tm, d), lambda i, j: (i, 0)),
            vec,
            pl.BlockSpec((1, d, tf), lambda i, j: (j, 0, 0)),
            pl.BlockSpec((1, d, tf), lambda i, j: (j, 0, 0)),
            pl.BlockSpec((tf, d), lambda i, j: (j, 0)),
            vec,
            vec,
        ],
        out_specs=pl.BlockSpec((tm, d), lambda i, j: (i, 0)),
        out_shape=jax.ShapeDtypeStruct((n_tok, d), F32),
        scratch_shapes=[pltpu.VMEM((tm, d), BF16)],
        compiler_params=_params(("parallel", "arbitrary")),
        name="ffn_grid",
    )(x, g_pre, wg, wu, wd, g_post, g_fin)


def _ffn_pair_body(xa_hbm, xb_hbm, gpre_ref, wg_hbm, wu_hbm, wd_hbm, gpost_ref, gfin_ref, oa_hbm, ob_hbm,
                   hbuf, accbuf, gbuf, ubuf, abuf, wgbuf, wubuf, wdbuf, xpbuf, xebuf, obuf, xedge, oedge,
                   wsem, xpsem, xesem, osem, edge_in_sem, edge_out_sem,
                   *, tm, tf, rc, edge_rows, m_split, n_a, n_tiles, n_ff, final_norm):
    n_chunks = tm // rc
    n_steps = n_tiles * n_ff
    n_edge = tm // edge_rows

    def coords(s):
        t = s // n_ff
        return t, s - t * n_ff

    def up_weight_copies(j_up, slot):
        cols = pl.ds(pl.multiple_of(j_up * tf, tf), tf)
        return [pltpu.make_async_copy(wg_hbm.at[:, cols], wgbuf.at[slot], wsem.at[slot, 0]),
                pltpu.make_async_copy(wu_hbm.at[:, cols], wubuf.at[slot], wsem.at[slot, 1])]

    def weight_copies(j_down, j_up, slot):
        rows = pl.ds(pl.multiple_of(j_down * tf, tf), tf)
        return up_weight_copies(j_up, slot) + [
            pltpu.make_async_copy(wd_hbm.at[rows, :], wdbuf.at[slot], wsem.at[slot, 2])]

    def start_rows_in(row0, n_rows, dst, sem):
        row0 = jnp.asarray(row0, jnp.int32)
        @pl.when(row0 < n_a)
        def _():
            pltpu.make_async_copy(xa_hbm.at[pl.ds(pl.multiple_of(row0, n_rows), n_rows), :], dst, sem).start()

        @pl.when(row0 >= n_a)
        def _():
            pltpu.make_async_copy(xb_hbm.at[pl.ds(pl.multiple_of(row0 - n_a, n_rows), n_rows), :], dst, sem).start()

    def wait_rows_in(n_rows, dst, sem):
        pltpu.make_async_copy(xa_hbm.at[pl.ds(0, n_rows), :], dst, sem).wait()

    def start_rows_out(row0, n_rows, src, sem):
        row0 = jnp.asarray(row0, jnp.int32)
        @pl.when(row0 < n_a)
        def _():
            pltpu.make_async_copy(src, oa_hbm.at[pl.ds(pl.multiple_of(row0, n_rows), n_rows), :], sem).start()

        @pl.when(row0 >= n_a)
        def _():
            pltpu.make_async_copy(src, ob_hbm.at[pl.ds(pl.multiple_of(row0 - n_a, n_rows), n_rows), :], sem).start()

    def wait_rows_out(n_rows, src, sem):
        pltpu.make_async_copy(src, oa_hbm.at[pl.ds(0, n_rows), :], sem).wait()

    def stage_inputs(s):
        t, j = coords(s)
        j_up = jnp.where(j == n_ff - 1, 0, j + 1)
        chunk = jnp.minimum(j, n_chunks - 1)
        row_pre = jnp.minimum(t + 1, n_tiles - 1) * tm + chunk * rc
        row_epi = jnp.clip(t - 1, 0, n_tiles - 1) * tm + chunk * rc
        return j, j_up, row_pre, row_epi, chunk

    def start_inputs(s, slot):
        j_down, j_up, row_pre, row_epi, _ = stage_inputs(s)
        for c in weight_copies(j_down, j_up, slot):
            c.start()
        start_rows_in(row_pre, rc, xpbuf.at[slot], xpsem.at[slot])
        start_rows_in(row_epi, rc, xebuf.at[slot], xesem.at[slot])

    def wait_inputs(slot):
        for c in weight_copies(0, 0, slot):
            c.wait()
        wait_rows_in(rc, xpbuf.at[slot], xpsem.at[slot])
        wait_rows_in(rc, xebuf.at[slot], xesem.at[slot])

    def finishes_rows(s):
        t, j = coords(s)
        return jnp.logical_and(jnp.logical_and(s >= 0, t >= 1), j < n_chunks)

    def finish_rows(acc_rows, x_rows):
        out = x_rows + 0.5 * _rms(acc_rows, gpost_ref[...])
        if final_norm:
            out = _rms(out, gfin_ref[...])
        return out

    m_rows = tm // m_split

    def up_chunk(h_sel, slot):
        for m in range(m_split):
            rows = pl.ds(m * m_rows, m_rows)
            h = hbuf[h_sel, rows, :]
            gbuf[rows, :] = jnp.dot(h, wgbuf[slot], preferred_element_type=F32)
            ubuf[rows, :] = jnp.dot(h, wubuf[slot], preferred_element_type=F32)

    start_rows_in(0, edge_rows, xedge.at[0], edge_in_sem.at[0])
    for c in range(n_edge):
        wait_rows_in(edge_rows, xedge.at[c % 2], edge_in_sem.at[c % 2])
        if c + 1 < n_edge:
            start_rows_in((c + 1) * edge_rows, edge_rows, xedge.at[(c + 1) % 2], edge_in_sem.at[(c + 1) % 2])
        hbuf[0, pl.ds(c * edge_rows, edge_rows), :] = _rms(xedge[c % 2], gpre_ref[...]).astype(BF16)
        accbuf[1, pl.ds(c * edge_rows, edge_rows), :] = jnp.zeros((edge_rows, accbuf.shape[2]), F32)
    for c in up_weight_copies(0, 1):
        c.start()
    for c in up_weight_copies(0, 1):
        c.wait()
    up_chunk(0, 1)
    start_inputs(0, 0)

    def step(s, carry):
        t, j = coords(s)
        chunk = stage_inputs(s)[4]
        slot = s % 2
        cur = t % 2
        other = 1 - cur
        wait_inputs(slot)
        start_inputs(s + 1, 1 - slot)

        @pl.when(finishes_rows(s - 2))
        def _():
            wait_rows_out(rc, obuf.at[slot], osem.at[slot])

        @pl.when(finishes_rows(s - 1))
        def _():
            start_rows_out(stage_inputs(s - 1)[3], rc, obuf.at[1 - slot], osem.at[1 - slot])

        @pl.when(j == 0)
        def _():
            def zero(zrows):
                accbuf[cur, zrows, :] = jnp.zeros((edge_rows, accbuf.shape[2]), F32)
            _for_row_chunks(tm, edge_rows, zero)

        abuf[...] = _silu_mul(gbuf[...], ubuf[...]).astype(BF16)
        rows = pl.ds(pl.multiple_of(chunk * rc, rc), rc)
        hbuf[other, rows, :] = _rms(xpbuf[slot], gpre_ref[...]).astype(BF16)
        obuf[slot] = finish_rows(accbuf[other, rows, :], xebuf[slot])
        for m in range(m_split):
            mrows = pl.ds(m * m_rows, m_rows)
            accbuf[cur, mrows, :] += jnp.dot(abuf[mrows, :], wdbuf[slot], preferred_element_type=F32)
        up_chunk(jnp.where(j == n_ff - 1, other, cur), slot)
        return carry

    lax.fori_loop(0, n_steps, step, 0)

    end_slot = n_steps % 2
    wait_inputs(end_slot)

    @pl.when(finishes_rows(jnp.int32(n_steps - 2)))
    def _():
        wait_rows_out(rc, obuf.at[end_slot], osem.at[end_slot])

    @pl.when(finishes_rows(jnp.int32(n_steps - 1)))
    def _():
        start_rows_out(stage_inputs(jnp.int32(n_steps - 1))[3], rc, obuf.at[1 - end_slot], osem.at[1 - end_slot])
        wait_rows_out(rc, obuf.at[1 - end_slot], osem.at[1 - end_slot])

    last_row0 = (n_tiles - 1) * tm
    start_rows_in(jnp.int32(last_row0), edge_rows, xedge.at[0], edge_in_sem.at[0])
    for c in range(n_edge):
        wait_rows_in(edge_rows, xedge.at[c % 2], edge_in_sem.at[c % 2])
        if c + 1 < n_edge:
            start_rows_in(jnp.int32(last_row0 + (c + 1) * edge_rows), edge_rows, xedge.at[(c + 1) % 2],
                          edge_in_sem.at[(c + 1) % 2])
        if c >= 2:
            wait_rows_out(edge_rows, oedge.at[c % 2], edge_out_sem.at[c % 2])
        oedge[c % 2] = finish_rows(accbuf[(n_tiles - 1) % 2, pl.ds(c * edge_rows, edge_rows), :], xedge[c % 2])
        start_rows_out(jnp.int32(last_row0 + c * edge_rows), edge_rows, oedge.at[c % 2], edge_out_sem.at[c % 2])
    for c in range(max(n_edge - 2, 0), n_edge):
        wait_rows_out(edge_rows, oedge.at[c % 2], edge_out_sem.at[c % 2])


def _ffn_pair(xa, xb, g_pre, wg, wu, wd, g_post, g_fin, *, final_norm, tm=768, tf=FFN_CHUNK, rc=32, edge_rows=32, m_split=1):
    (n_a, d), n_b = xa.shape, xb.shape[0]
    d_ff = wg.shape[1]
    n_tok = n_a + n_b
    tm = min(tm, n_tok // 2)
    tf = min(tf, d_ff)
    edge_rows = min(edge_rows, tm)
    n_tiles, n_ff = n_tok // tm, d_ff // tf
    assert n_tok % tm == 0 and d_ff % tf == 0 and tm % rc == 0 and tm % edge_rows == 0
    assert n_a % edge_rows == 0 and n_a % rc == 0 and tm // rc <= n_ff and tm // edge_rows >= 2 and n_tiles >= 2
    hbm = pl.BlockSpec(memory_space=pl.ANY)
    vec = pl.BlockSpec(memory_space=pltpu.VMEM)
    return pl.pallas_call(
        functools.partial(_ffn_pair_body, tm=tm, tf=tf, rc=rc, edge_rows=edge_rows, m_split=m_split, n_a=n_a, n_tiles=n_tiles,
                          n_ff=n_ff, final_norm=final_norm),
        in_specs=[hbm, hbm, vec, hbm, hbm, hbm, vec, vec],
        out_specs=[hbm, hbm],
        out_shape=[jax.ShapeDtypeStruct((n_a, d), F32), jax.ShapeDtypeStruct((n_b, d), F32)],
        scratch_shapes=[
            pltpu.VMEM((2, tm, d), BF16),
            pltpu.VMEM((2, tm, d), F32),
            pltpu.VMEM((tm, tf), F32),
            pltpu.VMEM((tm, tf), F32),
            pltpu.VMEM((tm, tf), BF16),
            pltpu.VMEM((2, d, tf), BF16),
            pltpu.VMEM((2, d, tf), BF16),
            pltpu.VMEM((2, tf, d), BF16),
            pltpu.VMEM((2, rc, d), F32),
            pltpu.VMEM((2, rc, d), F32),
            pltpu.VMEM((2, rc, d), F32),
            pltpu.VMEM((2, edge_rows, d), F32),
            pltpu.VMEM((2, edge_rows, d), F32),
            pltpu.SemaphoreType.DMA((2, 3)),
            pltpu.SemaphoreType.DMA((2,)),
            pltpu.SemaphoreType.DMA((2,)),
            pltpu.SemaphoreType.DMA((2,)),
            pltpu.SemaphoreType.DMA((2,)),
            pltpu.SemaphoreType.DMA((2,)),
        ],
        compiler_params=pltpu.CompilerParams(vmem_limit_bytes=VMEM_LIMIT),
        name="ffn_pair",
    )(xa, xb, g_pre, wg, wu, wd, g_post, g_fin)


def _inproj_body(x_ref, g_ref, w_ref, h_ref, qkv_ref, u_ref, qc_ref, *, row_chunk, n_qkv, n_u):
    j = pl.program_id(1)
    tm = x_ref.shape[0]

    @pl.when(j == 0)
    def _():
        def prologue(rows):
            h_ref[rows, :] = _rms(x_ref[rows, :], g_ref[...]).astype(BF16)
        _for_row_chunks(tm, row_chunk, prologue)

    z = jnp.dot(h_ref[...], w_ref[...], preferred_element_type=F32)

    @pl.when(j < n_qkv)
    def _():
        qkv_ref[...] = z.astype(BF16)

    @pl.when(jnp.logical_and(j >= n_qkv, j < n_qkv + n_u))
    def _():
        u_ref[...] = z

    @pl.when(j >= n_qkv + n_u)
    def _():
        qc_ref[...] = z.astype(BF16)


def _inproj(x, g, w_in, *, na_width, pool_width, mem_width, tm=512, tn=1024):
    n_tok, d = x.shape
    tm = min(tm, n_tok)
    tn = min(tn, pool_width, mem_width)
    assert n_tok % tm == 0 and na_width % tn == 0 and pool_width % tn == 0 and mem_width % tn == 0
    n_qkv = 3 * na_width // tn
    n_u = pool_width // tn
    n_qc = mem_width // tn
    return pl.pallas_call(
        functools.partial(_inproj_body, row_chunk=min(64, tm), n_qkv=n_qkv, n_u=n_u),
        grid=(n_tok // tm, n_qkv + n_u + n_qc),
        in_specs=[
            pl.BlockSpec((tm, d), lambda i, j: (i, 0)),
            pl.BlockSpec((1, d), lambda i, j: (0, 0)),
            pl.BlockSpec((d, tn), lambda i, j: (0, j)),
        ],
        out_specs=[
            pl.BlockSpec((tm, d), lambda i, j: (i, 0)),
            pl.BlockSpec((tm, tn), lambda i, j: (i, jnp.minimum(j, n_qkv - 1))),
            pl.BlockSpec((tm, tn), lambda i, j: (i, jnp.clip(j - n_qkv, 0, n_u - 1))),
            pl.BlockSpec((tm, tn), lambda i, j: (i, jnp.clip(j - n_qkv - n_u, 0, n_qc - 1))),
        ],
        out_shape=[
            jax.ShapeDtypeStruct((n_tok, d), BF16),
            jax.ShapeDtypeStruct((n_tok, 3 * na_width), BF16),
            jax.ShapeDtypeStruct((n_tok, pool_width), F32),
            jax.ShapeDtypeStruct((n_tok, mem_width), BF16),
        ],
        compiler_params=_params(("parallel", "arbitrary")),
        name="inproj",
    )(x, g, w_in)


def _natten_bias_tables(rpb):
    n_heads, n_dr, n_dc = rpb.shape
    kh, kw = (n_dr + 1) // 2, (n_dc + 1) // 2
    cols = np.arange(GRID_W)
    col_start = np.clip(cols - kw // 2, 0, GRID_W - kw)
    col_valid = (cols[None, :] >= col_start[:, None]) & (cols[None, :] < col_start[:, None] + kw)
    lpad = GRID_W - kw
    padded = jnp.pad(rpb.astype(F32) * LOG2_E, ((0, 0), (0, 0), (lpad, 2 * GRID_W - 1 - n_dc - lpad)))
    by_col = jnp.stack([padded[:, :, GRID_W - 1 - c: 2 * GRID_W - 1 - c] for c in range(GRID_W)], axis=1)
    by_col = jnp.where(jnp.asarray(col_valid)[None, :, None, :], by_col, MASK_VALUE)
    rpad = NAT_K_ROWS - kh
    by_col = jnp.pad(by_col, ((0, 0), (0, 0), (rpad, 2 * NAT_K_ROWS - 1 - n_dr - rpad), (0, 0)),
                     constant_values=MASK_VALUE)
    k_rows = np.arange(NAT_K_ROWS)
    tables = []
    for t in range(3):
        blocks = []
        for ri in range(NAT_Q_ROWS):
            r = (NAT_K_ROWS - NAT_Q_ROWS) // 2 * t + ri
            row_start = min(max(r - kh // 2, 0), NAT_K_ROWS - kh)
            row_valid = (k_rows >= row_start) & (k_rows < row_start + kh)
            blk = by_col[:, :, NAT_K_ROWS - 1 - r: 2 * NAT_K_ROWS - 1 - r, :]
            blk = jnp.where(jnp.asarray(row_valid)[None, None, :, None], blk, MASK_VALUE)
            blocks.append(blk.reshape(n_heads, GRID_W, NAT_K_ROWS * GRID_W))
        tables.append(jnp.stack(blocks, axis=1).reshape(n_heads, NAT_Q_ROWS * GRID_W, NAT_K_ROWS * GRID_W))
    return jnp.stack(tables)


def _natten_body(q_ref, k0_ref, k1_ref, k2_ref, k3_ref, v0_ref, v1_ref, v2_ref, v3_ref, bias_ref, o_ref,
                 *, log2_scale, head_dim, n_sub):
    k_all = jnp.concatenate([k0_ref[0], k1_ref[0], k2_ref[0], k3_ref[0]], axis=0)
    v_all = jnp.concatenate([v0_ref[0], v1_ref[0], v2_ref[0], v3_ref[0]], axis=0)
    rows = q_ref.shape[1] // n_sub
    for h in range(q_ref.shape[2] // head_dim):
        cols = slice(h * head_dim, (h + 1) * head_dim)
        k = k_all[:, cols]
        v = v_all[:, cols]
        for i in range(n_sub):
            r = slice(i * rows, (i + 1) * rows)
            s = lax.dot_general(q_ref[0, r, cols], k, (((1,), (1,)), ((), ())), preferred_element_type=F32)
            s = s * log2_scale + bias_ref[0, h, r, :]
            m = jnp.max(s, axis=-1, keepdims=True)
            p = jnp.exp2(s - m)
            l = jnp.sum(p, axis=-1, keepdims=True)
            o = jnp.dot(p.astype(BF16), v, preferred_element_type=F32)
            o_ref[0, r, cols] = (o / l).astype(o_ref.dtype)


def _natten(qkv, bias, *, n_heads, head_dim, heads_per_step=2, n_sub=2):
    b, seq, _ = qkv.shape
    tq = NAT_Q_ROWS * GRID_W
    tk = NAT_K_CHUNK_ROWS * GRID_W
    n_kc = NAT_K_ROWS // NAT_K_CHUNK_ROWS
    hp = min(heads_per_step, n_heads)
    assert n_kc == 4 and seq % (NAT_K_ROWS * GRID_W) == 0 and n_heads % hp == 0
    n_hg = n_heads // hp
    n_qb = seq // tq
    n_kb = seq // tk
    lead = (NAT_K_ROWS - NAT_Q_ROWS) // 2 // NAT_K_CHUNK_ROWS

    def kv_spec(which, c):
        def index(bi, h, r):
            start = jnp.clip(r * (tq // tk) - lead, 0, n_kb - n_kc)
            return (bi, start + c, which * n_hg + h)
        return pl.BlockSpec((1, tk, hp * head_dim), index)

    def bias_index(bi, h, r):
        t = jnp.where(r == 0, 0, jnp.where(r == n_qb - 1, 2, 1))
        return (t, h, 0, 0)

    return pl.pallas_call(
        functools.partial(_natten_body, log2_scale=head_dim ** -0.5 * LOG2_E, head_dim=head_dim, n_sub=n_sub),
        grid=(b, n_hg, n_qb),
        in_specs=[pl.BlockSpec((1, tq, hp * head_dim), lambda bi, h, r: (bi, r, h))]
        + [kv_spec(1, c) for c in range(n_kc)]
        + [kv_spec(2, c) for c in range(n_kc)]
        + [pl.BlockSpec((1, hp, tq, NAT_K_ROWS * GRID_W), bias_index)],
        out_specs=pl.BlockSpec((1, tq, hp * head_dim), lambda bi, h, r: (bi, r, h)),
        out_shape=jax.ShapeDtypeStruct((b, seq, n_heads * head_dim), BF16),
        compiler_params=_params(("parallel", "parallel", "arbitrary")),
        name="natten",
    )(qkv, *([qkv] * (2 * n_kc)), bias)


def _pool_body(u_ref, prev_ref, next_ref, w_ref, scale_ref, o_ref, ext_ref, *, seq, halo, group_dim):
    t = pl.program_id(1)
    tt = u_ref.shape[1]
    u = u_ref[0]
    ext_ref[pl.ds(halo, tt), :] = u
    ext_ref[pl.ds(0, halo), :] = jnp.where(t > 0, prev_ref[0], 0.0)
    ext_ref[pl.ds(halo + tt, halo), :] = jnp.where(t < pl.num_programs(1) - 1, next_ref[0], 0.0)
    pos = t * tt + lax.broadcasted_iota(jnp.int32, (tt, 1), 0)
    for g, w in enumerate(POOL_WINDOWS):
        cols = pl.ds(g * group_dim, group_dim)
        total = ext_ref[pl.ds(halo - w // 2, tt), cols]
        for d in range(-w // 2 + 1, w // 2):
            total = total + ext_ref[pl.ds(halo + d, tt), cols]
        count = jnp.minimum(pos + w // 2, seq) - jnp.maximum(pos - w // 2, 0)
        pooled = total / count.astype(F32) - u[:, g * group_dim:(g + 1) * group_dim]
        mixed = jnp.dot(pooled.astype(BF16), w_ref[g], preferred_element_type=F32)
        o_ref[0, :, cols] = (mixed * scale_ref[:, cols]).astype(o_ref.dtype)


def _pool(u, w_pool, pool_scale, *, tt=512):
    b, seq, width = u.shape
    n_groups, group_dim, _ = w_pool.shape
    assert n_groups == len(POOL_WINDOWS)
    halo = 8
    assert max(POOL_WINDOWS) // 2 <= halo
    tt = min(tt, seq)
    assert seq % tt == 0 and tt % halo == 0
    hb = tt // halo
    return pl.pallas_call(
        functools.partial(_pool_body, seq=seq, halo=halo, group_dim=group_dim),
        grid=(b, seq // tt),
        in_specs=[
            pl.BlockSpec((1, tt, width), lambda bi, t: (bi, t, 0)),
            pl.BlockSpec((1, halo, width), lambda bi, t: (bi, jnp.maximum(t * hb - 1, 0), 0)),
            pl.BlockSpec((1, halo, width), lambda bi, t: (bi, jnp.minimum((t + 1) * hb, seq // halo - 1), 0)),
            pl.BlockSpec((n_groups, group_dim, group_dim), lambda bi, t: (0, 0, 0)),
            pl.BlockSpec((1, width), lambda bi, t: (0, 0)),
        ],
        out_specs=pl.BlockSpec((1, tt, width), lambda bi, t: (bi, t, 0)),
        out_shape=jax.ShapeDtypeStruct((b, seq, width), BF16),
        scratch_shapes=[pltpu.VMEM((tt + 2 * halo, width), F32)],
        compiler_params=_params(("parallel", "arbitrary")),
        name="pool",
    )(u, u, u, w_pool, pool_scale)


def _memkv_body(m_ref, g_ref, w_ref, o_ref):
    h = _rms(m_ref[...], g_ref[...]).astype(BF16)
    o_ref[...] = jnp.dot(h, w_ref[...], preferred_element_type=F32).astype(o_ref.dtype)


def _memkv(mem, g_mem, w_kv, *, tr=256, tn=512):
    n_rows, d = mem.shape
    width = w_kv.shape[1]
    tr = min(tr, n_rows)
    tn = min(tn, width)
    assert n_rows % tr == 0 and width % tn == 0
    return pl.pallas_call(
        _memkv_body,
        grid=(n_rows // tr, width // tn),
        in_specs=[
            pl.BlockSpec((tr, d), lambda i, j: (i, 0)),
            pl.BlockSpec((1, d), lambda i, j: (0, 0)),
            pl.BlockSpec((d, tn), lambda i, j: (0, j)),
        ],
        out_specs=pl.BlockSpec((tr, tn), lambda i, j: (i, j)),
        out_shape=jax.ShapeDtypeStruct((n_rows, width), BF16),
        compiler_params=_params(("parallel", "arbitrary")),
        name="memkv",
    )(mem, g_mem, w_kv)


def _memattn_body(q_ref, kv_ref, o_ref, *, n_heads, head_dim):
    scale = head_dim ** -0.5
    width = n_heads * head_dim
    for h in range(n_heads):
        cols = slice(h * head_dim, (h + 1) * head_dim)
        q = q_ref[0, :, cols]
        k = kv_ref[0, :, cols]
        v = kv_ref[0, :, width + h * head_dim: width + (h + 1) * head_dim]
        s = lax.dot_general(q, k, (((1,), (1,)), ((), ())), preferred_element_type=F32) * scale
        m = jnp.max(s, axis=-1, keepdims=True)
        p = jnp.exp(s - m)
        l = jnp.sum(p, axis=-1, keepdims=True)
        o = jnp.dot(p.astype(BF16), v, preferred_element_type=F32)
        o_ref[0, :, cols] = (o / l).astype(o_ref.dtype)


def _memattn(q, kv, *, n_heads, tq=512):
    b, seq, width = q.shape
    n_mem = kv.shape[1]
    tq = min(tq, seq)
    assert seq % tq == 0
    return pl.pallas_call(
        functools.partial(_memattn_body, n_heads=n_heads, head_dim=width // n_heads),
        grid=(b, seq // tq),
        in_specs=[
            pl.BlockSpec((1, tq, width), lambda bi, t: (bi, t, 0)),
            pl.BlockSpec((1, n_mem, 2 * width), lambda bi, t: (bi, 0, 0)),
        ],
        out_specs=pl.BlockSpec((1, tq, width), lambda bi, t: (bi, t, 0)),
        out_shape=jax.ShapeDtypeStruct((b, seq, width), BF16),
        compiler_params=_params(("parallel", "arbitrary")),
        name="memattn",
    )(q, kv)


def _merge_body(h_ref, a_ref, b_ref, c_ref, wga_ref, wgb_ref, wgc_ref, bg_ref, wa_ref, wb_ref, wc_ref, o_ref):
    h = h_ref[...]
    merged = None
    for i, (x_ref, wg_ref, w_ref) in enumerate(
            ((a_ref, wga_ref, wa_ref), (b_ref, wgb_ref, wb_ref), (c_ref, wgc_ref, wc_ref))):
        logits = jnp.dot(h, wg_ref[...], preferred_element_type=F32) + bg_ref[i:i + 1, :]
        gate = 1.0 / (1.0 + jnp.exp(-logits))
        y = jnp.dot(x_ref[...], w_ref[...], preferred_element_type=F32)
        merged = gate * y if merged is None else merged + gate * y
    o_ref[...] = merged.astype(o_ref.dtype)


def _merge(h, att_a, mix_b, att_c, w_in, gate_col0, b_gate, w_a, w_b, w_c, *, tm=512, tn=512):
    n_tok, d = h.shape
    tm = min(tm, n_tok)
    tn = min(tn, d)
    assert n_tok % tm == 0 and d % tn == 0 and gate_col0 % tn == 0
    nj = d // tn
    g0 = gate_col0 // tn

    def act(x):
        return pl.BlockSpec((tm, x.shape[1]), lambda i, j: (i, 0))

    def gate_w(branch):
        return pl.BlockSpec((d, tn), lambda i, j: (0, g0 + branch * nj + j))

    def out_w(w):
        return pl.BlockSpec((w.shape[0], tn), lambda i, j: (0, j))

    return pl.pallas_call(
        _merge_body,
        grid=(n_tok // tm, nj),
        in_specs=[act(h), act(att_a), act(mix_b), act(att_c),
                  gate_w(0), gate_w(1), gate_w(2),
                  pl.BlockSpec((b_gate.shape[0], tn), lambda i, j: (0, j)),
                  out_w(w_a), out_w(w_b), out_w(w_c)],
        out_specs=pl.BlockSpec((tm, tn), lambda i, j: (i, j)),
        out_shape=jax.ShapeDtypeStruct((n_tok, d), BF16),
        compiler_params=_params(("parallel", "arbitrary")),
        name="merge",
    )(h, att_a, mix_b, att_c, w_in, w_in, w_in, b_gate, w_a, w_b, w_c)


def _wo_body(x_ref, m_ref, w_ref, g_ref, o_ref, *, row_chunk):
    j = pl.program_id(1)
    tm = x_ref.shape[0]

    @pl.when(j == 0)
    def _():
        def prologue(rows):
            o_ref[rows, :] = jnp.zeros((row_chunk, o_ref.shape[1]), F32)
        _for_row_chunks(tm, row_chunk, prologue)

    o_ref[...] += jnp.dot(m_ref[...], w_ref[...], preferred_element_type=F32)

    @pl.when(j == pl.num_programs(1) - 1)
    def _():
        def epilogue(rows):
            o_ref[rows, :] = x_ref[rows, :] + _rms(o_ref[rows, :], g_ref[...])
        _for_row_chunks(tm, row_chunk, epilogue)


def _wo(x, merged, w_o, g_post, *, tm=512, tk=1024):
    n_tok, d = x.shape
    tm = min(tm, n_tok)
    tk = min(tk, d)
    assert n_tok % tm == 0 and d % tk == 0
    return pl.pallas_call(
        functools.partial(_wo_body, row_chunk=min(64, tm)),
        grid=(n_tok // tm, d // tk),
        in_specs=[
            pl.BlockSpec((tm, d), lambda i, j: (i, 0)),
            pl.BlockSpec((tm, tk), lambda i, j: (i, j)),
            pl.BlockSpec((tk, d), lambda i, j: (j, 0)),
            pl.BlockSpec((1, d), lambda i, j: (0, 0)),
        ],
        out_specs=pl.BlockSpec((tm, d), lambda i, j: (i, 0)),
        out_shape=jax.ShapeDtypeStruct((n_tok, d), F32),
        compiler_params=_params(("parallel", "arbitrary")),
        name="wo",
    )(x, merged, w_o, g_post)


def _mixer(x1, mem, p, bias_tables, *, b, seq):
    n_tok, d = x1.shape
    na_heads = p["rpb"].shape[0]
    n_groups, group_dim, _ = p["w_pool"].shape
    pool_width = n_groups * group_dim
    mem_width = p["w_mem_kv"].shape[1] // 2
    na_width = p["w_a_out"].shape[0]
    gate_col0 = 3 * na_width + pool_width + mem_width

    h, qkv, u, qc = _inproj(x1, p["g_mix_pre"], p["w_in"],
                            na_width=na_width, pool_width=pool_width, mem_width=mem_width)
    att_a = _natten(qkv.reshape(b, seq, 3 * na_width), bias_tables,
                    n_heads=na_heads, head_dim=na_width // na_heads)
    mix_b = _pool(u.reshape(b, seq, pool_width), p["w_pool"], p["pool_scale"])
    kv = _memkv(mem.reshape(b * mem.shape[1], d), p["g_mem"], p["w_mem_kv"])
    att_c = _memattn(qc.reshape(b, seq, mem_width), kv.reshape(b, mem.shape[1], 2 * mem_width),
                     n_heads=MEM_HEADS)
    merged = _merge(h, att_a.reshape(n_tok, na_width), mix_b.reshape(n_tok, pool_width),
                    att_c.reshape(n_tok, mem_width), p["w_in"], gate_col0, p["b_gate"],
                    p["w_a_out"], p["w_b_out"], p["w_c_out"])
    return _wo(x1, merged, p["w_o"], p["g_mix_post"])


_MATMUL_WEIGHTS = ("w1_gate", "w1_up", "w1_down", "w_in", "w_pool", "w_mem_kv",
                   "w_a_out", "w_b_out", "w_c_out", "w_o", "w2_gate", "w2_up", "w2_down")
_ROW_VECTORS = ("g_ffn1_pre", "g_ffn1_post", "g_mix_pre", "pool_scale", "g_mem", "g_mix_post",
                "g_ffn2_pre", "g_ffn2_post", "g_final")


def kernel(x_prompt, x_sample, mem_prompt, mem_sample, g_ffn1_pre, w1_gate, w1_up, w1_down, g_ffn1_post, g_mix_pre, w_in, rpb, w_pool, pool_scale, g_mem, w_mem_kv, w_a_out, w_b_out, w_c_out, b_gate, w_o, g_mix_post, g_ffn2_pre, w2_gate, w2_up, w2_down, g_ffn2_post, g_final):
    stacked = dict(g_ffn1_pre=g_ffn1_pre, w1_gate=w1_gate, w1_up=w1_up, w1_down=w1_down,
                   g_ffn1_post=g_ffn1_post, g_mix_pre=g_mix_pre, w_in=w_in, rpb=rpb, w_pool=w_pool,
                   pool_scale=pool_scale, g_mem=g_mem, w_mem_kv=w_mem_kv, w_a_out=w_a_out,
                   w_b_out=w_b_out, w_c_out=w_c_out, b_gate=b_gate, w_o=w_o, g_mix_post=g_mix_post,
                   g_ffn2_pre=g_ffn2_pre, w2_gate=w2_gate, w2_up=w2_up, w2_down=w2_down,
                   g_ffn2_post=g_ffn2_post, g_final=g_final)
    y_prompt, y_sample = x_prompt, x_sample
    for layer in range(g_final.shape[0]):
        p = {name: a[layer] for name, a in stacked.items()}
        for name in _MATMUL_WEIGHTS:
            p[name] = p[name].astype(BF16)
        for name in _ROW_VECTORS:
            p[name] = p[name].reshape(1, -1)
        bias_tables = _natten_bias_tables(p["rpb"])
        (bp, sp, d), (bs, ss, _) = y_prompt.shape, y_sample.shape
        x1p, x1s = _ffn_pair(y_prompt.reshape(bp * sp, d), y_sample.reshape(bs * ss, d),
                             p["g_ffn1_pre"], p["w1_gate"], p["w1_up"], p["w1_down"], p["g_ffn1_post"],
                             p["g_final"], final_norm=False)
        x2p = _mixer(x1p, mem_prompt, p, bias_tables, b=bp, seq=sp)
        x2s = _mixer(x1s, mem_sample, p, bias_tables, b=bs, seq=ss)
        yp, ys = _ffn_pair(x2p, x2s, p["g_ffn2_pre"], p["w2_gate"], p["w2_up"], p["w2_down"], p["g_ffn2_post"],
                           p["g_final"], final_norm=True)
        y_prompt, y_sample = yp.reshape(bp, sp, d), ys.reshape(bs, ss, d)
    return (y_prompt, y_sample)
```

```python
import functools

import numpy as np
import jax
import jax.numpy as jnp
from jax import lax
from jax.experimental import pallas as pl
from jax.experimental.pallas import tpu as pltpu

GRID_W = 64
POOL_WINDOWS = (2, 4, 8, 16)
MEM_HEADS = 4
RMS_EPS = 1e-6
MASK_VALUE = -1e30
LOG2_E = 1.4426950408889634

FFN_CHUNK = 256
NAT_Q_ROWS = 8
NAT_K_ROWS = 16
NAT_K_CHUNK_ROWS = 4

V7X_VMEM_BYTES = 64 * 1024 * 1024
VMEM_LIMIT = V7X_VMEM_BYTES - 6 * 1024 * 1024

F32 = jnp.float32
BF16 = jnp.bfloat16


def _rms(x, g):
    ms = jnp.mean(x * x, axis=-1, keepdims=True)
    return (x * lax.rsqrt(ms + RMS_EPS)) * g


def _silu_mul(g, u):
    return (g * (1.0 / (1.0 + jnp.exp(-g)))) * u


def _for_row_chunks(n_rows, chunk, fn):
    def body(c, carry):
        fn(pl.ds(pl.multiple_of(c * chunk, chunk), chunk))
        return carry
    lax.fori_loop(0, n_rows // chunk, body, 0)


def _params(sem):
    return pltpu.CompilerParams(dimension_semantics=sem, vmem_limit_bytes=VMEM_LIMIT)


def _ffn_pair_body(*refs, tm, tf, rc, edge_rows, n_a, n_tiles, n_ff, final_norm, emit_h):
    (xa_hbm, xb_hbm, gpre_ref, wg_hbm, wu_hbm, wd_hbm, gpost_ref, gfin_ref, gnext_ref), refs = refs[:9], refs[9:]
    if emit_h:
        (oa_hbm, ob_hbm, na_hbm, nb_hbm), refs = refs[:4], refs[4:]
    else:
        (oa_hbm, ob_hbm), refs = refs[:2], refs[2:]
        na_hbm = nb_hbm = None
    (hbuf, accbuf, gbuf, ubuf, abuf, wgbuf, wubuf, wdbuf, xpbuf, xebuf, obuf, nbuf, xedge, oedge, nedge,
     wsem, xpsem, xesem, osem, nsem, edge_in_sem, edge_out_sem, edge_next_sem) = refs

    n_chunks = tm // rc
    n_steps = n_tiles * n_ff
    n_edge = tm // edge_rows

    def coords(s):
        t = s // n_ff
        return t, s - t * n_ff

    def up_weight_copies(j_up, slot):
        cols = pl.ds(pl.multiple_of(j_up * tf, tf), tf)
        return [pltpu.make_async_copy(wg_hbm.at[:, cols], wgbuf.at[slot], wsem.at[slot, 0]),
                pltpu.make_async_copy(wu_hbm.at[:, cols], wubuf.at[slot], wsem.at[slot, 1])]

    def weight_copies(j_down, j_up, slot):
        rows = pl.ds(pl.multiple_of(j_down * tf, tf), tf)
        return up_weight_copies(j_up, slot) + [
            pltpu.make_async_copy(wd_hbm.at[rows, :], wdbuf.at[slot], wsem.at[slot, 2])]

    def start_rows_in(row0, n_rows, dst, sem):
        row0 = jnp.asarray(row0, jnp.int32)

        @pl.when(row0 < n_a)
        def _():
            pltpu.make_async_copy(xa_hbm.at[pl.ds(pl.multiple_of(row0, n_rows), n_rows), :], dst, sem).start()

        @pl.when(row0 >= n_a)
        def _():
            pltpu.make_async_copy(xb_hbm.at[pl.ds(pl.multiple_of(row0 - n_a, n_rows), n_rows), :], dst, sem).start()

    def wait_rows_in(n_rows, dst, sem):
        pltpu.make_async_copy(xa_hbm.at[pl.ds(0, n_rows), :], dst, sem).wait()

    def start_rows_out(row0, n_rows, src, sem, dst_a, dst_b):
        row0 = jnp.asarray(row0, jnp.int32)

        @pl.when(row0 < n_a)
        def _():
            pltpu.make_async_copy(src, dst_a.at[pl.ds(pl.multiple_of(row0, n_rows), n_rows), :], sem).start()

        @pl.when(row0 >= n_a)
        def _():
            pltpu.make_async_copy(src, dst_b.at[pl.ds(pl.multiple_of(row0 - n_a, n_rows), n_rows), :], sem).start()

    def wait_rows_out(n_rows, src, sem, dst_a):
        pltpu.make_async_copy(src, dst_a.at[pl.ds(0, n_rows), :], sem).wait()

    def start_results_out(row0, n_rows, out_src, out_sem, next_src, next_sem):
        start_rows_out(row0, n_rows, out_src, out_sem, oa_hbm, ob_hbm)
        if emit_h:
            start_rows_out(row0, n_rows, next_src, next_sem, na_hbm, nb_hbm)

    def wait_results_out(n_rows, out_src, out_sem, next_src, next_sem):
        wait_rows_out(n_rows, out_src, out_sem, oa_hbm)
        if emit_h:
            wait_rows_out(n_rows, next_src, next_sem, na_hbm)

    def stage_inputs(s):
        t, j = coords(s)
        j_up = jnp.where(j == n_ff - 1, 0, j + 1)
        chunk = jnp.minimum(j, n_chunks - 1)
        row_pre = jnp.minimum(t + 1, n_tiles - 1) * tm + chunk * rc
        row_epi = jnp.clip(t - 1, 0, n_tiles - 1) * tm + chunk * rc
        return j, j_up, row_pre, row_epi, chunk

    def start_inputs(s, slot):
        j_down, j_up, row_pre, row_epi, _ = stage_inputs(s)
        for c in weight_copies(j_down, j_up, slot):
            c.start()
        start_rows_in(row_pre, rc, xpbuf.at[slot], xpsem.at[slot])
        start_rows_in(row_epi, rc, xebuf.at[slot], xesem.at[slot])

    def wait_inputs(slot):
        for c in weight_copies(0, 0, slot):
            c.wait()
        wait_rows_in(rc, xpbuf.at[slot], xpsem.at[slot])
        wait_rows_in(rc, xebuf.at[slot], xesem.at[slot])

    def finishes_rows(s):
        t, j = coords(s)
        return jnp.logical_and(jnp.logical_and(s >= 0, t >= 1), j < n_chunks)

    def finish_rows(acc_rows, x_rows, out_ref, next_ref):
        out = x_rows + 0.5 * _rms(acc_rows, gpost_ref[...])
        if final_norm:
            out = _rms(out, gfin_ref[...])
        out_ref[...] = out
        if emit_h:
            next_ref[...] = _rms(out, gnext_ref[...]).astype(BF16)

    def up_chunk(h_sel, slot):
        h = hbuf[h_sel]
        gbuf[...] = jnp.dot(h, wgbuf[slot], preferred_element_type=F32)
        ubuf[...] = jnp.dot(h, wubuf[slot], preferred_element_type=F32)

    start_rows_in(0, edge_rows, xedge.at[0], edge_in_sem.at[0])
    for c in range(n_edge):
        wait_rows_in(edge_rows, xedge.at[c % 2], edge_in_sem.at[c % 2])
        if c + 1 < n_edge:
            start_rows_in((c + 1) * edge_rows, edge_rows, xedge.at[(c + 1) % 2], edge_in_sem.at[(c + 1) % 2])
        hbuf[0, pl.ds(c * edge_rows, edge_rows), :] = _rms(xedge[c % 2], gpre_ref[...]).astype(BF16)
        accbuf[1, pl.ds(c * edge_rows, edge_rows), :] = jnp.zeros((edge_rows, accbuf.shape[2]), F32)
    for c in up_weight_copies(0, 1):
        c.start()
    for c in up_weight_copies(0, 1):
        c.wait()
    up_chunk(0, 1)
    start_inputs(0, 0)

    def step(s, carry):
        t, j = coords(s)
        chunk = stage_inputs(s)[4]
        slot = s % 2
        cur = t % 2
        other = 1 - cur
        wait_inputs(slot)
        start_inputs(s + 1, 1 - slot)

        @pl.when(finishes_rows(s - 2))
        def _():
            wait_results_out(rc, obuf.at[slot], osem.at[slot], nbuf.at[slot], nsem.at[slot])

        @pl.when(finishes_rows(s - 1))
        def _():
            start_results_out(stage_inputs(s - 1)[3], rc, obuf.at[1 - slot], osem.at[1 - slot],
                              nbuf.at[1 - slot], nsem.at[1 - slot])

        @pl.when(j == 0)
        def _():
            def zero(zrows):
                accbuf[cur, zrows, :] = jnp.zeros((edge_rows, accbuf.shape[2]), F32)
            _for_row_chunks(tm, edge_rows, zero)

        abuf[...] = _silu_mul(gbuf[...], ubuf[...]).astype(BF16)
        rows = pl.ds(pl.multiple_of(chunk * rc, rc), rc)
        hbuf[other, rows, :] = _rms(xpbuf[slot], gpre_ref[...]).astype(BF16)
        finish_rows(accbuf[other, rows, :], xebuf[slot], obuf.at[slot], nbuf.at[slot])
        accbuf[cur] += jnp.dot(abuf[...], wdbuf[slot], preferred_element_type=F32)
        up_chunk(jnp.where(j == n_ff - 1, other, cur), slot)
        return carry

    lax.fori_loop(0, n_steps, step, 0)

    end_slot = n_steps % 2
    wait_inputs(end_slot)

    @pl.when(finishes_rows(jnp.int32(n_steps - 2)))
    def _():
        wait_results_out(rc, obuf.at[end_slot], osem.at[end_slot], nbuf.at[end_slot], nsem.at[end_slot])

    @pl.when(finishes_rows(jnp.int32(n_steps - 1)))
    def _():
        last_bufs = (obuf.at[1 - end_slot], osem.at[1 - end_slot], nbuf.at[1 - end_slot], nsem.at[1 - end_slot])
        start_results_out(stage_inputs(jnp.int32(n_steps - 1))[3], rc, *last_bufs)
        wait_results_out(rc, *last_bufs)

    last_row0 = (n_tiles - 1) * tm

    def edge_bufs(c):
        return oedge.at[c % 2], edge_out_sem.at[c % 2], nedge.at[c % 2], edge_next_sem.at[c % 2]

    start_rows_in(last_row0, edge_rows, xedge.at[0], edge_in_sem.at[0])
    for c in range(n_edge):
        wait_rows_in(edge_rows, xedge.at[c % 2], edge_in_sem.at[c % 2])
        if c + 1 < n_edge:
            start_rows_in(last_row0 + (c + 1) * edge_rows, edge_rows, xedge.at[(c + 1) % 2],
                          edge_in_sem.at[(c + 1) % 2])
        if c >= 2:
            wait_results_out(edge_rows, *edge_bufs(c))
        finish_rows(accbuf[(n_tiles - 1) % 2, pl.ds(c * edge_rows, edge_rows), :], xedge[c % 2],
                    oedge.at[c % 2], nedge.at[c % 2])
        start_results_out(last_row0 + c * edge_rows, edge_rows, *edge_bufs(c))
    for c in range(max(n_edge - 2, 0), n_edge):
        wait_results_out(edge_rows, *edge_bufs(c))


def _ffn_pair(xa, xb, g_pre, wg, wu, wd, g_post, g_fin, g_next, *, final_norm, emit_h,
              tm=768, tf=FFN_CHUNK, rc=32, edge_rows=32):
    (n_a, d), n_b = xa.shape, xb.shape[0]
    d_ff = wg.shape[1]
    n_tok = n_a + n_b
    tm = min(tm, n_tok // 2)
    tf = min(tf, d_ff)
    edge_rows = min(edge_rows, tm)
    n_tiles, n_ff = n_tok // tm, d_ff // tf
    assert n_tok % tm == 0 and d_ff % tf == 0 and tm % rc == 0 and tm % edge_rows == 0
    assert n_a % edge_rows == 0 and n_a % rc == 0 and tm // rc <= n_ff and tm // edge_rows >= 2 and n_tiles >= 2
    hbm = pl.BlockSpec(memory_space=pl.ANY)
    vec = pl.BlockSpec(memory_space=pltpu.VMEM)
    out_shape = [jax.ShapeDtypeStruct((n_a, d), F32), jax.ShapeDtypeStruct((n_b, d), F32)]
    if emit_h:
        out_shape += [jax.ShapeDtypeStruct((n_a, d), BF16), jax.ShapeDtypeStruct((n_b, d), BF16)]
    next_rows = (rc, edge_rows) if emit_h else (8, 8)
    return pl.pallas_call(
        functools.partial(_ffn_pair_body, tm=tm, tf=tf, rc=rc, edge_rows=edge_rows, n_a=n_a, n_tiles=n_tiles,
                          n_ff=n_ff, final_norm=final_norm, emit_h=emit_h),
        in_specs=[hbm, hbm, vec, hbm, hbm, hbm, vec, vec, vec],
        out_specs=[hbm] * len(out_shape),
        out_shape=out_shape,
        scratch_shapes=[
            pltpu.VMEM((2, tm, d), BF16),
            pltpu.VMEM((2, tm, d), F32),
            pltpu.VMEM((tm, tf), F32),
            pltpu.VMEM((tm, tf), F32),
            pltpu.VMEM((tm, tf), BF16),
            pltpu.VMEM((2, d, tf), BF16),
            pltpu.VMEM((2, d, tf), BF16),
            pltpu.VMEM((2, tf, d), BF16),
            pltpu.VMEM((2, rc, d), F32),
            pltpu.VMEM((2, rc, d), F32),
            pltpu.VMEM((2, rc, d), F32),
            pltpu.VMEM((2, next_rows[0], d), BF16),
            pltpu.VMEM((2, edge_rows, d), F32),
            pltpu.VMEM((2, edge_rows, d), F32),
            pltpu.VMEM((2, next_rows[1], d), BF16),
            pltpu.SemaphoreType.DMA((2, 3)),
            pltpu.SemaphoreType.DMA((2,)),
            pltpu.SemaphoreType.DMA((2,)),
            pltpu.SemaphoreType.DMA((2,)),
            pltpu.SemaphoreType.DMA((2,)),
            pltpu.SemaphoreType.DMA((2,)),
            pltpu.SemaphoreType.DMA((2,)),
            pltpu.SemaphoreType.DMA((2,)),
        ],
        compiler_params=pltpu.CompilerParams(vmem_limit_bytes=VMEM_LIMIT),
        name="ffn_pair",
    )(xa, xb, g_pre, wg, wu, wd, g_post, g_fin, g_next)


def _inproj_body(h_ref, w_ref, qkv_ref, u_ref, qc_ref, *, n_qkv, n_u):
    j = pl.program_id(1)
    z = jnp.dot(h_ref[...], w_ref[...], preferred_element_type=F32)

    @pl.when(j < n_qkv)
    def _():
        qkv_ref[...] = z.astype(BF16)

    @pl.when(jnp.logical_and(j >= n_qkv, j < n_qkv + n_u))
    def _():
        u_ref[...] = z

    @pl.when(j >= n_qkv + n_u)
    def _():
        qc_ref[...] = z.astype(BF16)


def _inproj(h, w_in, *, na_width, pool_width, mem_width, tm=1024, tn=1024):
    n_tok, d = h.shape
    tm = min(tm, n_tok)
    tn = min(tn, pool_width, mem_width)
    assert n_tok % tm == 0 and na_width % tn == 0 and pool_width % tn == 0 and mem_width % tn == 0
    n_qkv = 3 * na_width // tn
    n_u = pool_width // tn
    n_qc = mem_width // tn
    return pl.pallas_call(
        functools.partial(_inproj_body, n_qkv=n_qkv, n_u=n_u),
        grid=(n_tok // tm, n_qkv + n_u + n_qc),
        in_specs=[
            pl.BlockSpec((tm, d), lambda i, j: (i, 0)),
            pl.BlockSpec((d, tn), lambda i, j: (0, j)),
        ],
        out_specs=[
            pl.BlockSpec((tm, tn), lambda i, j: (i, jnp.minimum(j, n_qkv - 1))),
            pl.BlockSpec((tm, tn), lambda i, j: (i, jnp.clip(j - n_qkv, 0, n_u - 1))),
            pl.BlockSpec((tm, tn), lambda i, j: (i, jnp.clip(j - n_qkv - n_u, 0, n_qc - 1))),
        ],
        out_shape=[
            jax.ShapeDtypeStruct((n_tok, 3 * na_width), BF16),
            jax.ShapeDtypeStruct((n_tok, pool_width), F32),
            jax.ShapeDtypeStruct((n_tok, mem_width), BF16),
        ],
        compiler_params=_params(("parallel", "arbitrary")),
        name="inproj",
    )(h, w_in)


def _natten_bias_tables(rpb):
    n_heads, n_dr, n_dc = rpb.shape
    kh, kw = (n_dr + 1) // 2, (n_dc + 1) // 2
    cols = np.arange(GRID_W)
    col_start = np.clip(cols - kw // 2, 0, GRID_W - kw)
    col_valid = (cols[None, :] >= col_start[:, None]) & (cols[None, :] < col_start[:, None] + kw)
    period = 2 * GRID_W
    lpad = GRID_W - kw
    padded = jnp.pad(rpb.astype(F32) * LOG2_E, ((0, 0), (0, 0), (lpad, period - n_dc - lpad)))
    skew = jnp.broadcast_to(padded[:, :, None, :], (n_heads, n_dr, GRID_W, period))
    skew = skew.reshape(n_heads, n_dr, GRID_W * period)[:, :, :GRID_W * (period - 1)]
    by_col = skew.reshape(n_heads, n_dr, GRID_W, period - 1)[:, :, :, GRID_W - 1:]
    by_col = jnp.where(jnp.asarray(col_valid)[None, None], by_col, MASK_VALUE).transpose(0, 2, 1, 3)
    rpad = NAT_K_ROWS - kh
    by_col = jnp.pad(by_col, ((0, 0), (0, 0), (rpad, 2 * NAT_K_ROWS - 1 - n_dr - rpad), (0, 0)),
                     constant_values=MASK_VALUE)
    k_rows = np.arange(NAT_K_ROWS)
    tables = []
    for t in range(3):
        blocks = []
        for ri in range(NAT_Q_ROWS):
            r = (NAT_K_ROWS - NAT_Q_ROWS) // 2 * t + ri
            row_start = min(max(r - kh // 2, 0), NAT_K_ROWS - kh)
            row_valid = (k_rows >= row_start) & (k_rows < row_start + kh)
            blk = by_col[:, :, NAT_K_ROWS - 1 - r: 2 * NAT_K_ROWS - 1 - r, :]
            blk = jnp.where(jnp.asarray(row_valid)[None, None, :, None], blk, MASK_VALUE)
            blocks.append(blk.reshape(n_heads, GRID_W, NAT_K_ROWS * GRID_W))
        tables.append(jnp.stack(blocks, axis=1).reshape(n_heads, NAT_Q_ROWS * GRID_W, NAT_K_ROWS * GRID_W))
    return jnp.stack(tables)


def _natten_body(q_ref, k0_ref, k1_ref, k2_ref, k3_ref, v0_ref, v1_ref, v2_ref, v3_ref, bias_ref, o_ref,
                 *, log2_scale, head_dim, n_sub):
    k_all = jnp.concatenate([k0_ref[0], k1_ref[0], k2_ref[0], k3_ref[0]], axis=0)
    v_all = jnp.concatenate([v0_ref[0], v1_ref[0], v2_ref[0], v3_ref[0]], axis=0)
    rows = q_ref.shape[1] // n_sub
    for h in range(q_ref.shape[2] // head_dim):
        cols = slice(h * head_dim, (h + 1) * head_dim)
        k = k_all[:, cols]
        v = v_all[:, cols]
        for i in range(n_sub):
            r = slice(i * rows, (i + 1) * rows)
            s = lax.dot_general(q_ref[0, r, cols], k, (((1,), (1,)), ((), ())), preferred_element_type=F32)
            s = s * log2_scale + bias_ref[0, h, r, :]
            m = jnp.max(s, axis=-1, keepdims=True)
            p = jnp.exp2(s - m)
            l = jnp.sum(p, axis=-1, keepdims=True)
            o = jnp.dot(p.astype(BF16), v, preferred_element_type=F32)
            o_ref[0, r, cols] = (o / l).astype(o_ref.dtype)


def _natten(qkv, bias, *, n_heads, head_dim, heads_per_step=4, n_sub=2):
    b, seq, _ = qkv.shape
    tq = NAT_Q_ROWS * GRID_W
    tk = NAT_K_CHUNK_ROWS * GRID_W
    n_kc = NAT_K_ROWS // NAT_K_CHUNK_ROWS
    hp = min(heads_per_step, n_heads)
    assert n_kc == 4 and seq % (NAT_K_ROWS * GRID_W) == 0 and n_heads % hp == 0
    n_hg = n_heads // hp
    n_qb = seq // tq
    n_kb = seq // tk
    lead = (NAT_K_ROWS - NAT_Q_ROWS) // 2 // NAT_K_CHUNK_ROWS

    def kv_spec(which, c):
        def index(bi, h, r):
            start = jnp.clip(r * (tq // tk) - lead, 0, n_kb - n_kc)
            return (bi, start + c, which * n_hg + h)
        return pl.BlockSpec((1, tk, hp * head_dim), index)

    def bias_index(bi, h, r):
        t = jnp.where(r == 0, 0, jnp.where(r == n_qb - 1, 2, 1))
        return (t, h, 0, 0)

    return pl.pallas_call(
        functools.partial(_natten_body, log2_scale=head_dim ** -0.5 * LOG2_E, head_dim=head_dim, n_sub=n_sub),
        grid=(b, n_hg, n_qb),
        in_specs=[pl.BlockSpec((1, tq, hp * head_dim), lambda bi, h, r: (bi, r, h))]
        + [kv_spec(1, c) for c in range(n_kc)]
        + [kv_spec(2, c) for c in range(n_kc)]
        + [pl.BlockSpec((1, hp, tq, NAT_K_ROWS * GRID_W), bias_index)],
        out_specs=pl.BlockSpec((1, tq, hp * head_dim), lambda bi, h, r: (bi, r, h)),
        out_shape=jax.ShapeDtypeStruct((b, seq, n_heads * head_dim), BF16),
        compiler_params=_params(("parallel", "parallel", "arbitrary")),
        name="natten",
    )(qkv, *([qkv] * (2 * n_kc)), bias)


def _pool_body(u_ref, prev_ref, next_ref, w_ref, scale_ref, o_ref, ext_ref, *, seq, halo, group_dim):
    t = pl.program_id(1)
    tt = u_ref.shape[1]
    u = u_ref[0]
    ext_ref[pl.ds(halo, tt), :] = u
    ext_ref[pl.ds(0, halo), :] = jnp.where(t > 0, prev_ref[0], 0.0)
    ext_ref[pl.ds(halo + tt, halo), :] = jnp.where(t < pl.num_programs(1) - 1, next_ref[0], 0.0)
    pos = t * tt + lax.broadcasted_iota(jnp.int32, (tt, 1), 0)
    for g, w in enumerate(POOL_WINDOWS):
        cols = pl.ds(g * group_dim, group_dim)
        total = ext_ref[pl.ds(halo - w // 2, tt), cols]
        for d in range(-w // 2 + 1, w // 2):
            total = total + ext_ref[pl.ds(halo + d, tt), cols]
        count = jnp.minimum(pos + w // 2, seq) - jnp.maximum(pos - w // 2, 0)
        pooled = total / count.astype(F32) - u[:, g * group_dim:(g + 1) * group_dim]
        mixed = jnp.dot(pooled.astype(BF16), w_ref[g], preferred_element_type=F32)
        o_ref[0, :, cols] = (mixed * scale_ref[:, cols]).astype(o_ref.dtype)


def _pool(u, w_pool, pool_scale, *, tt=512):
    b, seq, width = u.shape
    n_groups, group_dim, _ = w_pool.shape
    assert n_groups == len(POOL_WINDOWS)
    halo = 8
    assert max(POOL_WINDOWS) // 2 <= halo
    tt = min(tt, seq)
    assert seq % tt == 0 and tt % halo == 0
    hb = tt // halo
    return pl.pallas_call(
        functools.partial(_pool_body, seq=seq, halo=halo, group_dim=group_dim),
        grid=(b, seq // tt),
        in_specs=[
            pl.BlockSpec((1, tt, width), lambda bi, t: (bi, t, 0)),
            pl.BlockSpec((1, halo, width), lambda bi, t: (bi, jnp.maximum(t * hb - 1, 0), 0)),
            pl.BlockSpec((1, halo, width), lambda bi, t: (bi, jnp.minimum((t + 1) * hb, seq // halo - 1), 0)),
            pl.BlockSpec((n_groups, group_dim, group_dim), lambda bi, t: (0, 0, 0)),
            pl.BlockSpec((1, width), lambda bi, t: (0, 0)),
        ],
        out_specs=pl.BlockSpec((1, tt, width), lambda bi, t: (bi, t, 0)),
        out_shape=jax.ShapeDtypeStruct((b, seq, width), BF16),
        scratch_shapes=[pltpu.VMEM((tt + 2 * halo, width), F32)],
        compiler_params=_params(("parallel", "arbitrary")),
        name="pool",
    )(u, u, u, w_pool, pool_scale)


def _memkv_body(m_ref, g_ref, w_ref, o_ref):
    h = _rms(m_ref[...], g_ref[...]).astype(BF16)
    o_ref[...] = jnp.dot(h, w_ref[...], preferred_element_type=F32).astype(o_ref.dtype)


def _memkv(mem, g_mem, w_kv, *, tr=256, tn=512):
    n_rows, d = mem.shape
    width = w_kv.shape[1]
    tr = min(tr, n_rows)
    tn = min(tn, width)
    assert n_rows % tr == 0 and width % tn == 0
    return pl.pallas_call(
        _memkv_body,
        grid=(n_rows // tr, width // tn),
        in_specs=[
            pl.BlockSpec((tr, d), lambda i, j: (i, 0)),
            pl.BlockSpec((1, d), lambda i, j: (0, 0)),
            pl.BlockSpec((d, tn), lambda i, j: (0, j)),
        ],
        out_specs=pl.BlockSpec((tr, tn), lambda i, j: (i, j)),
        out_shape=jax.ShapeDtypeStruct((n_rows, width), BF16),
        compiler_params=_params(("parallel", "arbitrary")),
        name="memkv",
    )(mem, g_mem, w_kv)


def _memattn_body(q_ref, kv_ref, o_ref, *, n_heads, head_dim):
    scale = head_dim ** -0.5
    width = n_heads * head_dim
    for h in range(n_heads):
        cols = slice(h * head_dim, (h + 1) * head_dim)
        q = q_ref[0, :, cols]
        k = kv_ref[0, :, cols]
        v = kv_ref[0, :, width + h * head_dim: width + (h + 1) * head_dim]
        s = lax.dot_general(q, k, (((1,), (1,)), ((), ())), preferred_element_type=F32) * scale
        m = jnp.max(s, axis=-1, keepdims=True)
        p = jnp.exp(s - m)
        l = jnp.sum(p, axis=-1, keepdims=True)
        o = jnp.dot(p.astype(BF16), v, preferred_element_type=F32)
        o_ref[0, :, cols] = (o / l).astype(o_ref.dtype)


def _memattn(q, kv, *, n_heads, tq=512):
    b, seq, width = q.shape
    n_mem = kv.shape[1]
    tq = min(tq, seq)
    assert seq % tq == 0
    return pl.pallas_call(
        functools.partial(_memattn_body, n_heads=n_heads, head_dim=width // n_heads),
        grid=(b, seq // tq),
        in_specs=[
            pl.BlockSpec((1, tq, width), lambda bi, t: (bi, t, 0)),
            pl.BlockSpec((1, n_mem, 2 * width), lambda bi, t: (bi, 0, 0)),
        ],
        out_specs=pl.BlockSpec((1, tq, width), lambda bi, t: (bi, t, 0)),
        out_shape=jax.ShapeDtypeStruct((b, seq, width), BF16),
        compiler_params=_params(("parallel", "arbitrary")),
        name="memattn",
    )(q, kv)


def _merge_body(h_ref, a_ref, b_ref, c_ref, wga_ref, wgb_ref, wgc_ref, bg_ref, wa_ref, wb_ref, wc_ref, o_ref):
    h = h_ref[...]
    merged = None
    for i, (x_ref, wg_ref, w_ref) in enumerate(
            ((a_ref, wga_ref, wa_ref), (b_ref, wgb_ref, wb_ref), (c_ref, wgc_ref, wc_ref))):
        logits = jnp.dot(h, wg_ref[...], preferred_element_type=F32) + bg_ref[i:i + 1, :]
        gate = 1.0 / (1.0 + jnp.exp(-logits))
        y = jnp.dot(x_ref[...], w_ref[...], preferred_element_type=F32)
        merged = gate * y if merged is None else merged + gate * y
    o_ref[...] = merged.astype(o_ref.dtype)


def _merge(h, att_a, mix_b, att_c, w_in, gate_col0, b_gate, w_a, w_b, w_c, *, tm=512, tn=512):
    n_tok, d = h.shape
    tm = min(tm, n_tok)
    tn = min(tn, d)
    assert n_tok % tm == 0 and d % tn == 0 and gate_col0 % tn == 0
    nj = d // tn
    g0 = gate_col0 // tn

    def act(x):
        return pl.BlockSpec((tm, x.shape[1]), lambda i, j: (i, 0))

    def gate_w(branch):
        return pl.BlockSpec((d, tn), lambda i, j: (0, g0 + branch * nj + j))

    def out_w(w):
        return pl.BlockSpec((w.shape[0], tn), lambda i, j: (0, j))

    return pl.pallas_call(
        _merge_body,
        grid=(n_tok // tm, nj),
        in_specs=[act(h), act(att_a), act(mix_b), act(att_c),
                  gate_w(0), gate_w(1), gate_w(2),
                  pl.BlockSpec((b_gate.shape[0], tn), lambda i, j: (0, j)),
                  out_w(w_a), out_w(w_b), out_w(w_c)],
        out_specs=pl.BlockSpec((tm, tn), lambda i, j: (i, j)),
        out_shape=jax.ShapeDtypeStruct((n_tok, d), BF16),
        compiler_params=_params(("parallel", "arbitrary")),
        name="merge",
    )(h, att_a, mix_b, att_c, w_in, w_in, w_in, b_gate, w_a, w_b, w_c)


def _wo_body(x_ref, m_ref, w_ref, g_ref, o_ref, *, row_chunk):
    j = pl.program_id(1)
    tm = x_ref.shape[0]

    @pl.when(j == 0)
    def _():
        def prologue(rows):
            o_ref[rows, :] = jnp.zeros((row_chunk, o_ref.shape[1]), F32)
        _for_row_chunks(tm, row_chunk, prologue)

    o_ref[...] += jnp.dot(m_ref[...], w_ref[...], preferred_element_type=F32)

    @pl.when(j == pl.num_programs(1) - 1)
    def _():
        def epilogue(rows):
            o_ref[rows, :] = x_ref[rows, :] + _rms(o_ref[rows, :], g_ref[...])
        _for_row_chunks(tm, row_chunk, epilogue)


def _wo(x, merged, w_o, g_post, *, tm=512, tk=1024):
    n_tok, d = x.shape
    tm = min(tm, n_tok)
    tk = min(tk, d)
    assert n_tok % tm == 0 and d % tk == 0
    return pl.pallas_call(
        functools.partial(_wo_body, row_chunk=min(64, tm)),
        grid=(n_tok // tm, d // tk),
        in_specs=[
            pl.BlockSpec((tm, d), lambda i, j: (i, 0)),
            pl.BlockSpec((tm, tk), lambda i, j: (i, j)),
            pl.BlockSpec((tk, d), lambda i, j: (j, 0)),
            pl.BlockSpec((1, d), lambda i, j: (0, 0)),
        ],
        out_specs=pl.BlockSpec((tm, d), lambda i, j: (i, 0)),
        out_shape=jax.ShapeDtypeStruct((n_tok, d), F32),
        compiler_params=_params(("parallel", "arbitrary")),
        name="wo",
    )(x, merged, w_o, g_post)


def _mixer(x1, h, mem, p, bias_tables, *, b, seq):
    n_tok, d = x1.shape
    na_heads = p["rpb"].shape[0]
    n_groups, group_dim, _ = p["w_pool"].shape
    pool_width = n_groups * group_dim
    mem_width = p["w_mem_kv"].shape[1] // 2
    na_width = p["w_a_out"].shape[0]
    gate_col0 = 3 * na_width + pool_width + mem_width

    qkv, u, qc = _inproj(h, p["w_in"], na_width=na_width, pool_width=pool_width, mem_width=mem_width)
    att_a = _natten(qkv.reshape(b, seq, 3 * na_width), bias_tables,
                    n_heads=na_heads, head_dim=na_width // na_heads)
    mix_b = _pool(u.reshape(b, seq, pool_width), p["w_pool"], p["pool_scale"])
    kv = _memkv(mem.reshape(b * mem.shape[1], d), p["g_mem"], p["w_mem_kv"])
    att_c = _memattn(qc.reshape(b, seq, mem_width), kv.reshape(b, mem.shape[1], 2 * mem_width),
                     n_heads=MEM_HEADS)
    merged = _merge(h, att_a.reshape(n_tok, na_width), mix_b.reshape(n_tok, pool_width),
                    att_c.reshape(n_tok, mem_width), p["w_in"], gate_col0, p["b_gate"],
                    p["w_a_out"], p["w_b_out"], p["w_c_out"])
    return _wo(x1, merged, p["w_o"], p["g_mix_post"])


_MATMUL_WEIGHTS = ("w1_gate", "w1_up", "w1_down", "w_in", "w_pool", "w_mem_kv",
                   "w_a_out", "w_b_out", "w_c_out", "w_o", "w2_gate", "w2_up", "w2_down")
_ROW_VECTORS = ("g_ffn1_pre", "g_ffn1_post", "g_mix_pre", "pool_scale", "g_mem", "g_mix_post",
                "g_ffn2_pre", "g_ffn2_post", "g_final")


def kernel(x_prompt, x_sample, mem_prompt, mem_sample, g_ffn1_pre, w1_gate, w1_up, w1_down, g_ffn1_post, g_mix_pre, w_in, rpb, w_pool, pool_scale, g_mem, w_mem_kv, w_a_out, w_b_out, w_c_out, b_gate, w_o, g_mix_post, g_ffn2_pre, w2_gate, w2_up, w2_down, g_ffn2_post, g_final):
    stacked = dict(g_ffn1_pre=g_ffn1_pre, w1_gate=w1_gate, w1_up=w1_up, w1_down=w1_down,
                   g_ffn1_post=g_ffn1_post, g_mix_pre=g_mix_pre, w_in=w_in, rpb=rpb, w_pool=w_pool,
                   pool_scale=pool_scale, g_mem=g_mem, w_mem_kv=w_mem_kv, w_a_out=w_a_out,
                   w_b_out=w_b_out, w_c_out=w_c_out, b_gate=b_gate, w_o=w_o, g_mix_post=g_mix_post,
                   g_ffn2_pre=g_ffn2_pre, w2_gate=w2_gate, w2_up=w2_up, w2_down=w2_down,
                   g_ffn2_post=g_ffn2_post, g_final=g_final)
    y_prompt, y_sample = x_prompt, x_sample
    for layer in range(g_final.shape[0]):
        p = {name: a[layer] for name, a in stacked.items()}
        for name in _MATMUL_WEIGHTS:
            p[name] = p[name].astype(BF16)
        for name in _ROW_VECTORS:
            p[name] = p[name].reshape(1, -1)
        bias_tables = _natten_bias_tables(p["rpb"])
        (bp, sp, d), (bs, ss, _) = y_prompt.shape, y_sample.shape
        x1p, x1s, hp, hs = _ffn_pair(
            y_prompt.reshape(bp * sp, d), y_sample.reshape(bs * ss, d),
            p["g_ffn1_pre"], p["w1_gate"], p["w1_up"], p["w1_down"], p["g_ffn1_post"], p["g_final"],
            p["g_mix_pre"], final_norm=False, emit_h=True)
        x2p = _mixer(x1p, hp, mem_prompt, p, bias_tables, b=bp, seq=sp)
        x2s = _mixer(x1s, hs, mem_sample, p, bias_tables, b=bs, seq=ss)
        yp, ys = _ffn_pair(
            x2p, x2s, p["g_ffn2_pre"], p["w2_gate"], p["w2_up"], p["w2_down"], p["g_ffn2_post"], p["g_final"],
            p["g_final"], final_norm=True, emit_h=False)
        y_prompt, y_sample = yp.reshape(bp, sp, d), ys.reshape(bs, ss, d)
    return (y_prompt, y_sample)
```

```python
import functools

import numpy as np
import jax
import jax.numpy as jnp
from jax import lax
from jax.experimental import pallas as pl
from jax.experimental.pallas import tpu as pltpu

GRID_W = 64
POOL_WINDOWS = (2, 4, 8, 16)
MEM_HEADS = 4
RMS_EPS = 1e-6
MASK_VALUE = -1e30
LOG2_E = 1.4426950408889634

FFN_CHUNK = 256
NAT_Q_ROWS = 8
NAT_K_ROWS = 16
NAT_K_CHUNK_ROWS = 4

V7X_VMEM_BYTES = 64 * 1024 * 1024
VMEM_LIMIT = V7X_VMEM_BYTES - 6 * 1024 * 1024

F32 = jnp.float32
BF16 = jnp.bfloat16


def _rms(x, g):
    ms = jnp.mean(x * x, axis=-1, keepdims=True)
    return (x * lax.rsqrt(ms + RMS_EPS)) * g


def _silu_mul(g, u):
    return (g * (1.0 / (1.0 + jnp.exp(-g)))) * u


def _for_row_chunks(n_rows, chunk, fn):
    def body(c, carry):
        fn(pl.ds(pl.multiple_of(c * chunk, chunk), chunk))
        return carry
    lax.fori_loop(0, n_rows // chunk, body, 0)


def _params(sem):
    return pltpu.CompilerParams(dimension_semantics=sem, vmem_limit_bytes=VMEM_LIMIT)


def _ffn_pair_body(*refs, tm, tf, rc, edge_rows, n_a, n_tiles, n_ff, final_norm, emit_h):
    (xa_hbm, xb_hbm, gpre_ref, wg_hbm, wu_hbm, wd_hbm, gpost_ref, gfin_ref, gnext_ref), refs = refs[:9], refs[9:]
    if emit_h:
        (oa_hbm, ob_hbm, na_hbm, nb_hbm), refs = refs[:4], refs[4:]
    else:
        (oa_hbm, ob_hbm), refs = refs[:2], refs[2:]
        na_hbm = nb_hbm = None
    (hbuf, accbuf, gbuf, ubuf, abuf, wgbuf, wubuf, wdbuf, xpbuf, xebuf, obuf, nbuf, xedge, oedge, nedge,
     wsem, xpsem, xesem, osem, nsem, edge_in_sem, edge_out_sem, edge_next_sem) = refs

    n_chunks = tm // rc
    n_steps = n_tiles * n_ff
    n_edge = tm // edge_rows

    def step_after(tj):
        t, j = tj
        wrap = j == n_ff - 1
        return jnp.where(wrap, t + 1, t), jnp.where(wrap, 0, j + 1)

    def step_before(tj):
        t, j = tj
        wrap = j == 0
        return jnp.where(wrap, t - 1, t), jnp.where(wrap, n_ff - 1, j - 1)

    def coords(s):
        return jnp.int32(s // n_ff), jnp.int32(s % n_ff)

    def up_weight_copies(j_up, slot):
        cols = pl.ds(pl.multiple_of(j_up * tf, tf), tf)
        return [pltpu.make_async_copy(wg_hbm.at[:, cols], wgbuf.at[slot], wsem.at[slot, 0]),
                pltpu.make_async_copy(wu_hbm.at[:, cols], wubuf.at[slot], wsem.at[slot, 1])]

    def weight_copies(j_down, j_up, slot):
        rows = pl.ds(pl.multiple_of(j_down * tf, tf), tf)
        return up_weight_copies(j_up, slot) + [
            pltpu.make_async_copy(wd_hbm.at[rows, :], wdbuf.at[slot], wsem.at[slot, 2])]

    def start_rows_in(row0, n_rows, dst, sem):
        row0 = jnp.asarray(row0, jnp.int32)

        @pl.when(row0 < n_a)
        def _():
            pltpu.make_async_copy(xa_hbm.at[pl.ds(pl.multiple_of(row0, n_rows), n_rows), :], dst, sem).start()

        @pl.when(row0 >= n_a)
        def _():
            pltpu.make_async_copy(xb_hbm.at[pl.ds(pl.multiple_of(row0 - n_a, n_rows), n_rows), :], dst, sem).start()

    def wait_rows_in(n_rows, dst, sem):
        pltpu.make_async_copy(xa_hbm.at[pl.ds(0, n_rows), :], dst, sem).wait()

    def start_rows_out(row0, n_rows, src, sem, dst_a, dst_b):
        row0 = jnp.asarray(row0, jnp.int32)

        @pl.when(row0 < n_a)
        def _():
            pltpu.make_async_copy(src, dst_a.at[pl.ds(pl.multiple_of(row0, n_rows), n_rows), :], sem).start()

        @pl.when(row0 >= n_a)
        def _():
            pltpu.make_async_copy(src, dst_b.at[pl.ds(pl.multiple_of(row0 - n_a, n_rows), n_rows), :], sem).start()

    def wait_rows_out(n_rows, src, sem, dst_a):
        pltpu.make_async_copy(src, dst_a.at[pl.ds(0, n_rows), :], sem).wait()

    def start_results_out(row0, n_rows, out_src, out_sem, next_src, next_sem):
        start_rows_out(row0, n_rows, out_src, out_sem, oa_hbm, ob_hbm)
        if emit_h:
            start_rows_out(row0, n_rows, next_src, next_sem, na_hbm, nb_hbm)

    def wait_results_out(n_rows, out_src, out_sem, next_src, next_sem):
        wait_rows_out(n_rows, out_src, out_sem, oa_hbm)
        if emit_h:
            wait_rows_out(n_rows, next_src, next_sem, na_hbm)

    def stage_inputs(tj):
        t, j = tj
        j_up = jnp.where(j == n_ff - 1, 0, j + 1)
        chunk = jnp.minimum(j, n_chunks - 1)
        row_pre = jnp.minimum(t + 1, n_tiles - 1) * tm + chunk * rc
        row_epi = jnp.clip(t - 1, 0, n_tiles - 1) * tm + chunk * rc
        return j, j_up, row_pre, row_epi, chunk

    def start_inputs(tj, slot):
        j_down, j_up, row_pre, row_epi, _ = stage_inputs(tj)
        for c in weight_copies(j_down, j_up, slot):
            c.start()
        start_rows_in(row_pre, rc, xpbuf.at[slot], xpsem.at[slot])
        start_rows_in(row_epi, rc, xebuf.at[slot], xesem.at[slot])

    def wait_inputs(slot):
        for c in weight_copies(0, 0, slot):
            c.wait()
        wait_rows_in(rc, xpbuf.at[slot], xpsem.at[slot])
        wait_rows_in(rc, xebuf.at[slot], xesem.at[slot])

    def finishes_rows(tj):
        t, j = tj
        return jnp.logical_and(t >= 1, j < n_chunks)

    def finish_rows(acc_rows, x_rows, out_ref, next_ref):
        out = x_rows + 0.5 * _rms(acc_rows, gpost_ref[...])
        if final_norm:
            out = _rms(out, gfin_ref[...])
        out_ref[...] = out
        if emit_h:
            next_ref[...] = _rms(out, gnext_ref[...]).astype(BF16)

    def up_chunk(h_sel, slot):
        h = hbuf[h_sel]
        gbuf[...] = jnp.dot(h, wgbuf[slot], preferred_element_type=F32)
        ubuf[...] = jnp.dot(h, wubuf[slot], preferred_element_type=F32)

    start_rows_in(0, edge_rows, xedge.at[0], edge_in_sem.at[0])
    for c in range(n_edge):
        wait_rows_in(edge_rows, xedge.at[c % 2], edge_in_sem.at[c % 2])
        if c + 1 < n_edge:
            start_rows_in((c + 1) * edge_rows, edge_rows, xedge.at[(c + 1) % 2], edge_in_sem.at[(c + 1) % 2])
        hbuf[0, pl.ds(c * edge_rows, edge_rows), :] = _rms(xedge[c % 2], gpre_ref[...]).astype(BF16)
        accbuf[1, pl.ds(c * edge_rows, edge_rows), :] = jnp.zeros((edge_rows, accbuf.shape[2]), F32)
    for c in up_weight_copies(0, 1):
        c.start()
    for c in up_weight_copies(0, 1):
        c.wait()
    up_chunk(0, 1)
    start_inputs(coords(0), 0)

    def step(_, carry):
        t, j, slot = carry
        tj = (t, j)
        before = step_before(tj)
        cur = t % 2
        other = 1 - cur
        start_inputs(step_after(tj), 1 - slot)

        @pl.when(finishes_rows(step_before(before)))
        def _():
            wait_results_out(rc, obuf.at[slot], osem.at[slot], nbuf.at[slot], nsem.at[slot])

        @pl.when(finishes_rows(before))
        def _():
            start_results_out(stage_inputs(before)[3], rc, obuf.at[1 - slot], osem.at[1 - slot],
                              nbuf.at[1 - slot], nsem.at[1 - slot])

        @pl.when(j == 0)
        def _():
            def zero(zrows):
                accbuf[cur, zrows, :] = jnp.zeros((edge_rows, accbuf.shape[2]), F32)
            _for_row_chunks(tm, edge_rows, zero)

        wait_inputs(slot)
        abuf[...] = _silu_mul(gbuf[...], ubuf[...]).astype(BF16)
        rows = pl.ds(pl.multiple_of(jnp.minimum(j, n_chunks - 1) * rc, rc), rc)
        hbuf[other, rows, :] = _rms(xpbuf[slot], gpre_ref[...]).astype(BF16)
        finish_rows(accbuf[other, rows, :], xebuf[slot], obuf.at[slot], nbuf.at[slot])
        accbuf[cur] += jnp.dot(abuf[...], wdbuf[slot], preferred_element_type=F32)
        up_chunk(jnp.where(j == n_ff - 1, other, cur), slot)
        return step_after(tj) + (1 - slot,)

    lax.fori_loop(0, n_steps, step, coords(0) + (jnp.int32(0),))

    end_slot = n_steps % 2
    wait_inputs(end_slot)

    @pl.when(finishes_rows(coords(n_steps - 2)))
    def _():
        wait_results_out(rc, obuf.at[end_slot], osem.at[end_slot], nbuf.at[end_slot], nsem.at[end_slot])

    @pl.when(finishes_rows(coords(n_steps - 1)))
    def _():
        last_bufs = (obuf.at[1 - end_slot], osem.at[1 - end_slot], nbuf.at[1 - end_slot], nsem.at[1 - end_slot])
        start_results_out(stage_inputs(coords(n_steps - 1))[3], rc, *last_bufs)
        wait_results_out(rc, *last_bufs)

    last_row0 = (n_tiles - 1) * tm

    def edge_bufs(c):
        return oedge.at[c % 2], edge_out_sem.at[c % 2], nedge.at[c % 2], edge_next_sem.at[c % 2]

    start_rows_in(last_row0, edge_rows, xedge.at[0], edge_in_sem.at[0])
    for c in range(n_edge):
        wait_rows_in(edge_rows, xedge.at[c % 2], edge_in_sem.at[c % 2])
        if c + 1 < n_edge:
            start_rows_in(last_row0 + (c + 1) * edge_rows, edge_rows, xedge.at[(c + 1) % 2],
                          edge_in_sem.at[(c + 1) % 2])
        if c >= 2:
            wait_results_out(edge_rows, *edge_bufs(c))
        finish_rows(accbuf[(n_tiles - 1) % 2, pl.ds(c * edge_rows, edge_rows), :], xedge[c % 2],
                    oedge.at[c % 2], nedge.at[c % 2])
        start_results_out(last_row0 + c * edge_rows, edge_rows, *edge_bufs(c))
    for c in range(max(n_edge - 2, 0), n_edge):
        wait_results_out(edge_rows, *edge_bufs(c))


def _ffn_pair(xa, xb, g_pre, wg, wu, wd, g_post, g_fin, g_next, *, final_norm, emit_h,
              tm=768, tf=FFN_CHUNK, rc=32, edge_rows=32):
    (n_a, d), n_b = xa.shape, xb.shape[0]
    d_ff = wg.shape[1]
    n_tok = n_a + n_b
    tm = min(tm, n_tok // 2)
    tf = min(tf, d_ff)
    edge_rows = min(edge_rows, tm)
    n_tiles, n_ff = n_tok // tm, d_ff // tf
    assert n_tok % tm == 0 and d_ff % tf == 0 and tm % rc == 0 and tm % edge_rows == 0
    assert n_a % edge_rows == 0 and n_a % rc == 0 and tm // rc <= n_ff and tm // edge_rows >= 2 and n_tiles >= 2
    hbm = pl.BlockSpec(memory_space=pl.ANY)
    vec = pl.BlockSpec(memory_space=pltpu.VMEM)
    out_shape = [jax.ShapeDtypeStruct((n_a, d), F32), jax.ShapeDtypeStruct((n_b, d), F32)]
    if emit_h:
        out_shape += [jax.ShapeDtypeStruct((n_a, d), BF16), jax.ShapeDtypeStruct((n_b, d), BF16)]
    next_rows = (rc, edge_rows) if emit_h else (8, 8)
    return pl.pallas_call(
        functools.partial(_ffn_pair_body, tm=tm, tf=tf, rc=rc, edge_rows=edge_rows, n_a=n_a, n_tiles=n_tiles,
                          n_ff=n_ff, final_norm=final_norm, emit_h=emit_h),
        in_specs=[hbm, hbm, vec, hbm, hbm, hbm, vec, vec, vec],
        out_specs=[hbm] * len(out_shape),
        out_shape=out_shape,
        scratch_shapes=[
            pltpu.VMEM((2, tm, d), BF16),
            pltpu.VMEM((2, tm, d), F32),
            pltpu.VMEM((tm, tf), F32),
            pltpu.VMEM((tm, tf), F32),
            pltpu.VMEM((tm, tf), BF16),
            pltpu.VMEM((2, d, tf), BF16),
            pltpu.VMEM((2, d, tf), BF16),
            pltpu.VMEM((2, tf, d), BF16),
            pltpu.VMEM((2, rc, d), F32),
            pltpu.VMEM((2, rc, d), F32),
            pltpu.VMEM((2, rc, d), F32),
            pltpu.VMEM((2, next_rows[0], d), BF16),
            pltpu.VMEM((2, edge_rows, d), F32),
            pltpu.VMEM((2, edge_rows, d), F32),
            pltpu.VMEM((2, next_rows[1], d), BF16),
            pltpu.SemaphoreType.DMA((2, 3)),
            pltpu.SemaphoreType.DMA((2,)),
            pltpu.SemaphoreType.DMA((2,)),
            pltpu.SemaphoreType.DMA((2,)),
            pltpu.SemaphoreType.DMA((2,)),
            pltpu.SemaphoreType.DMA((2,)),
            pltpu.SemaphoreType.DMA((2,)),
            pltpu.SemaphoreType.DMA((2,)),
        ],
        compiler_params=pltpu.CompilerParams(vmem_limit_bytes=VMEM_LIMIT),
        name="ffn_pair",
    )(xa, xb, g_pre, wg, wu, wd, g_post, g_fin, g_next)


def _inproj_body(h_ref, w_ref, qkv_ref, u_ref, qc_ref, *, n_qkv, n_u):
    j = pl.program_id(1)
    z = jnp.dot(h_ref[...], w_ref[...], preferred_element_type=F32)

    @pl.when(j < n_qkv)
    def _():
        qkv_ref[...] = z.astype(BF16)

    @pl.when(jnp.logical_and(j >= n_qkv, j < n_qkv + n_u))
    def _():
        u_ref[...] = z

    @pl.when(j >= n_qkv + n_u)
    def _():
        qc_ref[...] = z.astype(BF16)


def _inproj(h, w_in, *, na_width, pool_width, mem_width, tm=1024, tn=1024):
    n_tok, d = h.shape
    tm = min(tm, n_tok)
    tn = min(tn, pool_width, mem_width)
    assert n_tok % tm == 0 and na_width % tn == 0 and pool_width % tn == 0 and mem_width % tn == 0
    n_qkv = 3 * na_width // tn
    n_u = pool_width // tn
    n_qc = mem_width // tn
    return pl.pallas_call(
        functools.partial(_inproj_body, n_qkv=n_qkv, n_u=n_u),
        grid=(n_tok // tm, n_qkv + n_u + n_qc),
        in_specs=[
            pl.BlockSpec((tm, d), lambda i, j: (i, 0)),
            pl.BlockSpec((d, tn), lambda i, j: (0, j)),
        ],
        out_specs=[
            pl.BlockSpec((tm, tn), lambda i, j: (i, jnp.minimum(j, n_qkv - 1))),
            pl.BlockSpec((tm, tn), lambda i, j: (i, jnp.clip(j - n_qkv, 0, n_u - 1))),
            pl.BlockSpec((tm, tn), lambda i, j: (i, jnp.clip(j - n_qkv - n_u, 0, n_qc - 1))),
        ],
        out_shape=[
            jax.ShapeDtypeStruct((n_tok, 3 * na_width), BF16),
            jax.ShapeDtypeStruct((n_tok, pool_width), F32),
            jax.ShapeDtypeStruct((n_tok, mem_width), BF16),
        ],
        compiler_params=_params(("parallel", "arbitrary")),
        name="inproj",
    )(h, w_in)


def _natten_bias_tables(rpb):
    n_heads, n_dr, n_dc = rpb.shape
    kh, kw = (n_dr + 1) // 2, (n_dc + 1) // 2
    cols = np.arange(GRID_W)
    col_start = np.clip(cols - kw // 2, 0, GRID_W - kw)
    col_valid = (cols[None, :] >= col_start[:, None]) & (cols[None, :] < col_start[:, None] + kw)
    period = 2 * GRID_W
    lpad = GRID_W - kw
    padded = jnp.pad(rpb.astype(F32) * LOG2_E, ((0, 0), (0, 0), (lpad, period - n_dc - lpad)))
    skew = jnp.broadcast_to(padded[:, :, None, :], (n_heads, n_dr, GRID_W, period))
    skew = skew.reshape(n_heads, n_dr, GRID_W * period)[:, :, :GRID_W * (period - 1)]
    by_col = skew.reshape(n_heads, n_dr, GRID_W, period - 1)[:, :, :, GRID_W - 1:]
    by_col = jnp.where(jnp.asarray(col_valid)[None, None], by_col, MASK_VALUE)
    rpad = NAT_K_ROWS - kh
    wide = jnp.concatenate([by_col, by_col], axis=-1)
    wide = jnp.pad(wide, ((0, 0), (rpad, 2 * NAT_K_ROWS - n_dr - rpad), (0, 0), (0, 0)), constant_values=MASK_VALUE)
    return pl.pallas_call(
        functools.partial(_natten_bias_body, kh=kh),
        grid=(n_heads,),
        in_specs=[pl.BlockSpec((1,) + wide.shape[1:], lambda h: (h, 0, 0, 0))],
        out_specs=pl.BlockSpec((3, 1, NAT_Q_ROWS * GRID_W, NAT_K_ROWS * GRID_W), lambda h: (0, h, 0, 0)),
        out_shape=jax.ShapeDtypeStruct((3, n_heads, NAT_Q_ROWS * GRID_W, NAT_K_ROWS * GRID_W), F32),
        compiler_params=_params(("parallel",)),
        name="natten_bias",
    )(wide)


def _natten_bias_body(wide_ref, o_ref, *, kh):
    low_half = lax.broadcasted_iota(jnp.int32, (GRID_W, 2 * GRID_W), 1) < GRID_W
    masked = jnp.full((GRID_W, 2 * GRID_W), MASK_VALUE, F32)
    for t in range(3):
        for ri in range(NAT_Q_ROWS):
            r = (NAT_K_ROWS - NAT_Q_ROWS) // 2 * t + ri
            row_start = min(max(r - kh // 2, 0), NAT_K_ROWS - kh)
            for pair in range(NAT_K_ROWS // 2):
                halves = []
                for rj in (2 * pair, 2 * pair + 1):
                    valid = row_start <= rj < row_start + kh
                    halves.append(wide_ref[0, rj - r + NAT_K_ROWS - 1] if valid else masked)
                o_ref[t, 0, pl.ds(ri * GRID_W, GRID_W), pl.ds(pair * 2 * GRID_W, 2 * GRID_W)] = (
                    jnp.where(low_half, halves[0], halves[1]))


def _natten_body(q_ref, k0_ref, k1_ref, k2_ref, k3_ref, v0_ref, v1_ref, v2_ref, v3_ref, bias_ref, o_ref,
                 *, log2_scale, head_dim, n_sub):
    k_all = jnp.concatenate([k0_ref[0], k1_ref[0], k2_ref[0], k3_ref[0]], axis=0)
    v_all = jnp.concatenate([v0_ref[0], v1_ref[0], v2_ref[0], v3_ref[0]], axis=0)
    rows = q_ref.shape[1] // n_sub
    for h in range(q_ref.shape[2] // head_dim):
        cols = slice(h * head_dim, (h + 1) * head_dim)
        k = k_all[:, cols]
        v = v_all[:, cols]
        for i in range(n_sub):
            r = slice(i * rows, (i + 1) * rows)
            s = lax.dot_general(q_ref[0, r, cols], k, (((1,), (1,)), ((), ())), preferred_element_type=F32)
            s = s * log2_scale + bias_ref[0, h, r, :]
            m = jnp.max(s, axis=-1, keepdims=True)
            p = jnp.exp2(s - m)
            l = jnp.sum(p, axis=-1, keepdims=True)
            o = jnp.dot(p.astype(BF16), v, preferred_element_type=F32)
            o_ref[0, r, cols] = (o / l).astype(o_ref.dtype)


def _natten(qkv, bias, *, n_heads, head_dim, heads_per_step=4, n_sub=2):
    b, seq, _ = qkv.shape
    tq = NAT_Q_ROWS * GRID_W
    tk = NAT_K_CHUNK_ROWS * GRID_W
    n_kc = NAT_K_ROWS // NAT_K_CHUNK_ROWS
    hp = min(heads_per_step, n_heads)
    assert n_kc == 4 and seq % (NAT_K_ROWS * GRID_W) == 0 and n_heads % hp == 0
    n_hg = n_heads // hp
    n_qb = seq // tq
    n_kb = seq // tk
    lead = (NAT_K_ROWS - NAT_Q_ROWS) // 2 // NAT_K_CHUNK_ROWS

    def kv_spec(which, c):
        def index(bi, h, r):
            start = jnp.clip(r * (tq // tk) - lead, 0, n_kb - n_kc)
            return (bi, start + c, which * n_hg + h)
        return pl.BlockSpec((1, tk, hp * head_dim), index)

    def bias_index(bi, h, r):
        t = jnp.where(r == 0, 0, jnp.where(r == n_qb - 1, 2, 1))
        return (t, h, 0, 0)

    return pl.pallas_call(
        functools.partial(_natten_body, log2_scale=head_dim ** -0.5 * LOG2_E, head_dim=head_dim, n_sub=n_sub),
        grid=(b, n_hg, n_qb),
        in_specs=[pl.BlockSpec((1, tq, hp * head_dim), lambda bi, h, r: (bi, r, h))]
        + [kv_spec(1, c) for c in range(n_kc)]
        + [kv_spec(2, c) for c in range(n_kc)]
        + [pl.BlockSpec((1, hp, tq, NAT_K_ROWS * GRID_W), bias_index)],
        out_specs=pl.BlockSpec((1, tq, hp * head_dim), lambda bi, h, r: (bi, r, h)),
        out_shape=jax.ShapeDtypeStruct((b, seq, n_heads * head_dim), BF16),
        compiler_params=_params(("parallel", "parallel", "arbitrary")),
        name="natten",
    )(qkv, *([qkv] * (2 * n_kc)), bias)


def _pool_body(u_ref, prev_ref, next_ref, w_ref, scale_ref, o_ref, ext_ref, *, seq, halo, group_dim):
    t = pl.program_id(1)
    tt = u_ref.shape[1]
    u = u_ref[0]
    ext_ref[pl.ds(halo, tt), :] = u
    ext_ref[pl.ds(0, halo), :] = jnp.where(t > 0, prev_ref[0], 0.0)
    ext_ref[pl.ds(halo + tt, halo), :] = jnp.where(t < pl.num_programs(1) - 1, next_ref[0], 0.0)
    pos = t * tt + lax.broadcasted_iota(jnp.int32, (tt, 1), 0)
    for g, w in enumerate(POOL_WINDOWS):
        cols = pl.ds(g * group_dim, group_dim)
        total = ext_ref[pl.ds(halo - w // 2, tt), cols]
        for d in range(-w // 2 + 1, w // 2):
            total = total + ext_ref[pl.ds(halo + d, tt), cols]
        count = jnp.minimum(pos + w // 2, seq) - jnp.maximum(pos - w // 2, 0)
        pooled = total / count.astype(F32) - u[:, g * group_dim:(g + 1) * group_dim]
        mixed = jnp.dot(pooled.astype(BF16), w_ref[g], preferred_element_type=F32)
        o_ref[0, :, cols] = (mixed * scale_ref[:, cols]).astype(o_ref.dtype)


def _pool(u, w_pool, pool_scale, *, tt=512):
    b, seq, width = u.shape
    n_groups, group_dim, _ = w_pool.shape
    assert n_groups == len(POOL_WINDOWS)
    halo = 8
    assert max(POOL_WINDOWS) // 2 <= halo
    tt = min(tt, seq)
    assert seq % tt == 0 and tt % halo == 0
    hb = tt // halo
    return pl.pallas_call(
        functools.partial(_pool_body, seq=seq, halo=halo, group_dim=group_dim),
        grid=(b, seq // tt),
        in_specs=[
            pl.BlockSpec((1, tt, width), lambda bi, t: (bi, t, 0)),
            pl.BlockSpec((1, halo, width), lambda bi, t: (bi, jnp.maximum(t * hb - 1, 0), 0)),
            pl.BlockSpec((1, halo, width), lambda bi, t: (bi, jnp.minimum((t + 1) * hb, seq // halo - 1), 0)),
            pl.BlockSpec((n_groups, group_dim, group_dim), lambda bi, t: (0, 0, 0)),
            pl.BlockSpec((1, width), lambda bi, t: (0, 0)),
        ],
        out_specs=pl.BlockSpec((1, tt, width), lambda bi, t: (bi, t, 0)),
        out_shape=jax.ShapeDtypeStruct((b, seq, width), BF16),
        scratch_shapes=[pltpu.VMEM((tt + 2 * halo, width), F32)],
        compiler_params=_params(("parallel", "arbitrary")),
        name="pool",
    )(u, u, u, w_pool, pool_scale)


def _memkv_body(m_ref, g_ref, w_ref, o_ref):
    h = _rms(m_ref[...], g_ref[...]).astype(BF16)
    o_ref[...] = jnp.dot(h, w_ref[...], preferred_element_type=F32).astype(o_ref.dtype)


def _memkv(mem, g_mem, w_kv, *, tr=256, tn=512):
    n_rows, d = mem.shape
    width = w_kv.shape[1]
    tr = min(tr, n_rows)
    tn = min(tn, width)
    assert n_rows % tr == 0 and width % tn == 0
    return pl.pallas_call(
        _memkv_body,
        grid=(n_rows // tr, width // tn),
        in_specs=[
            pl.BlockSpec((tr, d), lambda i, j: (i, 0)),
            pl.BlockSpec((1, d), lambda i, j: (0, 0)),
            pl.BlockSpec((d, tn), lambda i, j: (0, j)),
        ],
        out_specs=pl.BlockSpec((tr, tn), lambda i, j: (i, j)),
        out_shape=jax.ShapeDtypeStruct((n_rows, width), BF16),
        compiler_params=_params(("parallel", "arbitrary")),
        name="memkv",
    )(mem, g_mem, w_kv)


def _memattn_body(q_ref, kv_ref, o_ref, *, n_heads, head_dim):
    scale = head_dim ** -0.5
    width = n_heads * head_dim
    for h in range(n_heads):
        cols = slice(h * head_dim, (h + 1) * head_dim)
        q = q_ref[0, :, cols]
        k = kv_ref[0, :, cols]
        v = kv_ref[0, :, width + h * head_dim: width + (h + 1) * head_dim]
        s = lax.dot_general(q, k, (((1,), (1,)), ((), ())), preferred_element_type=F32) * scale
        m = jnp.max(s, axis=-1, keepdims=True)
        p = jnp.exp(s - m)
        l = jnp.sum(p, axis=-1, keepdims=True)
        o = jnp.dot(p.astype(BF16), v, preferred_element_type=F32)
        o_ref[0, :, cols] = (o / l).astype(o_ref.dtype)


def _memattn(q, kv, *, n_heads, tq=512):
    b, seq, width = q.shape
    n_mem = kv.shape[1]
    tq = min(tq, seq)
    assert seq % tq == 0
    return pl.pallas_call(
        functools.partial(_memattn_body, n_heads=n_heads, head_dim=width // n_heads),
        grid=(b, seq // tq),
        in_specs=[
            pl.BlockSpec((1, tq, width), lambda bi, t: (bi, t, 0)),
            pl.BlockSpec((1, n_mem, 2 * width), lambda bi, t: (bi, 0, 0)),
        ],
        out_specs=pl.BlockSpec((1, tq, width), lambda bi, t: (bi, t, 0)),
        out_shape=jax.ShapeDtypeStruct((b, seq, width), BF16),
        compiler_params=_params(("parallel", "arbitrary")),
        name="memattn",
    )(q, kv)


def _merge_body(h_ref, a_ref, b_ref, c_ref, wga_ref, wgb_ref, wgc_ref, bg_ref, wa_ref, wb_ref, wc_ref, o_ref):
    h = h_ref[...]
    merged = None
    for i, (x_ref, wg_ref, w_ref) in enumerate(
            ((a_ref, wga_ref, wa_ref), (b_ref, wgb_ref, wb_ref), (c_ref, wgc_ref, wc_ref))):
        logits = jnp.dot(h, wg_ref[...], preferred_element_type=F32) + bg_ref[i:i + 1, :]
        gate = 1.0 / (1.0 + jnp.exp(-logits))
        y = jnp.dot(x_ref[...], w_ref[...], preferred_element_type=F32)
        merged = gate * y if merged is None else merged + gate * y
    o_ref[...] = merged.astype(o_ref.dtype)


def _merge(h, att_a, mix_b, att_c, w_in, gate_col0, b_gate, w_a, w_b, w_c, *, tm=512, tn=512):
    n_tok, d = h.shape
    tm = min(tm, n_tok)
    tn = min(tn, d)
    assert n_tok % tm == 0 and d % tn == 0 and gate_col0 % tn == 0
    nj = d // tn
    g0 = gate_col0 // tn

    def act(x):
        return pl.BlockSpec((tm, x.shape[1]), lambda i, j: (i, 0))

    def gate_w(branch):
        return pl.BlockSpec((d, tn), lambda i, j: (0, g0 + branch * nj + j))

    def out_w(w):
        return pl.BlockSpec((w.shape[0], tn), lambda i, j: (0, j))

    return pl.pallas_call(
        _merge_body,
        grid=(n_tok // tm, nj),
        in_specs=[act(h), act(att_a), act(mix_b), act(att_c),
                  gate_w(0), gate_w(1), gate_w(2),
                  pl.BlockSpec((b_gate.shape[0], tn), lambda i, j: (0, j)),
                  out_w(w_a), out_w(w_b), out_w(w_c)],
        out_specs=pl.BlockSpec((tm, tn), lambda i, j: (i, j)),
        out_shape=jax.ShapeDtypeStruct((n_tok, d), BF16),
        compiler_params=_params(("parallel", "arbitrary")),
        name="merge",
    )(h, att_a, mix_b, att_c, w_in, w_in, w_in, b_gate, w_a, w_b, w_c)


def _wo_body(x_ref, m_ref, w_ref, g_ref, o_ref, *, row_chunk):
    j = pl.program_id(1)
    tm = x_ref.shape[0]

    @pl.when(j == 0)
    def _():
        def prologue(rows):
            o_ref[rows, :] = jnp.zeros((row_chunk, o_ref.shape[1]), F32)
        _for_row_chunks(tm, row_chunk, prologue)

    o_ref[...] += jnp.dot(m_ref[...], w_ref[...], preferred_element_type=F32)

    @pl.when(j == pl.num_programs(1) - 1)
    def _():
        def epilogue(rows):
            o_ref[rows, :] = x_ref[rows, :] + _rms(o_ref[rows, :], g_ref[...])
        _for_row_chunks(tm, row_chunk, epilogue)


def _wo(x, merged, w_o, g_post, *, tm=512, tk=1024):
    n_tok, d = x.shape
    tm = min(tm, n_tok)
    tk = min(tk, d)
    assert n_tok % tm == 0 and d % tk == 0
    return pl.pallas_call(
        functools.partial(_wo_body, row_chunk=min(64, tm)),
        grid=(n_tok // tm, d // tk),
        in_specs=[
            pl.BlockSpec((tm, d), lambda i, j: (i, 0)),
            pl.BlockSpec((tm, tk), lambda i, j: (i, j)),
            pl.BlockSpec((tk, d), lambda i, j: (j, 0)),
            pl.BlockSpec((1, d), lambda i, j: (0, 0)),
        ],
        out_specs=pl.BlockSpec((tm, d), lambda i, j: (i, 0)),
        out_shape=jax.ShapeDtypeStruct((n_tok, d), F32),
        compiler_params=_params(("parallel", "arbitrary")),
        name="wo",
    )(x, merged, w_o, g_post)


def _mixer(x1, h, mem, p, bias_tables, *, b, seq):
    n_tok, d = x1.shape
    na_heads = p["rpb"].shape[0]
    n_groups, group_dim, _ = p["w_pool"].shape
    pool_width = n_groups * group_dim
    mem_width = p["w_mem_kv"].shape[1] // 2
    na_width = p["w_a_out"].shape[0]
    gate_col0 = 3 * na_width + pool_width + mem_width

    qkv, u, qc = _inproj(h, p["w_in"], na_width=na_width, pool_width=pool_width, mem_width=mem_width)
    att_a = _natten(qkv.reshape(b, seq, 3 * na_width), bias_tables,
                    n_heads=na_heads, head_dim=na_width // na_heads)
    mix_b = _pool(u.reshape(b, seq, pool_width), p["w_pool"], p["pool_scale"])
    kv = _memkv(mem.reshape(b * mem.shape[1], d), p["g_mem"], p["w_mem_kv"])
    att_c = _memattn(qc.reshape(b, seq, mem_width), kv.reshape(b, mem.shape[1], 2 * mem_width),
                     n_heads=MEM_HEADS)
    merged = _merge(h, att_a.reshape(n_tok, na_width), mix_b.reshape(n_tok, pool_width),
                    att_c.reshape(n_tok, mem_width), p["w_in"], gate_col0, p["b_gate"],
                    p["w_a_out"], p["w_b_out"], p["w_c_out"])
    return _wo(x1, merged, p["w_o"], p["g_mix_post"])


_MATMUL_WEIGHTS = ("w1_gate", "w1_up", "w1_down", "w_in", "w_pool", "w_mem_kv",
                   "w_a_out", "w_b_out", "w_c_out", "w_o", "w2_gate", "w2_up", "w2_down")
_ROW_VECTORS = ("g_ffn1_pre", "g_ffn1_post", "g_mix_pre", "pool_scale", "g_mem", "g_mix_post",
                "g_ffn2_pre", "g_ffn2_post", "g_final")


def kernel(x_prompt, x_sample, mem_prompt, mem_sample, g_ffn1_pre, w1_gate, w1_up, w1_down, g_ffn1_post, g_mix_pre, w_in, rpb, w_pool, pool_scale, g_mem, w_mem_kv, w_a_out, w_b_out, w_c_out, b_gate, w_o, g_mix_post, g_ffn2_pre, w2_gate, w2_up, w2_down, g_ffn2_post, g_final):
    stacked = dict(g_ffn1_pre=g_ffn1_pre, w1_gate=w1_gate, w1_up=w1_up, w1_down=w1_down,
                   g_ffn1_post=g_ffn1_post, g_mix_pre=g_mix_pre, w_in=w_in, rpb=rpb, w_pool=w_pool,
                   pool_scale=pool_scale, g_mem=g_mem, w_mem_kv=w_mem_kv, w_a_out=w_a_out,
                   w_b_out=w_b_out, w_c_out=w_c_out, b_gate=b_gate, w_o=w_o, g_mix_post=g_mix_post,
                   g_ffn2_pre=g_ffn2_pre, w2_gate=w2_gate, w2_up=w2_up, w2_down=w2_down,
                   g_ffn2_post=g_ffn2_post, g_final=g_final)
    y_prompt, y_sample = x_prompt, x_sample
    for layer in range(g_final.shape[0]):
        p = {name: a[layer] for name, a in stacked.items()}
        for name in _MATMUL_WEIGHTS:
            p[name] = p[name].astype(BF16)
        for name in _ROW_VECTORS:
            p[name] = p[name].reshape(1, -1)
        bias_tables = _natten_bias_tables(p["rpb"])
        (bp, sp, d), (bs, ss, _) = y_prompt.shape, y_sample.shape
        x1p, x1s, hp, hs = _ffn_pair(
            y_prompt.reshape(bp * sp, d), y_sample.reshape(bs * ss, d),
            p["g_ffn1_pre"], p["w1_gate"], p["w1_up"], p["w1_down"], p["g_ffn1_post"], p["g_final"],
            p["g_mix_pre"], final_norm=False, emit_h=True)
        x2p = _mixer(x1p, hp, mem_prompt, p, bias_tables, b=bp, seq=sp)
        x2s = _mixer(x1s, hs, mem_sample, p, bias_tables, b=bs, seq=ss)
        yp, ys = _ffn_pair(
            x2p, x2s, p["g_ffn2_pre"], p["w2_gate"], p["w2_up"], p["w2_down"], p["g_ffn2_post"], p["g_final"],
            p["g_final"], final_norm=True, emit_h=False)
        y_prompt, y_sample = yp.reshape(bp, sp, d), ys.reshape(bs, ss, d)
    return (y_prompt, y_sample)
```

```python
import functools

import numpy as np
import jax
import jax.numpy as jnp
from jax import lax
from jax.experimental import pallas as pl
from jax.experimental.pallas import tpu as pltpu

GRID_W = 64
POOL_WINDOWS = (2, 4, 8, 16)
MEM_HEADS = 4
RMS_EPS = 1e-6
MASK_VALUE = -1e30
LOG2_E = 1.4426950408889634

FFN_CHUNK = 256
NAT_Q_ROWS = 8
NAT_K_ROWS = 16
NAT_K_CHUNK_ROWS = 4
NAT_SUB_ROWS = 4

V7X_VMEM_BYTES = 64 * 1024 * 1024
VMEM_LIMIT = V7X_VMEM_BYTES - 6 * 1024 * 1024

F32 = jnp.float32
BF16 = jnp.bfloat16


def _rms(x, g):
    ms = jnp.mean(x * x, axis=-1, keepdims=True)
    return (x * lax.rsqrt(ms + RMS_EPS)) * g


def _silu_mul(g, u):
    return (g * (1.0 / (1.0 + jnp.exp(-g)))) * u


def _for_row_chunks(n_rows, chunk, fn):
    def body(c, carry):
        fn(pl.ds(pl.multiple_of(c * chunk, chunk), chunk))
        return carry
    lax.fori_loop(0, n_rows // chunk, body, 0)


def _params(sem):
    return pltpu.CompilerParams(dimension_semantics=sem, vmem_limit_bytes=VMEM_LIMIT)


def _ffn_pair_body(*refs, tm, tf, rc, edge_rows, n_a, n_tiles, n_ff, final_norm, emit_h):
    (xa_hbm, xb_hbm, gpre_ref, wg_hbm, wu_hbm, wd_hbm, gpost_ref, gfin_ref, gnext_ref), refs = refs[:9], refs[9:]
    if emit_h:
        (oa_hbm, ob_hbm, na_hbm, nb_hbm), refs = refs[:4], refs[4:]
    else:
        (oa_hbm, ob_hbm), refs = refs[:2], refs[2:]
        na_hbm = nb_hbm = None
    (hbuf, accbuf, gbuf, ubuf, abuf, wgbuf, wubuf, wdbuf, xpbuf, xebuf, obuf, nbuf, xedge, oedge, nedge,
     wsem, xpsem, xesem, osem, nsem, edge_in_sem, edge_out_sem, edge_next_sem) = refs

    n_chunks = tm // rc
    n_steps = n_tiles * n_ff
    n_edge = tm // edge_rows

    def step_after(tj):
        t, j = tj
        wrap = j == n_ff - 1
        return jnp.where(wrap, t + 1, t), jnp.where(wrap, 0, j + 1)

    def step_before(tj):
        t, j = tj
        wrap = j == 0
        return jnp.where(wrap, t - 1, t), jnp.where(wrap, n_ff - 1, j - 1)

    def coords(s):
        return jnp.int32(s // n_ff), jnp.int32(s % n_ff)

    def up_weight_copies(j_up, slot):
        cols = pl.ds(pl.multiple_of(j_up * tf, tf), tf)
        return [pltpu.make_async_copy(wg_hbm.at[:, cols], wgbuf.at[slot], wsem.at[slot, 0]),
                pltpu.make_async_copy(wu_hbm.at[:, cols], wubuf.at[slot], wsem.at[slot, 1])]

    def weight_copies(j_down, j_up, slot):
        rows = pl.ds(pl.multiple_of(j_down * tf, tf), tf)
        return up_weight_copies(j_up, slot) + [
            pltpu.make_async_copy(wd_hbm.at[rows, :], wdbuf.at[slot], wsem.at[slot, 2])]

    def start_rows_in(row0, n_rows, dst, sem):
        row0 = jnp.asarray(row0, jnp.int32)

        @pl.when(row0 < n_a)
        def _():
            pltpu.make_async_copy(xa_hbm.at[pl.ds(pl.multiple_of(row0, n_rows), n_rows), :], dst, sem).start()

        @pl.when(row0 >= n_a)
        def _():
            pltpu.make_async_copy(xb_hbm.at[pl.ds(pl.multiple_of(row0 - n_a, n_rows), n_rows), :], dst, sem).start()

    def wait_rows_in(n_rows, dst, sem):
        pltpu.make_async_copy(xa_hbm.at[pl.ds(0, n_rows), :], dst, sem).wait()

    def start_rows_out(row0, n_rows, src, sem, dst_a, dst_b):
        row0 = jnp.asarray(row0, jnp.int32)

        @pl.when(row0 < n_a)
        def _():
            pltpu.make_async_copy(src, dst_a.at[pl.ds(pl.multiple_of(row0, n_rows), n_rows), :], sem).start()

        @pl.when(row0 >= n_a)
        def _():
            pltpu.make_async_copy(src, dst_b.at[pl.ds(pl.multiple_of(row0 - n_a, n_rows), n_rows), :], sem).start()

    def wait_rows_out(n_rows, src, sem, dst_a):
        pltpu.make_async_copy(src, dst_a.at[pl.ds(0, n_rows), :], sem).wait()

    def start_results_out(row0, n_rows, out_src, out_sem, next_src, next_sem):
        start_rows_out(row0, n_rows, out_src, out_sem, oa_hbm, ob_hbm)
        if emit_h:
            start_rows_out(row0, n_rows, next_src, next_sem, na_hbm, nb_hbm)

    def wait_results_out(n_rows, out_src, out_sem, next_src, next_sem):
        wait_rows_out(n_rows, out_src, out_sem, oa_hbm)
        if emit_h:
            wait_rows_out(n_rows, next_src, next_sem, na_hbm)

    def stage_inputs(tj):
        t, j = tj
        j_up = jnp.where(j == n_ff - 1, 0, j + 1)
        chunk = jnp.minimum(j, n_chunks - 1)
        row_pre = jnp.minimum(t + 1, n_tiles - 1) * tm + chunk * rc
        row_epi = jnp.clip(t - 1, 0, n_tiles - 1) * tm + chunk * rc
        return j, j_up, row_pre, row_epi, chunk

    def start_inputs(tj, slot):
        j_down, j_up, row_pre, row_epi, _ = stage_inputs(tj)
        for c in weight_copies(j_down, j_up, slot):
            c.start()
        start_rows_in(row_pre, rc, xpbuf.at[slot], xpsem.at[slot])
        start_rows_in(row_epi, rc, xebuf.at[slot], xesem.at[slot])

    def wait_inputs(slot):
        for c in weight_copies(0, 0, slot):
            c.wait()
        wait_rows_in(rc, xpbuf.at[slot], xpsem.at[slot])
        wait_rows_in(rc, xebuf.at[slot], xesem.at[slot])

    def finishes_rows(tj):
        t, j = tj
        return jnp.logical_and(t >= 1, j < n_chunks)

    def finish_rows(acc_rows, x_rows, out_ref, next_ref):
        out = x_rows + 0.5 * _rms(acc_rows, gpost_ref[...])
        if final_norm:
            out = _rms(out, gfin_ref[...])
        out_ref[...] = out
        if emit_h:
            next_ref[...] = _rms(out, gnext_ref[...]).astype(BF16)

    def up_chunk(h_sel, slot):
        h = hbuf[h_sel]
        gbuf[...] = jnp.dot(h, wgbuf[slot], preferred_element_type=F32)
        ubuf[...] = jnp.dot(h, wubuf[slot], preferred_element_type=F32)

    start_rows_in(0, edge_rows, xedge.at[0], edge_in_sem.at[0])
    for c in range(n_edge):
        wait_rows_in(edge_rows, xedge.at[c % 2], edge_in_sem.at[c % 2])
        if c + 1 < n_edge:
            start_rows_in((c + 1) * edge_rows, edge_rows, xedge.at[(c + 1) % 2], edge_in_sem.at[(c + 1) % 2])
        hbuf[0, pl.ds(c * edge_rows, edge_rows), :] = _rms(xedge[c % 2], gpre_ref[...]).astype(BF16)
        accbuf[1, pl.ds(c * edge_rows, edge_rows), :] = jnp.zeros((edge_rows, accbuf.shape[2]), F32)
    for c in up_weight_copies(0, 1):
        c.start()
    for c in up_weight_copies(0, 1):
        c.wait()
    up_chunk(0, 1)
    start_inputs(coords(0), 0)

    def step(_, carry):
        t, j, slot = carry
        tj = (t, j)
        before = step_before(tj)
        cur = t % 2
        other = 1 - cur
        start_inputs(step_after(tj), 1 - slot)

        @pl.when(finishes_rows(step_before(before)))
        def _():
            wait_results_out(rc, obuf.at[slot], osem.at[slot], nbuf.at[slot], nsem.at[slot])

        @pl.when(finishes_rows(before))
        def _():
            start_results_out(stage_inputs(before)[3], rc, obuf.at[1 - slot], osem.at[1 - slot],
                              nbuf.at[1 - slot], nsem.at[1 - slot])

        @pl.when(j == 0)
        def _():
            def zero(zrows):
                accbuf[cur, zrows, :] = jnp.zeros((edge_rows, accbuf.shape[2]), F32)
            _for_row_chunks(tm, edge_rows, zero)

        wait_inputs(slot)
        abuf[...] = _silu_mul(gbuf[...], ubuf[...]).astype(BF16)
        rows = pl.ds(pl.multiple_of(jnp.minimum(j, n_chunks - 1) * rc, rc), rc)
        hbuf[other, rows, :] = _rms(xpbuf[slot], gpre_ref[...]).astype(BF16)
        finish_rows(accbuf[other, rows, :], xebuf[slot], obuf.at[slot], nbuf.at[slot])
        accbuf[cur] += jnp.dot(abuf[...], wdbuf[slot], preferred_element_type=F32)
        up_chunk(jnp.where(j == n_ff - 1, other, cur), slot)
        return step_after(tj) + (1 - slot,)

    lax.fori_loop(0, n_steps, step, coords(0) + (jnp.int32(0),))

    end_slot = n_steps % 2
    wait_inputs(end_slot)

    @pl.when(finishes_rows(coords(n_steps - 2)))
    def _():
        wait_results_out(rc, obuf.at[end_slot], osem.at[end_slot], nbuf.at[end_slot], nsem.at[end_slot])

    @pl.when(finishes_rows(coords(n_steps - 1)))
    def _():
        last_bufs = (obuf.at[1 - end_slot], osem.at[1 - end_slot], nbuf.at[1 - end_slot], nsem.at[1 - end_slot])
        start_results_out(stage_inputs(coords(n_steps - 1))[3], rc, *last_bufs)
        wait_results_out(rc, *last_bufs)

    last_row0 = (n_tiles - 1) * tm

    def edge_bufs(c):
        return oedge.at[c % 2], edge_out_sem.at[c % 2], nedge.at[c % 2], edge_next_sem.at[c % 2]

    start_rows_in(last_row0, edge_rows, xedge.at[0], edge_in_sem.at[0])
    for c in range(n_edge):
        wait_rows_in(edge_rows, xedge.at[c % 2], edge_in_sem.at[c % 2])
        if c + 1 < n_edge:
            start_rows_in(last_row0 + (c + 1) * edge_rows, edge_rows, xedge.at[(c + 1) % 2],
                          edge_in_sem.at[(c + 1) % 2])
        if c >= 2:
            wait_results_out(edge_rows, *edge_bufs(c))
        finish_rows(accbuf[(n_tiles - 1) % 2, pl.ds(c * edge_rows, edge_rows), :], xedge[c % 2],
                    oedge.at[c % 2], nedge.at[c % 2])
        start_results_out(last_row0 + c * edge_rows, edge_rows, *edge_bufs(c))
    for c in range(max(n_edge - 2, 0), n_edge):
        wait_results_out(edge_rows, *edge_bufs(c))


def _ffn_pair(xa, xb, g_pre, wg, wu, wd, g_post, g_fin, g_next, *, final_norm, emit_h,
              tm=768, tf=FFN_CHUNK, rc=32, edge_rows=32):
    (n_a, d), n_b = xa.shape, xb.shape[0]
    d_ff = wg.shape[1]
    n_tok = n_a + n_b
    tm = min(tm, n_tok // 2)
    tf = min(tf, d_ff)
    edge_rows = min(edge_rows, tm)
    n_tiles, n_ff = n_tok // tm, d_ff // tf
    assert n_tok % tm == 0 and d_ff % tf == 0 and tm % rc == 0 and tm % edge_rows == 0
    assert n_a % edge_rows == 0 and n_a % rc == 0 and tm // rc <= n_ff and tm // edge_rows >= 2 and n_tiles >= 2
    hbm = pl.BlockSpec(memory_space=pl.ANY)
    vec = pl.BlockSpec(memory_space=pltpu.VMEM)
    out_shape = [jax.ShapeDtypeStruct((n_a, d), F32), jax.ShapeDtypeStruct((n_b, d), F32)]
    if emit_h:
        out_shape += [jax.ShapeDtypeStruct((n_a, d), BF16), jax.ShapeDtypeStruct((n_b, d), BF16)]
    next_rows = (rc, edge_rows) if emit_h else (8, 8)
    return pl.pallas_call(
        functools.partial(_ffn_pair_body, tm=tm, tf=tf, rc=rc, edge_rows=edge_rows, n_a=n_a, n_tiles=n_tiles,
                          n_ff=n_ff, final_norm=final_norm, emit_h=emit_h),
        in_specs=[hbm, hbm, vec, hbm, hbm, hbm, vec, vec, vec],
        out_specs=[hbm] * len(out_shape),
        out_shape=out_shape,
        scratch_shapes=[
            pltpu.VMEM((2, tm, d), BF16),
            pltpu.VMEM((2, tm, d), F32),
            pltpu.VMEM((tm, tf), F32),
            pltpu.VMEM((tm, tf), F32),
            pltpu.VMEM((tm, tf), BF16),
            pltpu.VMEM((2, d, tf), BF16),
            pltpu.VMEM((2, d, tf), BF16),
            pltpu.VMEM((2, tf, d), BF16),
            pltpu.VMEM((2, rc, d), F32),
            pltpu.VMEM((2, rc, d), F32),
            pltpu.VMEM((2, rc, d), F32),
            pltpu.VMEM((2, next_rows[0], d), BF16),
            pltpu.VMEM((2, edge_rows, d), F32),
            pltpu.VMEM((2, edge_rows, d), F32),
            pltpu.VMEM((2, next_rows[1], d), BF16),
            pltpu.SemaphoreType.DMA((2, 3)),
            pltpu.SemaphoreType.DMA((2,)),
            pltpu.SemaphoreType.DMA((2,)),
            pltpu.SemaphoreType.DMA((2,)),
            pltpu.SemaphoreType.DMA((2,)),
            pltpu.SemaphoreType.DMA((2,)),
            pltpu.SemaphoreType.DMA((2,)),
            pltpu.SemaphoreType.DMA((2,)),
        ],
        compiler_params=pltpu.CompilerParams(vmem_limit_bytes=VMEM_LIMIT),
        name="ffn_pair",
    )(xa, xb, g_pre, wg, wu, wd, g_post, g_fin, g_next)


def _inproj_body(h_ref, w_ref, qkv_ref, u_ref, qc_ref, *, n_qkv, n_u):
    j = pl.program_id(1)
    z = jnp.dot(h_ref[...], w_ref[...], preferred_element_type=F32)

    @pl.when(j < n_qkv)
    def _():
        qkv_ref[...] = z.astype(BF16)

    @pl.when(jnp.logical_and(j >= n_qkv, j < n_qkv + n_u))
    def _():
        u_ref[...] = z

    @pl.when(j >= n_qkv + n_u)
    def _():
        qc_ref[...] = z.astype(BF16)


def _inproj(h, w_in, *, na_width, pool_width, mem_width, tm=1024, tn=1024):
    n_tok, d = h.shape
    tm = min(tm, n_tok)
    tn = min(tn, pool_width, mem_width)
    assert n_tok % tm == 0 and na_width % tn == 0 and pool_width % tn == 0 and mem_width % tn == 0
    n_qkv = 3 * na_width // tn
    n_u = pool_width // tn
    n_qc = mem_width // tn
    return pl.pallas_call(
        functools.partial(_inproj_body, n_qkv=n_qkv, n_u=n_u),
        grid=(n_tok // tm, n_qkv + n_u + n_qc),
        in_specs=[
            pl.BlockSpec((tm, d), lambda i, j: (i, 0)),
            pl.BlockSpec((d, tn), lambda i, j: (0, j)),
        ],
        out_specs=[
            pl.BlockSpec((tm, tn), lambda i, j: (i, jnp.minimum(j, n_qkv - 1))),
            pl.BlockSpec((tm, tn), lambda i, j: (i, jnp.clip(j - n_qkv, 0, n_u - 1))),
            pl.BlockSpec((tm, tn), lambda i, j: (i, jnp.clip(j - n_qkv - n_u, 0, n_qc - 1))),
        ],
        out_shape=[
            jax.ShapeDtypeStruct((n_tok, 3 * na_width), BF16),
            jax.ShapeDtypeStruct((n_tok, pool_width), F32),
            jax.ShapeDtypeStruct((n_tok, mem_width), BF16),
        ],
        compiler_params=_params(("parallel", "arbitrary")),
        name="inproj",
    )(h, w_in)


def _natten_bias_tables(rpb):
    n_heads, n_dr, n_dc = rpb.shape
    kh, kw = (n_dr + 1) // 2, (n_dc + 1) // 2
    cols = np.arange(GRID_W)
    col_start = np.clip(cols - kw // 2, 0, GRID_W - kw)
    col_valid = (cols[None, :] >= col_start[:, None]) & (cols[None, :] < col_start[:, None] + kw)
    period = 2 * GRID_W
    lpad = GRID_W - kw
    padded = jnp.pad(rpb.astype(F32) * LOG2_E, ((0, 0), (0, 0), (lpad, period - n_dc - lpad)))
    skew = jnp.broadcast_to(padded[:, :, None, :], (n_heads, n_dr, GRID_W, period))
    skew = skew.reshape(n_heads, n_dr, GRID_W * period)[:, :, :GRID_W * (period - 1)]
    by_col = skew.reshape(n_heads, n_dr, GRID_W, period - 1)[:, :, :, GRID_W - 1:]
    by_col = jnp.where(jnp.asarray(col_valid)[None, None], by_col, MASK_VALUE).transpose(0, 1, 3, 2)
    rpad = NAT_K_ROWS - kh
    wide = jnp.concatenate([by_col, by_col], axis=-1)
    wide = jnp.pad(wide, ((0, 0), (rpad, 2 * NAT_K_ROWS - n_dr - rpad), (0, 0), (0, 0)), constant_values=MASK_VALUE)
    n_sub = NAT_Q_ROWS // NAT_SUB_ROWS
    block = (n_sub, _natten_window_rows(kh) * GRID_W, NAT_SUB_ROWS * GRID_W)
    return pl.pallas_call(
        functools.partial(_natten_bias_body, kh=kh),
        grid=(n_heads,),
        in_specs=[pl.BlockSpec((1,) + wide.shape[1:], lambda h: (h, 0, 0, 0))],
        out_specs=pl.BlockSpec((3, 1) + block, lambda h: (0, h, 0, 0, 0)),
        out_shape=jax.ShapeDtypeStruct((3, n_heads) + block, F32),
        compiler_params=_params(("parallel",)),
        name="natten_bias",
    )(wide)


def _natten_row_start(r, kh):
    return min(max(r - kh // 2, 0), NAT_K_ROWS - kh)


def _natten_window_rows(kh):
    return NAT_SUB_ROWS - 1 + kh


def _natten_window_start(t, sub, kh):
    r_first = (NAT_K_ROWS - NAT_Q_ROWS) // 2 * t + sub * NAT_SUB_ROWS
    return min(_natten_row_start(r_first, kh), NAT_K_ROWS - _natten_window_rows(kh))


def _natten_bias_body(wide_ref, o_ref, *, kh):
    low_half = lax.broadcasted_iota(jnp.int32, (GRID_W, 2 * GRID_W), 1) < GRID_W
    masked = jnp.full((GRID_W, 2 * GRID_W), MASK_VALUE, F32)
    for t in range(3):
        for sub in range(NAT_Q_ROWS // NAT_SUB_ROWS):
            first_key_row = _natten_window_start(t, sub, kh)
            for kl in range(_natten_window_rows(kh)):
                rj = first_key_row + kl
                for pair in range(NAT_SUB_ROWS // 2):
                    halves = []
                    for ri in (sub * NAT_SUB_ROWS + 2 * pair, sub * NAT_SUB_ROWS + 2 * pair + 1):
                        r = (NAT_K_ROWS - NAT_Q_ROWS) // 2 * t + ri
                        valid = _natten_row_start(r, kh) <= rj < _natten_row_start(r, kh) + kh
                        halves.append(wide_ref[0, rj - r + NAT_K_ROWS - 1] if valid else masked)
                    o_ref[t, 0, sub, pl.ds(kl * GRID_W, GRID_W), pl.ds(pair * 2 * GRID_W, 2 * GRID_W)] = (
                        jnp.where(low_half, halves[0], halves[1]))


def _natten_body(q_ref, k0_ref, k1_ref, k2_ref, k3_ref, v0_ref, v1_ref, v2_ref, v3_ref, bias_ref, o_ref,
                 kbuf, vbuf, *, log2_scale, head_dim, kh):
    r = pl.program_id(2)
    table = jnp.where(r == 0, 0, jnp.where(r == pl.num_programs(2) - 1, 2, 1))
    chunk = k0_ref.shape[1]
    for c, (k_ref, v_ref) in enumerate(((k0_ref, v0_ref), (k1_ref, v1_ref), (k2_ref, v2_ref), (k3_ref, v3_ref))):
        kbuf[pl.ds(c * chunk, chunk), :] = k_ref[0]
        vbuf[pl.ds(c * chunk, chunk), :] = v_ref[0]
    n_win = _natten_window_rows(kh) * GRID_W
    n_q = NAT_SUB_ROWS * GRID_W
    blocks = []
    for sub in range(NAT_Q_ROWS // NAT_SUB_ROWS):
        starts = [_natten_window_start(t, sub, kh) * GRID_W for t in range(3)]
        start = jnp.where(table == 0, starts[0], jnp.where(table == 1, starts[1], starts[2]))
        keys = pl.ds(pl.multiple_of(start, GRID_W), n_win)
        for h in range(q_ref.shape[2] // head_dim):
            blocks.append((sub, h, keys, pl.ds(sub * n_q, n_q), slice(h * head_dim, (h + 1) * head_dim)))
    scores = [lax.dot_general(kbuf[keys, cols], q_ref[0, queries, cols], (((1,), (1,)), ((), ())),
                              preferred_element_type=F32)
              for _, _, keys, queries, cols in blocks]
    probs = []
    for (sub, h, _, _, _), s in zip(blocks, scores):
        s = s * log2_scale + bias_ref[0, h, sub]
        p = jnp.exp2(s - jnp.max(s, axis=0, keepdims=True))
        probs.append((p.astype(BF16), jnp.sum(p, axis=0, keepdims=True)))
    for (_, _, keys, queries, cols), (p, l) in zip(blocks, probs):
        o = lax.dot_general(vbuf[keys, cols], p, (((0,), (0,)), ((), ())), preferred_element_type=F32)
        o_ref[0, queries, cols] = (o / l).T.astype(o_ref.dtype)


def _natten(qkv, bias, *, n_heads, head_dim, kh, heads_per_step=4):
    b, seq, _ = qkv.shape
    tq = NAT_Q_ROWS * GRID_W
    tk = NAT_K_CHUNK_ROWS * GRID_W
    n_kc = NAT_K_ROWS // NAT_K_CHUNK_ROWS
    hp = min(heads_per_step, n_heads)
    assert n_kc == 4 and seq % (NAT_K_ROWS * GRID_W) == 0 and n_heads % hp == 0
    n_hg = n_heads // hp
    n_qb = seq // tq
    n_kb = seq // tk
    lead = (NAT_K_ROWS - NAT_Q_ROWS) // 2 // NAT_K_CHUNK_ROWS

    def kv_spec(which, c):
        def index(bi, h, r):
            start = jnp.clip(r * (tq // tk) - lead, 0, n_kb - n_kc)
            return (bi, start + c, which * n_hg + h)
        return pl.BlockSpec((1, tk, hp * head_dim), index)

    def bias_index(bi, h, r):
        t = jnp.where(r == 0, 0, jnp.where(r == n_qb - 1, 2, 1))
        return (t, h, 0, 0, 0)

    return pl.pallas_call(
        functools.partial(_natten_body, log2_scale=head_dim ** -0.5 * LOG2_E, head_dim=head_dim, kh=kh),
        grid=(b, n_hg, n_qb),
        in_specs=[pl.BlockSpec((1, tq, hp * head_dim), lambda bi, h, r: (bi, r, h))]
        + [kv_spec(1, c) for c in range(n_kc)]
        + [kv_spec(2, c) for c in range(n_kc)]
        + [pl.BlockSpec((1, hp) + bias.shape[2:], bias_index)],
        out_specs=pl.BlockSpec((1, tq, hp * head_dim), lambda bi, h, r: (bi, r, h)),
        out_shape=jax.ShapeDtypeStruct((b, seq, n_heads * head_dim), BF16),
        scratch_shapes=[pltpu.VMEM((NAT_K_ROWS * GRID_W, hp * head_dim), BF16),
                        pltpu.VMEM((NAT_K_ROWS * GRID_W, hp * head_dim), BF16)],
        compiler_params=_params(("parallel", "parallel", "arbitrary")),
        name="natten",
    )(qkv, *([qkv] * (2 * n_kc)), bias)


def _pool_body(u_ref, prev_ref, next_ref, w_ref, scale_ref, o_ref, ext_ref, *, seq, halo, group_dim):
    t = pl.program_id(1)
    tt = u_ref.shape[1]
    u = u_ref[0]
    ext_ref[pl.ds(halo, tt), :] = u
    ext_ref[pl.ds(0, halo), :] = jnp.where(t > 0, prev_ref[0], 0.0)
    ext_ref[pl.ds(halo + tt, halo), :] = jnp.where(t < pl.num_programs(1) - 1, next_ref[0], 0.0)
    pos = t * tt + lax.broadcasted_iota(jnp.int32, (tt, 1), 0)
    for g, w in enumerate(POOL_WINDOWS):
        cols = pl.ds(g * group_dim, group_dim)
        total = ext_ref[pl.ds(halo - w // 2, tt), cols]
        for d in range(-w // 2 + 1, w // 2):
            total = total + ext_ref[pl.ds(halo + d, tt), cols]
        count = jnp.minimum(pos + w // 2, seq) - jnp.maximum(pos - w // 2, 0)
        pooled = total / count.astype(F32) - u[:, g * group_dim:(g + 1) * group_dim]
        mixed = jnp.dot(pooled.astype(BF16), w_ref[g], preferred_element_type=F32)
        o_ref[0, :, cols] = (mixed * scale_ref[:, cols]).astype(o_ref.dtype)


def _pool(u, w_pool, pool_scale, *, tt=512):
    b, seq, width = u.shape
    n_groups, group_dim, _ = w_pool.shape
    assert n_groups == len(POOL_WINDOWS)
    halo = 8
    assert max(POOL_WINDOWS) // 2 <= halo
    tt = min(tt, seq)
    assert seq % tt == 0 and tt % halo == 0
    hb = tt // halo
    return pl.pallas_call(
        functools.partial(_pool_body, seq=seq, halo=halo, group_dim=group_dim),
        grid=(b, seq // tt),
        in_specs=[
            pl.BlockSpec((1, tt, width), lambda bi, t: (bi, t, 0)),
            pl.BlockSpec((1, halo, width), lambda bi, t: (bi, jnp.maximum(t * hb - 1, 0), 0)),
            pl.BlockSpec((1, halo, width), lambda bi, t: (bi, jnp.minimum((t + 1) * hb, seq // halo - 1), 0)),
            pl.BlockSpec((n_groups, group_dim, group_dim), lambda bi, t: (0, 0, 0)),
            pl.BlockSpec((1, width), lambda bi, t: (0, 0)),
        ],
        out_specs=pl.BlockSpec((1, tt, width), lambda bi, t: (bi, t, 0)),
        out_shape=jax.ShapeDtypeStruct((b, seq, width), BF16),
        scratch_shapes=[pltpu.VMEM((tt + 2 * halo, width), F32)],
        compiler_params=_params(("parallel", "arbitrary")),
        name="pool",
    )(u, u, u, w_pool, pool_scale)


def _memkv_body(m_ref, g_ref, w_ref, o_ref):
    h = _rms(m_ref[...], g_ref[...]).astype(BF16)
    o_ref[...] = jnp.dot(h, w_ref[...], preferred_element_type=F32).astype(o_ref.dtype)


def _memkv(mem, g_mem, w_kv, *, tr=256, tn=512):
    n_rows, d = mem.shape
    width = w_kv.shape[1]
    tr = min(tr, n_rows)
    tn = min(tn, width)
    assert n_rows % tr == 0 and width % tn == 0
    return pl.pallas_call(
        _memkv_body,
        grid=(n_rows // tr, width // tn),
        in_specs=[
            pl.BlockSpec((tr, d), lambda i, j: (i, 0)),
            pl.BlockSpec((1, d), lambda i, j: (0, 0)),
            pl.BlockSpec((d, tn), lambda i, j: (0, j)),
        ],
        out_specs=pl.BlockSpec((tr, tn), lambda i, j: (i, j)),
        out_shape=jax.ShapeDtypeStruct((n_rows, width), BF16),
        compiler_params=_params(("parallel", "arbitrary")),
        name="memkv",
    )(mem, g_mem, w_kv)


def _memattn_body(q_ref, kv_ref, o_ref, *, n_heads, head_dim):
    scale = head_dim ** -0.5
    width = n_heads * head_dim
    for h in range(n_heads):
        cols = slice(h * head_dim, (h + 1) * head_dim)
        q = q_ref[0, :, cols]
        k = kv_ref[0, :, cols]
        v = kv_ref[0, :, width + h * head_dim: width + (h + 1) * head_dim]
        s = lax.dot_general(q, k, (((1,), (1,)), ((), ())), preferred_element_type=F32) * scale
        m = jnp.max(s, axis=-1, keepdims=True)
        p = jnp.exp(s - m)
        l = jnp.sum(p, axis=-1, keepdims=True)
        o = jnp.dot(p.astype(BF16), v, preferred_element_type=F32)
        o_ref[0, :, cols] = (o / l).astype(o_ref.dtype)


def _memattn(q, kv, *, n_heads, tq=512):
    b, seq, width = q.shape
    n_mem = kv.shape[1]
    tq = min(tq, seq)
    assert seq % tq == 0
    return pl.pallas_call(
        functools.partial(_memattn_body, n_heads=n_heads, head_dim=width // n_heads),
        grid=(b, seq // tq),
        in_specs=[
            pl.BlockSpec((1, tq, width), lambda bi, t: (bi, t, 0)),
            pl.BlockSpec((1, n_mem, 2 * width), lambda bi, t: (bi, 0, 0)),
        ],
        out_specs=pl.BlockSpec((1, tq, width), lambda bi, t: (bi, t, 0)),
        out_shape=jax.ShapeDtypeStruct((b, seq, width), BF16),
        compiler_params=_params(("parallel", "arbitrary")),
        name="memattn",
    )(q, kv)


def _merge_body(h_ref, a_ref, b_ref, c_ref, wga_ref, wgb_ref, wgc_ref, bg_ref, wa_ref, wb_ref, wc_ref, o_ref):
    h = h_ref[...]
    merged = None
    for i, (x_ref, wg_ref, w_ref) in enumerate(
            ((a_ref, wga_ref, wa_ref), (b_ref, wgb_ref, wb_ref), (c_ref, wgc_ref, wc_ref))):
        logits = jnp.dot(h, wg_ref[...], preferred_element_type=F32) + bg_ref[i:i + 1, :]
        gate = 1.0 / (1.0 + jnp.exp(-logits))
        y = jnp.dot(x_ref[...], w_ref[...], preferred_element_type=F32)
        merged = gate * y if merged is None else merged + gate * y
    o_ref[...] = merged.astype(o_ref.dtype)


def _merge(h, att_a, mix_b, att_c, w_in, gate_col0, b_gate, w_a, w_b, w_c, *, tm=512, tn=512):
    n_tok, d = h.shape
    tm = min(tm, n_tok)
    tn = min(tn, d)
    assert n_tok % tm == 0 and d % tn == 0 and gate_col0 % tn == 0
    nj = d // tn
    g0 = gate_col0 // tn

    def act(x):
        return pl.BlockSpec((tm, x.shape[1]), lambda i, j: (i, 0))

    def gate_w(branch):
        return pl.BlockSpec((d, tn), lambda i, j: (0, g0 + branch * nj + j))

    def out_w(w):
        return pl.BlockSpec((w.shape[0], tn), lambda i, j: (0, j))

    return pl.pallas_call(
        _merge_body,
        grid=(n_tok // tm, nj),
        in_specs=[act(h), act(att_a), act(mix_b), act(att_c),
                  gate_w(0), gate_w(1), gate_w(2),
                  pl.BlockSpec((b_gate.shape[0], tn), lambda i, j: (0, j)),
                  out_w(w_a), out_w(w_b), out_w(w_c)],
        out_specs=pl.BlockSpec((tm, tn), lambda i, j: (i, j)),
        out_shape=jax.ShapeDtypeStruct((n_tok, d), BF16),
        compiler_params=_params(("parallel", "arbitrary")),
        name="merge",
    )(h, att_a, mix_b, att_c, w_in, w_in, w_in, b_gate, w_a, w_b, w_c)


def _wo_body(x_ref, m_ref, w_ref, g_ref, o_ref, *, row_chunk):
    j = pl.program_id(1)
    tm = x_ref.shape[0]

    @pl.when(j == 0)
    def _():
        def prologue(rows):
            o_ref[rows, :] = jnp.zeros((row_chunk, o_ref.shape[1]), F32)
        _for_row_chunks(tm, row_chunk, prologue)

    o_ref[...] += jnp.dot(m_ref[...], w_ref[...], preferred_element_type=F32)

    @pl.when(j == pl.num_programs(1) - 1)
    def _():
        def epilogue(rows):
            o_ref[rows, :] = x_ref[rows, :] + _rms(o_ref[rows, :], g_ref[...])
        _for_row_chunks(tm, row_chunk, epilogue)


def _wo(x, merged, w_o, g_post, *, tm=512, tk=1024):
    n_tok, d = x.shape
    tm = min(tm, n_tok)
    tk = min(tk, d)
    assert n_tok % tm == 0 and d % tk == 0
    return pl.pallas_call(
        functools.partial(_wo_body, row_chunk=min(64, tm)),
        grid=(n_tok // tm, d // tk),
        in_specs=[
            pl.BlockSpec((tm, d), lambda i, j: (i, 0)),
            pl.BlockSpec((tm, tk), lambda i, j: (i, j)),
            pl.BlockSpec((tk, d), lambda i, j: (j, 0)),
            pl.BlockSpec((1, d), lambda i, j: (0, 0)),
        ],
        out_specs=pl.BlockSpec((tm, d), lambda i, j: (i, 0)),
        out_shape=jax.ShapeDtypeStruct((n_tok, d), F32),
        compiler_params=_params(("parallel", "arbitrary")),
        name="wo",
    )(x, merged, w_o, g_post)


def _mixer(x1, h, mem, p, bias_tables, *, b, seq):
    n_tok, d = x1.shape
    na_heads = p["rpb"].shape[0]
    n_groups, group_dim, _ = p["w_pool"].shape
    pool_width = n_groups * group_dim
    mem_width = p["w_mem_kv"].shape[1] // 2
    na_width = p["w_a_out"].shape[0]
    gate_col0 = 3 * na_width + pool_width + mem_width

    qkv, u, qc = _inproj(h, p["w_in"], na_width=na_width, pool_width=pool_width, mem_width=mem_width)
    att_a = _natten(qkv.reshape(b, seq, 3 * na_width), bias_tables,
                    n_heads=na_heads, head_dim=na_width // na_heads, kh=(p["rpb"].shape[1] + 1) // 2)
    mix_b = _pool(u.reshape(b, seq, pool_width), p["w_pool"], p["pool_scale"])
    kv = _memkv(mem.reshape(b * mem.shape[1], d), p["g_mem"], p["w_mem_kv"])
    att_c = _memattn(qc.reshape(b, seq, mem_width), kv.reshape(b, mem.shape[1], 2 * mem_width),
                     n_heads=MEM_HEADS)
    merged = _merge(h, att_a.reshape(n_tok, na_width), mix_b.reshape(n_tok, pool_width),
                    att_c.reshape(n_tok, mem_width), p["w_in"], gate_col0, p["b_gate"],
                    p["w_a_out"], p["w_b_out"], p["w_c_out"])
    return _wo(x1, merged, p["w_o"], p["g_mix_post"])


_MATMUL_WEIGHTS = ("w1_gate", "w1_up", "w1_down", "w_in", "w_pool", "w_mem_kv",
                   "w_a_out", "w_b_out", "w_c_out", "w_o", "w2_gate", "w2_up", "w2_down")
_ROW_VECTORS = ("g_ffn1_pre", "g_ffn1_post", "g_mix_pre", "pool_scale", "g_mem", "g_mix_post",
                "g_ffn2_pre", "g_ffn2_post", "g_final")


def kernel(x_prompt, x_sample, mem_prompt, mem_sample, g_ffn1_pre, w1_gate, w1_up, w1_down, g_ffn1_post, g_mix_pre, w_in, rpb, w_pool, pool_scale, g_mem, w_mem_kv, w_a_out, w_b_out, w_c_out, b_gate, w_o, g_mix_post, g_ffn2_pre, w2_gate, w2_up, w2_down, g_ffn2_post, g_final):
    stacked = dict(g_ffn1_pre=g_ffn1_pre, w1_gate=w1_gate, w1_up=w1_up, w1_down=w1_down,
                   g_ffn1_post=g_ffn1_post, g_mix_pre=g_mix_pre, w_in=w_in, rpb=rpb, w_pool=w_pool,
                   pool_scale=pool_scale, g_mem=g_mem, w_mem_kv=w_mem_kv, w_a_out=w_a_out,
                   w_b_out=w_b_out, w_c_out=w_c_out, b_gate=b_gate, w_o=w_o, g_mix_post=g_mix_post,
                   g_ffn2_pre=g_ffn2_pre, w2_gate=w2_gate, w2_up=w2_up, w2_down=w2_down,
                   g_ffn2_post=g_ffn2_post, g_final=g_final)
    y_prompt, y_sample = x_prompt, x_sample
    for layer in range(g_final.shape[0]):
        p = {name: a[layer] for name, a in stacked.items()}
        for name in _MATMUL_WEIGHTS:
            p[name] = p[name].astype(BF16)
        for name in _ROW_VECTORS:
            p[name] = p[name].reshape(1, -1)
        bias_tables = _natten_bias_tables(p["rpb"])
        (bp, sp, d), (bs, ss, _) = y_prompt.shape, y_sample.shape
        x1p, x1s, hp, hs = _ffn_pair(
            y_prompt.reshape(bp * sp, d), y_sample.reshape(bs * ss, d),
            p["g_ffn1_pre"], p["w1_gate"], p["w1_up"], p["w1_down"], p["g_ffn1_post"], p["g_final"],
            p["g_mix_pre"], final_norm=False, emit_h=True)
        x2p = _mixer(x1p, hp, mem_prompt, p, bias_tables, b=bp, seq=sp)
        x2s = _mixer(x1s, hs, mem_sample, p, bias_tables, b=bs, seq=ss)
        yp, ys = _ffn_pair(
            x2p, x2s, p["g_ffn2_pre"], p["w2_gate"], p["w2_up"], p["w2_down"], p["g_ffn2_post"], p["g_final"],
            p["g_final"], final_norm=True, emit_h=False)
        y_prompt, y_sample = yp.reshape(bp, sp, d), ys.reshape(bs, ss, d)
    return (y_prompt, y_sample)
```

```python
import functools

import numpy as np
import jax
import jax.numpy as jnp
from jax import lax
from jax.experimental import pallas as pl
from jax.experimental.pallas import tpu as pltpu

GRID_W = 64
POOL_WINDOWS = (2, 4, 8, 16)
MEM_HEADS = 4
RMS_EPS = 1e-6
MASK_VALUE = -1e30
LOG2_E = 1.4426950408889634

FFN_CHUNK = 256
NAT_Q_ROWS = 8
NAT_K_ROWS = 16
NAT_K_CHUNK_ROWS = 4
NAT_SUB_ROWS = 4

MXU_COLS = 256
V7X_VMEM_BYTES = 64 * 1024 * 1024
VMEM_LIMIT = V7X_VMEM_BYTES - 6 * 1024 * 1024

F32 = jnp.float32
BF16 = jnp.bfloat16


def _rms(x, g):
    ms = jnp.mean(x * x, axis=-1, keepdims=True)
    return (x * lax.rsqrt(ms + RMS_EPS)) * g


def _silu_mul(g, u):
    return (g * (1.0 / (1.0 + jnp.exp(-g)))) * u


def _for_row_chunks(n_rows, chunk, fn):
    def body(c, carry):
        fn(pl.ds(pl.multiple_of(c * chunk, chunk), chunk))
        return carry
    lax.fori_loop(0, n_rows // chunk, body, 0)


def _params(sem):
    return pltpu.CompilerParams(dimension_semantics=sem, vmem_limit_bytes=VMEM_LIMIT)


def _ffn_pair_body(*refs, tm, tf, rc, edge_rows, n_a, n_tiles, n_ff, final_norm, emit_h):
    (xa_hbm, xb_hbm, gpre_ref, wg_hbm, wu_hbm, wd_hbm, gpost_ref, gfin_ref, gnext_ref), refs = refs[:9], refs[9:]
    if emit_h:
        (oa_hbm, ob_hbm, na_hbm, nb_hbm), refs = refs[:4], refs[4:]
    else:
        (oa_hbm, ob_hbm), refs = refs[:2], refs[2:]
        na_hbm = nb_hbm = None
    (hbuf, accbuf, gbuf, ubuf, abuf, wgbuf, wubuf, wdbuf, xpbuf, xebuf, obuf, nbuf, xedge, oedge, nedge,
     wsem, xpsem, xesem, osem, nsem, edge_in_sem, edge_out_sem, edge_next_sem) = refs

    n_chunks = tm // rc
    n_steps = n_tiles * n_ff
    n_edge = tm // edge_rows

    def step_after(tj):
        t, j = tj
        wrap = j == n_ff - 1
        return jnp.where(wrap, t + 1, t), jnp.where(wrap, 0, j + 1)

    def step_before(tj):
        t, j = tj
        wrap = j == 0
        return jnp.where(wrap, t - 1, t), jnp.where(wrap, n_ff - 1, j - 1)

    def coords(s):
        return jnp.int32(s // n_ff), jnp.int32(s % n_ff)

    def up_weight_copies(j_up, slot):
        cols = pl.ds(pl.multiple_of(j_up * tf, tf), tf)
        return [pltpu.make_async_copy(wg_hbm.at[:, cols], wgbuf.at[slot], wsem.at[slot, 0]),
                pltpu.make_async_copy(wu_hbm.at[:, cols], wubuf.at[slot], wsem.at[slot, 1])]

    def weight_copies(j_down, j_up, slot):
        rows = pl.ds(pl.multiple_of(j_down * tf, tf), tf)
        return up_weight_copies(j_up, slot) + [
            pltpu.make_async_copy(wd_hbm.at[rows, :], wdbuf.at[slot], wsem.at[slot, 2])]

    def start_rows_in(row0, n_rows, dst, sem):
        row0 = jnp.asarray(row0, jnp.int32)

        @pl.when(row0 < n_a)
        def _():
            pltpu.make_async_copy(xa_hbm.at[pl.ds(pl.multiple_of(row0, n_rows), n_rows), :], dst, sem).start()

        @pl.when(row0 >= n_a)
        def _():
            pltpu.make_async_copy(xb_hbm.at[pl.ds(pl.multiple_of(row0 - n_a, n_rows), n_rows), :], dst, sem).start()

    def wait_rows_in(n_rows, dst, sem):
        pltpu.make_async_copy(xa_hbm.at[pl.ds(0, n_rows), :], dst, sem).wait()

    def start_rows_out(row0, n_rows, src, sem, dst_a, dst_b):
        row0 = jnp.asarray(row0, jnp.int32)

        @pl.when(row0 < n_a)
        def _():
            pltpu.make_async_copy(src, dst_a.at[pl.ds(pl.multiple_of(row0, n_rows), n_rows), :], sem).start()

        @pl.when(row0 >= n_a)
        def _():
            pltpu.make_async_copy(src, dst_b.at[pl.ds(pl.multiple_of(row0 - n_a, n_rows), n_rows), :], sem).start()

    def wait_rows_out(n_rows, src, sem, dst_a):
        pltpu.make_async_copy(src, dst_a.at[pl.ds(0, n_rows), :], sem).wait()

    def start_results_out(row0, n_rows, out_src, out_sem, next_src, next_sem):
        start_rows_out(row0, n_rows, out_src, out_sem, oa_hbm, ob_hbm)
        if emit_h:
            start_rows_out(row0, n_rows, next_src, next_sem, na_hbm, nb_hbm)

    def wait_results_out(n_rows, out_src, out_sem, next_src, next_sem):
        wait_rows_out(n_rows, out_src, out_sem, oa_hbm)
        if emit_h:
            wait_rows_out(n_rows, next_src, next_sem, na_hbm)

    def stage_inputs(tj):
        t, j = tj
        j_up = jnp.where(j == n_ff - 1, 0, j + 1)
        chunk = jnp.minimum(j, n_chunks - 1)
        row_pre = jnp.minimum(t + 1, n_tiles - 1) * tm + chunk * rc
        row_epi = jnp.clip(t - 1, 0, n_tiles - 1) * tm + chunk * rc
        return j, j_up, row_pre, row_epi, chunk

    def start_inputs(tj, slot):
        j_down, j_up, row_pre, row_epi, _ = stage_inputs(tj)
        for c in weight_copies(j_down, j_up, slot):
            c.start()
        start_rows_in(row_pre, rc, xpbuf.at[slot], xpsem.at[slot])
        start_rows_in(row_epi, rc, xebuf.at[slot], xesem.at[slot])

    def wait_inputs(slot):
        for c in weight_copies(0, 0, slot):
            c.wait()
        wait_rows_in(rc, xpbuf.at[slot], xpsem.at[slot])
        wait_rows_in(rc, xebuf.at[slot], xesem.at[slot])

    def finishes_rows(tj):
        t, j = tj
        return jnp.logical_and(t >= 1, j < n_chunks)

    def finish_rows(acc_rows, x_rows, out_ref, next_ref):
        out = x_rows + 0.5 * _rms(acc_rows, gpost_ref[...])
        if final_norm:
            out = _rms(out, gfin_ref[...])
        out_ref[...] = out
        if emit_h:
            next_ref[...] = _rms(out, gnext_ref[...]).astype(BF16)

    def up_chunk(h_sel, slot):
        h = hbuf[h_sel]
        gbuf[...] = jnp.dot(h, wgbuf[slot], preferred_element_type=F32)
        ubuf[...] = jnp.dot(h, wubuf[slot], preferred_element_type=F32)

    start_rows_in(0, edge_rows, xedge.at[0], edge_in_sem.at[0])
    for c in range(n_edge):
        wait_rows_in(edge_rows, xedge.at[c % 2], edge_in_sem.at[c % 2])
        if c + 1 < n_edge:
            start_rows_in((c + 1) * edge_rows, edge_rows, xedge.at[(c + 1) % 2], edge_in_sem.at[(c + 1) % 2])
        hbuf[0, pl.ds(c * edge_rows, edge_rows), :] = _rms(xedge[c % 2], gpre_ref[...]).astype(BF16)
        accbuf[1, pl.ds(c * edge_rows, edge_rows), :] = jnp.zeros((edge_rows, accbuf.shape[2]), F32)
    for c in up_weight_copies(0, 1):
        c.start()
    for c in up_weight_copies(0, 1):
        c.wait()
    up_chunk(0, 1)
    start_inputs(coords(0), 0)

    def step(_, carry):
        t, j, slot = carry
        tj = (t, j)
        before = step_before(tj)
        cur = t % 2
        other = 1 - cur
        start_inputs(step_after(tj), 1 - slot)

        @pl.when(finishes_rows(step_before(before)))
        def _():
            wait_results_out(rc, obuf.at[slot], osem.at[slot], nbuf.at[slot], nsem.at[slot])

        @pl.when(finishes_rows(before))
        def _():
            start_results_out(stage_inputs(before)[3], rc, obuf.at[1 - slot], osem.at[1 - slot],
                              nbuf.at[1 - slot], nsem.at[1 - slot])

        @pl.when(j == 0)
        def _():
            def zero(zrows):
                accbuf[cur, zrows, :] = jnp.zeros((edge_rows, accbuf.shape[2]), F32)
            _for_row_chunks(tm, edge_rows, zero)

        wait_inputs(slot)
        abuf[...] = _silu_mul(gbuf[...], ubuf[...]).astype(BF16)
        rows = pl.ds(pl.multiple_of(jnp.minimum(j, n_chunks - 1) * rc, rc), rc)
        hbuf[other, rows, :] = _rms(xpbuf[slot], gpre_ref[...]).astype(BF16)
        finish_rows(accbuf[other, rows, :], xebuf[slot], obuf.at[slot], nbuf.at[slot])
        accbuf[cur] += jnp.dot(abuf[...], wdbuf[slot], preferred_element_type=F32)
        up_chunk(jnp.where(j == n_ff - 1, other, cur), slot)
        return step_after(tj) + (1 - slot,)

    lax.fori_loop(0, n_steps, step, coords(0) + (jnp.int32(0),))

    end_slot = n_steps % 2
    wait_inputs(end_slot)

    @pl.when(finishes_rows(coords(n_steps - 2)))
    def _():
        wait_results_out(rc, obuf.at[end_slot], osem.at[end_slot], nbuf.at[end_slot], nsem.at[end_slot])

    @pl.when(finishes_rows(coords(n_steps - 1)))
    def _():
        last_bufs = (obuf.at[1 - end_slot], osem.at[1 - end_slot], nbuf.at[1 - end_slot], nsem.at[1 - end_slot])
        start_results_out(stage_inputs(coords(n_steps - 1))[3], rc, *last_bufs)
        wait_results_out(rc, *last_bufs)

    last_row0 = (n_tiles - 1) * tm

    def edge_bufs(c):
        return oedge.at[c % 2], edge_out_sem.at[c % 2], nedge.at[c % 2], edge_next_sem.at[c % 2]

    start_rows_in(last_row0, edge_rows, xedge.at[0], edge_in_sem.at[0])
    for c in range(n_edge):
        wait_rows_in(edge_rows, xedge.at[c % 2], edge_in_sem.at[c % 2])
        if c + 1 < n_edge:
            start_rows_in(last_row0 + (c + 1) * edge_rows, edge_rows, xedge.at[(c + 1) % 2],
                          edge_in_sem.at[(c + 1) % 2])
        if c >= 2:
            wait_results_out(edge_rows, *edge_bufs(c))
        finish_rows(accbuf[(n_tiles - 1) % 2, pl.ds(c * edge_rows, edge_rows), :], xedge[c % 2],
                    oedge.at[c % 2], nedge.at[c % 2])
        start_results_out(last_row0 + c * edge_rows, edge_rows, *edge_bufs(c))
    for c in range(max(n_edge - 2, 0), n_edge):
        wait_results_out(edge_rows, *edge_bufs(c))


def _ffn_pair(xa, xb, g_pre, wg, wu, wd, g_post, g_fin, g_next, *, final_norm, emit_h,
              tm=768, tf=FFN_CHUNK, rc=32, edge_rows=32):
    (n_a, d), n_b = xa.shape, xb.shape[0]
    d_ff = wg.shape[1]
    n_tok = n_a + n_b
    tm = min(tm, n_tok // 2)
    tf = min(tf, d_ff)
    edge_rows = min(edge_rows, tm)
    n_tiles, n_ff = n_tok // tm, d_ff // tf
    assert n_tok % tm == 0 and d_ff % tf == 0 and tm % rc == 0 and tm % edge_rows == 0
    assert n_a % edge_rows == 0 and n_a % rc == 0 and tm // rc <= n_ff and tm // edge_rows >= 2 and n_tiles >= 2
    hbm = pl.BlockSpec(memory_space=pl.ANY)
    vec = pl.BlockSpec(memory_space=pltpu.VMEM)
    out_shape = [jax.ShapeDtypeStruct((n_a, d), F32), jax.ShapeDtypeStruct((n_b, d), F32)]
    if emit_h:
        out_shape += [jax.ShapeDtypeStruct((n_a, d), BF16), jax.ShapeDtypeStruct((n_b, d), BF16)]
    next_rows = (rc, edge_rows) if emit_h else (8, 8)
    return pl.pallas_call(
        functools.partial(_ffn_pair_body, tm=tm, tf=tf, rc=rc, edge_rows=edge_rows, n_a=n_a, n_tiles=n_tiles,
                          n_ff=n_ff, final_norm=final_norm, emit_h=emit_h),
        in_specs=[hbm, hbm, vec, hbm, hbm, hbm, vec, vec, vec],
        out_specs=[hbm] * len(out_shape),
        out_shape=out_shape,
        scratch_shapes=[
            pltpu.VMEM((2, tm, d), BF16),
            pltpu.VMEM((2, tm, d), F32),
            pltpu.VMEM((tm, tf), F32),
            pltpu.VMEM((tm, tf), F32),
            pltpu.VMEM((tm, tf), BF16),
            pltpu.VMEM((2, d, tf), BF16),
            pltpu.VMEM((2, d, tf), BF16),
            pltpu.VMEM((2, tf, d), BF16),
            pltpu.VMEM((2, rc, d), F32),
            pltpu.VMEM((2, rc, d), F32),
            pltpu.VMEM((2, rc, d), F32),
            pltpu.VMEM((2, next_rows[0], d), BF16),
            pltpu.VMEM((2, edge_rows, d), F32),
            pltpu.VMEM((2, edge_rows, d), F32),
            pltpu.VMEM((2, next_rows[1], d), BF16),
            pltpu.SemaphoreType.DMA((2, 3)),
            pltpu.SemaphoreType.DMA((2,)),
            pltpu.SemaphoreType.DMA((2,)),
            pltpu.SemaphoreType.DMA((2,)),
            pltpu.SemaphoreType.DMA((2,)),
            pltpu.SemaphoreType.DMA((2,)),
            pltpu.SemaphoreType.DMA((2,)),
            pltpu.SemaphoreType.DMA((2,)),
        ],
        compiler_params=pltpu.CompilerParams(vmem_limit_bytes=VMEM_LIMIT),
        name="ffn_pair",
    )(xa, xb, g_pre, wg, wu, wd, g_post, g_fin, g_next)


def _inproj_body(h_ref, w_ref, qkv_ref, u_ref, qc_ref, *, n_qkv, n_u):
    j = pl.program_id(1)

    def project_into(o_ref):
        for c in range(0, w_ref.shape[1], MXU_COLS):
            z = jnp.dot(h_ref[...], w_ref[:, c:c + MXU_COLS], preferred_element_type=F32)
            o_ref[:, c:c + MXU_COLS] = z.astype(o_ref.dtype)

    @pl.when(j < n_qkv)
    def _():
        project_into(qkv_ref)

    @pl.when(jnp.logical_and(j >= n_qkv, j < n_qkv + n_u))
    def _():
        project_into(u_ref)

    @pl.when(j >= n_qkv + n_u)
    def _():
        project_into(qc_ref)


def _inproj(h, w_in, *, na_width, pool_width, mem_width, tm=1024, tn=1024):
    n_tok, d = h.shape
    tm = min(tm, n_tok)
    tn = min(tn, pool_width, mem_width)
    assert n_tok % tm == 0 and na_width % tn == 0 and pool_width % tn == 0 and mem_width % tn == 0
    n_qkv = 3 * na_width // tn
    n_u = pool_width // tn
    n_qc = mem_width // tn
    return pl.pallas_call(
        functools.partial(_inproj_body, n_qkv=n_qkv, n_u=n_u),
        grid=(n_tok // tm, n_qkv + n_u + n_qc),
        in_specs=[
            pl.BlockSpec((tm, d), lambda i, j: (i, 0)),
            pl.BlockSpec((d, tn), lambda i, j: (0, j)),
        ],
        out_specs=[
            pl.BlockSpec((tm, tn), lambda i, j: (i, jnp.minimum(j, n_qkv - 1))),
            pl.BlockSpec((tm, tn), lambda i, j: (i, jnp.clip(j - n_qkv, 0, n_u - 1))),
            pl.BlockSpec((tm, tn), lambda i, j: (i, jnp.clip(j - n_qkv - n_u, 0, n_qc - 1))),
        ],
        out_shape=[
            jax.ShapeDtypeStruct((n_tok, 3 * na_width), BF16),
            jax.ShapeDtypeStruct((n_tok, pool_width), F32),
            jax.ShapeDtypeStruct((n_tok, mem_width), BF16),
        ],
        compiler_params=_params(("parallel", "arbitrary")),
        name="inproj",
    )(h, w_in)


def _natten_bias_tables(rpb):
    n_heads, n_dr, n_dc = rpb.shape
    kh, kw = (n_dr + 1) // 2, (n_dc + 1) // 2
    cols = np.arange(GRID_W)
    col_start = np.clip(cols - kw // 2, 0, GRID_W - kw)
    col_valid = (cols[None, :] >= col_start[:, None]) & (cols[None, :] < col_start[:, None] + kw)
    period = 2 * GRID_W
    lpad = GRID_W - kw
    padded = jnp.pad(rpb.astype(F32) * LOG2_E, ((0, 0), (0, 0), (lpad, period - n_dc - lpad)))
    skew = jnp.broadcast_to(padded[:, :, None, :], (n_heads, n_dr, GRID_W, period))
    skew = skew.reshape(n_heads, n_dr, GRID_W * period)[:, :, :GRID_W * (period - 1)]
    by_col = skew.reshape(n_heads, n_dr, GRID_W, period - 1)[:, :, :, GRID_W - 1:]
    by_col = jnp.where(jnp.asarray(col_valid)[None, None], by_col, MASK_VALUE).transpose(0, 1, 3, 2)
    rpad = NAT_K_ROWS - kh
    wide = jnp.concatenate([by_col, by_col], axis=-1)
    wide = jnp.pad(wide, ((0, 0), (rpad, 2 * NAT_K_ROWS - n_dr - rpad), (0, 0), (0, 0)), constant_values=MASK_VALUE)
    n_sub = NAT_Q_ROWS // NAT_SUB_ROWS
    block = (n_sub, _natten_window_rows(kh) * GRID_W, NAT_SUB_ROWS * GRID_W)
    return pl.pallas_call(
        functools.partial(_natten_bias_body, kh=kh),
        grid=(n_heads,),
        in_specs=[pl.BlockSpec((1,) + wide.shape[1:], lambda h: (h, 0, 0, 0))],
        out_specs=pl.BlockSpec((3, 1) + block, lambda h: (0, h, 0, 0, 0)),
        out_shape=jax.ShapeDtypeStruct((3, n_heads) + block, F32),
        compiler_params=_params(("parallel",)),
        name="natten_bias",
    )(wide)


def _natten_row_start(r, kh):
    return min(max(r - kh // 2, 0), NAT_K_ROWS - kh)


def _natten_window_rows(kh):
    return NAT_SUB_ROWS - 1 + kh


def _natten_window_start(t, sub, kh):
    r_first = (NAT_K_ROWS - NAT_Q_ROWS) // 2 * t + sub * NAT_SUB_ROWS
    return min(_natten_row_start(r_first, kh), NAT_K_ROWS - _natten_window_rows(kh))


def _natten_bias_body(wide_ref, o_ref, *, kh):
    low_half = lax.broadcasted_iota(jnp.int32, (GRID_W, 2 * GRID_W), 1) < GRID_W
    masked = jnp.full((GRID_W, 2 * GRID_W), MASK_VALUE, F32)
    for t in range(3):
        for sub in range(NAT_Q_ROWS // NAT_SUB_ROWS):
            first_key_row = _natten_window_start(t, sub, kh)
            for kl in range(_natten_window_rows(kh)):
                rj = first_key_row + kl
                for pair in range(NAT_SUB_ROWS // 2):
                    halves = []
                    for ri in (sub * NAT_SUB_ROWS + 2 * pair, sub * NAT_SUB_ROWS + 2 * pair + 1):
                        r = (NAT_K_ROWS - NAT_Q_ROWS) // 2 * t + ri
                        valid = _natten_row_start(r, kh) <= rj < _natten_row_start(r, kh) + kh
                        halves.append(wide_ref[0, rj - r + NAT_K_ROWS - 1] if valid else masked)
                    o_ref[t, 0, sub, pl.ds(kl * GRID_W, GRID_W), pl.ds(pair * 2 * GRID_W, 2 * GRID_W)] = (
                        jnp.where(low_half, halves[0], halves[1]))


def _natten_body(q_ref, k0_ref, k1_ref, k2_ref, k3_ref, v0_ref, v1_ref, v2_ref, v3_ref, bias_ref, o_ref,
                 kbuf, vbuf, *, log2_scale, head_dim, kh):
    r = pl.program_id(2)
    table = jnp.where(r == 0, 0, jnp.where(r == pl.num_programs(2) - 1, 2, 1))
    chunk = k0_ref.shape[1]
    for c, (k_ref, v_ref) in enumerate(((k0_ref, v0_ref), (k1_ref, v1_ref), (k2_ref, v2_ref), (k3_ref, v3_ref))):
        kbuf[pl.ds(c * chunk, chunk), :] = k_ref[0]
        vbuf[pl.ds(c * chunk, chunk), :] = v_ref[0]
    n_win = _natten_window_rows(kh) * GRID_W
    n_q = NAT_SUB_ROWS * GRID_W
    blocks = []
    for sub in range(NAT_Q_ROWS // NAT_SUB_ROWS):
        starts = [_natten_window_start(t, sub, kh) * GRID_W for t in range(3)]
        start = jnp.where(table == 0, starts[0], jnp.where(table == 1, starts[1], starts[2]))
        keys = pl.ds(pl.multiple_of(start, GRID_W), n_win)
        for h in range(q_ref.shape[2] // head_dim):
            blocks.append((sub, h, keys, pl.ds(sub * n_q, n_q), slice(h * head_dim, (h + 1) * head_dim)))
    scores = [lax.dot_general(kbuf[keys, cols], q_ref[0, queries, cols], (((1,), (1,)), ((), ())),
                              preferred_element_type=F32)
              for _, _, keys, queries, cols in blocks]
    probs = []
    for (sub, h, _, _, _), s in zip(blocks, scores):
        s = s * log2_scale + bias_ref[0, h, sub]
        p = jnp.exp2(s - jnp.max(s, axis=0, keepdims=True))
        probs.append((p.astype(BF16), jnp.sum(p, axis=0, keepdims=True)))
    for (_, _, keys, queries, cols), (p, l) in zip(blocks, probs):
        o = lax.dot_general(vbuf[keys, cols], p, (((0,), (0,)), ((), ())), preferred_element_type=F32)
        o_ref[0, queries, cols] = (o / l).T.astype(o_ref.dtype)


def _natten(qkv, bias, *, n_heads, head_dim, kh, heads_per_step=8):
    b, seq, _ = qkv.shape
    tq = NAT_Q_ROWS * GRID_W
    tk = NAT_K_CHUNK_ROWS * GRID_W
    n_kc = NAT_K_ROWS // NAT_K_CHUNK_ROWS
    hp = min(heads_per_step, n_heads)
    assert n_kc == 4 and seq % (NAT_K_ROWS * GRID_W) == 0 and n_heads % hp == 0
    n_hg = n_heads // hp
    n_qb = seq // tq
    n_kb = seq // tk
    lead = (NAT_K_ROWS - NAT_Q_ROWS) // 2 // NAT_K_CHUNK_ROWS

    def kv_spec(which, c):
        def index(bi, h, r):
            start = jnp.clip(r * (tq // tk) - lead, 0, n_kb - n_kc)
            return (bi, start + c, which * n_hg + h)
        return pl.BlockSpec((1, tk, hp * head_dim), index)

    def bias_index(bi, h, r):
        t = jnp.where(r == 0, 0, jnp.where(r == n_qb - 1, 2, 1))
        return (t, h, 0, 0, 0)

    return pl.pallas_call(
        functools.partial(_natten_body, log2_scale=head_dim ** -0.5 * LOG2_E, head_dim=head_dim, kh=kh),
        grid=(b, n_hg, n_qb),
        in_specs=[pl.BlockSpec((1, tq, hp * head_dim), lambda bi, h, r: (bi, r, h))]
        + [kv_spec(1, c) for c in range(n_kc)]
        + [kv_spec(2, c) for c in range(n_kc)]
        + [pl.BlockSpec((1, hp) + bias.shape[2:], bias_index)],
        out_specs=pl.BlockSpec((1, tq, hp * head_dim), lambda bi, h, r: (bi, r, h)),
        out_shape=jax.ShapeDtypeStruct((b, seq, n_heads * head_dim), BF16),
        scratch_shapes=[pltpu.VMEM((NAT_K_ROWS * GRID_W, hp * head_dim), BF16),
                        pltpu.VMEM((NAT_K_ROWS * GRID_W, hp * head_dim), BF16)],
        compiler_params=_params(("parallel", "parallel", "arbitrary")),
        name="natten",
    )(qkv, *([qkv] * (2 * n_kc)), bias)


def _pool_body(u_ref, prev_ref, next_ref, w_ref, scale_ref, o_ref, ext_ref, *, seq, halo, group_dim):
    t = pl.program_id(1)
    tt = u_ref.shape[1]
    u = u_ref[0]
    ext_ref[pl.ds(halo, tt), :] = u
    ext_ref[pl.ds(0, halo), :] = jnp.where(t > 0, prev_ref[0], 0.0)
    ext_ref[pl.ds(halo + tt, halo), :] = jnp.where(t < pl.num_programs(1) - 1, next_ref[0], 0.0)
    pos = t * tt + lax.broadcasted_iota(jnp.int32, (tt, 1), 0)
    for g, w in enumerate(POOL_WINDOWS):
        cols = pl.ds(g * group_dim, group_dim)
        total = ext_ref[pl.ds(halo - w // 2, tt), cols]
        for d in range(-w // 2 + 1, w // 2):
            total = total + ext_ref[pl.ds(halo + d, tt), cols]
        count = jnp.minimum(pos + w // 2, seq) - jnp.maximum(pos - w // 2, 0)
        pooled = total / count.astype(F32) - u[:, g * group_dim:(g + 1) * group_dim]
        mixed = jnp.dot(pooled.astype(BF16), w_ref[g], preferred_element_type=F32)
        o_ref[0, :, cols] = (mixed * scale_ref[:, cols]).astype(o_ref.dtype)


def _pool(u, w_pool, pool_scale, *, tt=512):
    b, seq, width = u.shape
    n_groups, group_dim, _ = w_pool.shape
    assert n_groups == len(POOL_WINDOWS)
    halo = 8
    assert max(POOL_WINDOWS) // 2 <= halo
    tt = min(tt, seq)
    assert seq % tt == 0 and tt % halo == 0
    hb = tt // halo
    return pl.pallas_call(
        functools.partial(_pool_body, seq=seq, halo=halo, group_dim=group_dim),
        grid=(b, seq // tt),
        in_specs=[
            pl.BlockSpec((1, tt, width), lambda bi, t: (bi, t, 0)),
            pl.BlockSpec((1, halo, width), lambda bi, t: (bi, jnp.maximum(t * hb - 1, 0), 0)),
            pl.BlockSpec((1, halo, width), lambda bi, t: (bi, jnp.minimum((t + 1) * hb, seq // halo - 1), 0)),
            pl.BlockSpec((n_groups, group_dim, group_dim), lambda bi, t: (0, 0, 0)),
            pl.BlockSpec((1, width), lambda bi, t: (0, 0)),
        ],
        out_specs=pl.BlockSpec((1, tt, width), lambda bi, t: (bi, t, 0)),
        out_shape=jax.ShapeDtypeStruct((b, seq, width), BF16),
        scratch_shapes=[pltpu.VMEM((tt + 2 * halo, width), F32)],
        compiler_params=_params(("parallel", "arbitrary")),
        name="pool",
    )(u, u, u, w_pool, pool_scale)


def _memkv_body(m_ref, g_ref, w_ref, o_ref):
    h = _rms(m_ref[...], g_ref[...]).astype(BF16)
    o_ref[...] = jnp.dot(h, w_ref[...], preferred_element_type=F32).astype(o_ref.dtype)


def _memkv(mem, g_mem, w_kv, *, tr=256, tn=512):
    n_rows, d = mem.shape
    width = w_kv.shape[1]
    tr = min(tr, n_rows)
    tn = min(tn, width)
    assert n_rows % tr == 0 and width % tn == 0
    return pl.pallas_call(
        _memkv_body,
        grid=(n_rows // tr, width // tn),
        in_specs=[
            pl.BlockSpec((tr, d), lambda i, j: (i, 0)),
            pl.BlockSpec((1, d), lambda i, j: (0, 0)),
            pl.BlockSpec((d, tn), lambda i, j: (0, j)),
        ],
        out_specs=pl.BlockSpec((tr, tn), lambda i, j: (i, j)),
        out_shape=jax.ShapeDtypeStruct((n_rows, width), BF16),
        compiler_params=_params(("parallel", "arbitrary")),
        name="memkv",
    )(mem, g_mem, w_kv)


def _memattn_body(q_ref, kv_ref, o_ref, *, n_heads, head_dim):
    scale = head_dim ** -0.5
    width = n_heads * head_dim
    for h in range(n_heads):
        cols = slice(h * head_dim, (h + 1) * head_dim)
        q = q_ref[0, :, cols]
        k = kv_ref[0, :, cols]
        v = kv_ref[0, :, width + h * head_dim: width + (h + 1) * head_dim]
        s = lax.dot_general(q, k, (((1,), (1,)), ((), ())), preferred_element_type=F32) * scale
        m = jnp.max(s, axis=-1, keepdims=True)
        p = jnp.exp(s - m)
        l = jnp.sum(p, axis=-1, keepdims=True)
        o = jnp.dot(p.astype(BF16), v, preferred_element_type=F32)
        o_ref[0, :, cols] = (o / l).astype(o_ref.dtype)


def _memattn(q, kv, *, n_heads, tq=512):
    b, seq, width = q.shape
    n_mem = kv.shape[1]
    tq = min(tq, seq)
    assert seq % tq == 0
    return pl.pallas_call(
        functools.partial(_memattn_body, n_heads=n_heads, head_dim=width // n_heads),
        grid=(b, seq // tq),
        in_specs=[
            pl.BlockSpec((1, tq, width), lambda bi, t: (bi, t, 0)),
            pl.BlockSpec((1, n_mem, 2 * width), lambda bi, t: (bi, 0, 0)),
        ],
        out_specs=pl.BlockSpec((1, tq, width), lambda bi, t: (bi, t, 0)),
        out_shape=jax.ShapeDtypeStruct((b, seq, width), BF16),
        compiler_params=_params(("parallel", "arbitrary")),
        name="memattn",
    )(q, kv)


def _merge_body(h_ref, a_ref, b_ref, c_ref, wga_ref, wgb_ref, wgc_ref, bg_ref, wa_ref, wb_ref, wc_ref, o_ref):
    h = h_ref[...]
    merged = None
    for i, (x_ref, wg_ref, w_ref) in enumerate(
            ((a_ref, wga_ref, wa_ref), (b_ref, wgb_ref, wb_ref), (c_ref, wgc_ref, wc_ref))):
        logits = jnp.dot(h, wg_ref[...], preferred_element_type=F32) + bg_ref[i:i + 1, :]
        gate = 1.0 / (1.0 + jnp.exp(-logits))
        y = jnp.dot(x_ref[...], w_ref[...], preferred_element_type=F32)
        merged = gate * y if merged is None else merged + gate * y
    o_ref[...] = merged.astype(o_ref.dtype)


def _merge(h, att_a, mix_b, att_c, w_in, gate_col0, b_gate, w_a, w_b, w_c, *, tm=512, tn=512):
    n_tok, d = h.shape
    tm = min(tm, n_tok)
    tn = min(tn, d)
    assert n_tok % tm == 0 and d % tn == 0 and gate_col0 % tn == 0
    nj = d // tn
    g0 = gate_col0 // tn

    def act(x):
        return pl.BlockSpec((tm, x.shape[1]), lambda i, j: (i, 0))

    def gate_w(branch):
        return pl.BlockSpec((d, tn), lambda i, j: (0, g0 + branch * nj + j))

    def out_w(w):
        return pl.BlockSpec((w.shape[0], tn), lambda i, j: (0, j))

    return pl.pallas_call(
        _merge_body,
        grid=(n_tok // tm, nj),
        in_specs=[act(h), act(att_a), act(mix_b), act(att_c),
                  gate_w(0), gate_w(1), gate_w(2),
                  pl.BlockSpec((b_gate.shape[0], tn), lambda i, j: (0, j)),
                  out_w(w_a), out_w(w_b), out_w(w_c)],
        out_specs=pl.BlockSpec((tm, tn), lambda i, j: (i, j)),
        out_shape=jax.ShapeDtypeStruct((n_tok, d), BF16),
        compiler_params=_params(("parallel", "arbitrary")),
        name="merge",
    )(h, att_a, mix_b, att_c, w_in, w_in, w_in, b_gate, w_a, w_b, w_c)


def _wo_body(x_ref, m_ref, w_ref, g_ref, o_ref, *, row_chunk):
    j = pl.program_id(1)
    tm = x_ref.shape[0]

    @pl.when(j == 0)
    def _():
        def prologue(rows):
            o_ref[rows, :] = jnp.zeros((row_chunk, o_ref.shape[1]), F32)
        _for_row_chunks(tm, row_chunk, prologue)

    o_ref[...] += jnp.dot(m_ref[...], w_ref[...], preferred_element_type=F32)

    @pl.when(j == pl.num_programs(1) - 1)
    def _():
        def epilogue(rows):
            o_ref[rows, :] = x_ref[rows, :] + _rms(o_ref[rows, :], g_ref[...])
        _for_row_chunks(tm, row_chunk, epilogue)


def _wo(x, merged, w_o, g_post, *, tm=512, tk=1024):
    n_tok, d = x.shape
    tm = min(tm, n_tok)
    tk = min(tk, d)
    assert n_tok % tm == 0 and d % tk == 0
    return pl.pallas_call(
        functools.partial(_wo_body, row_chunk=min(64, tm)),
        grid=(n_tok // tm, d // tk),
        in_specs=[
            pl.BlockSpec((tm, d), lambda i, j: (i, 0)),
            pl.BlockSpec((tm, tk), lambda i, j: (i, j)),
            pl.BlockSpec((tk, d), lambda i, j: (j, 0)),
            pl.BlockSpec((1, d), lambda i, j: (0, 0)),
        ],
        out_specs=pl.BlockSpec((tm, d), lambda i, j: (i, 0)),
        out_shape=jax.ShapeDtypeStruct((n_tok, d), F32),
        compiler_params=_params(("parallel", "arbitrary")),
        name="wo",
    )(x, merged, w_o, g_post)


def _mixer(x1, h, mem, p, bias_tables, *, b, seq):
    n_tok, d = x1.shape
    na_heads = p["rpb"].shape[0]
    n_groups, group_dim, _ = p["w_pool"].shape
    pool_width = n_groups * group_dim
    mem_width = p["w_mem_kv"].shape[1] // 2
    na_width = p["w_a_out"].shape[0]
    gate_col0 = 3 * na_width + pool_width + mem_width

    qkv, u, qc = _inproj(h, p["w_in"], na_width=na_width, pool_width=pool_width, mem_width=mem_width)
    att_a = _natten(qkv.reshape(b, seq, 3 * na_width), bias_tables,
                    n_heads=na_heads, head_dim=na_width // na_heads, kh=(p["rpb"].shape[1] + 1) // 2)
    mix_b = _pool(u.reshape(b, seq, pool_width), p["w_pool"], p["pool_scale"])
    kv = _memkv(mem.reshape(b * mem.shape[1], d), p["g_mem"], p["w_mem_kv"])
    att_c = _memattn(qc.reshape(b, seq, mem_width), kv.reshape(b, mem.shape[1], 2 * mem_width),
                     n_heads=MEM_HEADS)
    merged = _merge(h, att_a.reshape(n_tok, na_width), mix_b.reshape(n_tok, pool_width),
                    att_c.reshape(n_tok, mem_width), p["w_in"], gate_col0, p["b_gate"],
                    p["w_a_out"], p["w_b_out"], p["w_c_out"])
    return _wo(x1, merged, p["w_o"], p["g_mix_post"])


_MATMUL_WEIGHTS = ("w1_gate", "w1_up", "w1_down", "w_in", "w_pool", "w_mem_kv",
                   "w_a_out", "w_b_out", "w_c_out", "w_o", "w2_gate", "w2_up", "w2_down")
_ROW_VECTORS = ("g_ffn1_pre", "g_ffn1_post", "g_mix_pre", "pool_scale", "g_mem", "g_mix_post",
                "g_ffn2_pre", "g_ffn2_post", "g_final")


def kernel(x_prompt, x_sample, mem_prompt, mem_sample, g_ffn1_pre, w1_gate, w1_up, w1_down, g_ffn1_post, g_mix_pre, w_in, rpb, w_pool, pool_scale, g_mem, w_mem_kv, w_a_out, w_b_out, w_c_out, b_gate, w_o, g_mix_post, g_ffn2_pre, w2_gate, w2_up, w2_down, g_ffn2_post, g_final):
    stacked = dict(g_ffn1_pre=g_ffn1_pre, w1_gate=w1_gate, w1_up=w1_up, w1_down=w1_down,
                   g_ffn1_post=g_ffn1_post, g_mix_pre=g_mix_pre, w_in=w_in, rpb=rpb, w_pool=w_pool,
                   pool_scale=pool_scale, g_mem=g_mem, w_mem_kv=w_mem_kv, w_a_out=w_a_out,
                   w_b_out=w_b_out, w_c_out=w_c_out, b_gate=b_gate, w_o=w_o, g_mix_post=g_mix_post,
                   g_ffn2_pre=g_ffn2_pre, w2_gate=w2_gate, w2_up=w2_up, w2_down=w2_down,
                   g_ffn2_post=g_ffn2_post, g_final=g_final)
    y_prompt, y_sample = x_prompt, x_sample
    for layer in range(g_final.shape[0]):
        p = {name: a[layer] for name, a in stacked.items()}
        for name in _MATMUL_WEIGHTS:
            p[name] = p[name].astype(BF16)
        for name in _ROW_VECTORS:
            p[name] = p[name].reshape(1, -1)
        bias_tables = _natten_bias_tables(p["rpb"])
        (bp, sp, d), (bs, ss, _) = y_prompt.shape, y_sample.shape
        x1p, x1s, hp, hs = _ffn_pair(
            y_prompt.reshape(bp * sp, d), y_sample.reshape(bs * ss, d),
            p["g_ffn1_pre"], p["w1_gate"], p["w1_up"], p["w1_down"], p["g_ffn1_post"], p["g_final"],
            p["g_mix_pre"], final_norm=False, emit_h=True)
        x2p = _mixer(x1p, hp, mem_prompt, p, bias_tables, b=bp, seq=sp)
        x2s = _mixer(x1s, hs, mem_sample, p, bias_tables, b=bs, seq=ss)
        yp, ys = _ffn_pair(
            x2p, x2s, p["g_ffn2_pre"], p["w2_gate"], p["w2_up"], p["w2_down"], p["g_ffn2_post"], p["g_final"],
            p["g_final"], final_norm=True, emit_h=False)
        y_prompt, y_sample = yp.reshape(bp, sp, d), ys.reshape(bs, ss, d)
    return (y_prompt, y_sample)
```

```python
import functools

import numpy as np
import jax
import jax.numpy as jnp
from jax import lax
from jax.experimental import pallas as pl
from jax.experimental.pallas import tpu as pltpu

GRID_W = 64
POOL_WINDOWS = (2, 4, 8, 16)
MEM_HEADS = 4
RMS_EPS = 1e-6
MASK_VALUE = -1e30
LOG2_E = 1.4426950408889634

FFN_CHUNK = 256
NAT_Q_ROWS = 8
NAT_K_ROWS = 16
NAT_K_CHUNK_ROWS = 4
NAT_SUB_ROWS = 4

MXU_COLS = 256
V7X_VMEM_BYTES = 64 * 1024 * 1024
VMEM_LIMIT = V7X_VMEM_BYTES - 6 * 1024 * 1024

F32 = jnp.float32
BF16 = jnp.bfloat16


def _rms(x, g):
    ms = jnp.mean(x * x, axis=-1, keepdims=True)
    return (x * lax.rsqrt(ms + RMS_EPS)) * g


def _silu_mul(g, u):
    return (g * (1.0 / (1.0 + jnp.exp(-g)))) * u


def _for_row_chunks(n_rows, chunk, fn):
    def body(c, carry):
        fn(pl.ds(pl.multiple_of(c * chunk, chunk), chunk))
        return carry
    lax.fori_loop(0, n_rows // chunk, body, 0)


def _params(sem):
    return pltpu.CompilerParams(dimension_semantics=sem, vmem_limit_bytes=VMEM_LIMIT)


def _ffn_pair_body(*refs, tm, tf, rc, edge_rows, n_a, n_tiles, n_ff, final_norm, emit_h):
    (xa_hbm, xb_hbm, gpre_ref, wg_hbm, wu_hbm, wd_hbm, gpost_ref, gfin_ref, gnext_ref), refs = refs[:9], refs[9:]
    if emit_h:
        (oa_hbm, ob_hbm, na_hbm, nb_hbm), refs = refs[:4], refs[4:]
    else:
        (oa_hbm, ob_hbm), refs = refs[:2], refs[2:]
        na_hbm = nb_hbm = None
    (hbuf, accbuf, gbuf, ubuf, abuf, wgbuf, wubuf, wdbuf, xpbuf, xebuf, obuf, nbuf, xedge, oedge, nedge,
     wsem, xpsem, xesem, osem, nsem, edge_in_sem, edge_out_sem, edge_next_sem) = refs

    n_chunks = tm // rc
    n_steps = n_tiles * n_ff
    n_edge = tm // edge_rows

    def step_after(tj):
        t, j = tj
        wrap = j == n_ff - 1
        return jnp.where(wrap, t + 1, t), jnp.where(wrap, 0, j + 1)

    def step_before(tj):
        t, j = tj
        wrap = j == 0
        return jnp.where(wrap, t - 1, t), jnp.where(wrap, n_ff - 1, j - 1)

    def coords(s):
        return jnp.int32(s // n_ff), jnp.int32(s % n_ff)

    def up_weight_copies(j_up, slot):
        cols = pl.ds(pl.multiple_of(j_up * tf, tf), tf)
        return [pltpu.make_async_copy(wg_hbm.at[:, cols], wgbuf.at[slot], wsem.at[slot, 0]),
                pltpu.make_async_copy(wu_hbm.at[:, cols], wubuf.at[slot], wsem.at[slot, 1])]

    def weight_copies(j_down, j_up, slot):
        rows = pl.ds(pl.multiple_of(j_down * tf, tf), tf)
        return up_weight_copies(j_up, slot) + [
            pltpu.make_async_copy(wd_hbm.at[rows, :], wdbuf.at[slot], wsem.at[slot, 2])]

    def start_rows_in(row0, n_rows, dst, sem):
        row0 = jnp.asarray(row0, jnp.int32)

        @pl.when(row0 < n_a)
        def _():
            pltpu.make_async_copy(xa_hbm.at[pl.ds(pl.multiple_of(row0, n_rows), n_rows), :], dst, sem).start()

        @pl.when(row0 >= n_a)
        def _():
            pltpu.make_async_copy(xb_hbm.at[pl.ds(pl.multiple_of(row0 - n_a, n_rows), n_rows), :], dst, sem).start()

    def wait_rows_in(n_rows, dst, sem):
        pltpu.make_async_copy(xa_hbm.at[pl.ds(0, n_rows), :], dst, sem).wait()

    def start_rows_out(row0, n_rows, src, sem, dst_a, dst_b):
        row0 = jnp.asarray(row0, jnp.int32)

        @pl.when(row0 < n_a)
        def _():
            pltpu.make_async_copy(src, dst_a.at[pl.ds(pl.multiple_of(row0, n_rows), n_rows), :], sem).start()

        @pl.when(row0 >= n_a)
        def _():
            pltpu.make_async_copy(src, dst_b.at[pl.ds(pl.multiple_of(row0 - n_a, n_rows), n_rows), :], sem).start()

    def wait_rows_out(n_rows, src, sem, dst_a):
        pltpu.make_async_copy(src, dst_a.at[pl.ds(0, n_rows), :], sem).wait()

    def start_results_out(row0, n_rows, out_src, out_sem, next_src, next_sem):
        start_rows_out(row0, n_rows, out_src, out_sem, oa_hbm, ob_hbm)
        if emit_h:
            start_rows_out(row0, n_rows, next_src, next_sem, na_hbm, nb_hbm)

    def wait_results_out(n_rows, out_src, out_sem, next_src, next_sem):
        wait_rows_out(n_rows, out_src, out_sem, oa_hbm)
        if emit_h:
            wait_rows_out(n_rows, next_src, next_sem, na_hbm)

    def stage_inputs(tj):
        t, j = tj
        j_up = jnp.where(j == n_ff - 1, 0, j + 1)
        chunk = jnp.minimum(j, n_chunks - 1)
        row_pre = jnp.minimum(t + 1, n_tiles - 1) * tm + chunk * rc
        row_epi = jnp.clip(t - 1, 0, n_tiles - 1) * tm + chunk * rc
        return j, j_up, row_pre, row_epi, chunk

    def start_inputs(tj, slot):
        j_down, j_up, row_pre, row_epi, _ = stage_inputs(tj)
        for c in weight_copies(j_down, j_up, slot):
            c.start()
        start_rows_in(row_pre, rc, xpbuf.at[slot], xpsem.at[slot])
        start_rows_in(row_epi, rc, xebuf.at[slot], xesem.at[slot])

    def wait_inputs(slot):
        for c in weight_copies(0, 0, slot):
            c.wait()
        wait_rows_in(rc, xpbuf.at[slot], xpsem.at[slot])
        wait_rows_in(rc, xebuf.at[slot], xesem.at[slot])

    def finishes_rows(tj):
        t, j = tj
        return jnp.logical_and(t >= 1, j < n_chunks)

    def finish_rows(acc_rows, x_rows, out_ref, next_ref):
        out = x_rows + 0.5 * _rms(acc_rows, gpost_ref[...])
        if final_norm:
            out = _rms(out, gfin_ref[...])
        out_ref[...] = out
        if emit_h:
            next_ref[...] = _rms(out, gnext_ref[...]).astype(BF16)

    def up_chunk(h_sel, slot):
        h = hbuf[h_sel]
        gbuf[...] = jnp.dot(h, wgbuf[slot], preferred_element_type=F32)
        ubuf[...] = jnp.dot(h, wubuf[slot], preferred_element_type=F32)

    start_rows_in(0, edge_rows, xedge.at[0], edge_in_sem.at[0])
    for c in range(n_edge):
        wait_rows_in(edge_rows, xedge.at[c % 2], edge_in_sem.at[c % 2])
        if c + 1 < n_edge:
            start_rows_in((c + 1) * edge_rows, edge_rows, xedge.at[(c + 1) % 2], edge_in_sem.at[(c + 1) % 2])
        hbuf[0, pl.ds(c * edge_rows, edge_rows), :] = _rms(xedge[c % 2], gpre_ref[...]).astype(BF16)
        accbuf[1, pl.ds(c * edge_rows, edge_rows), :] = jnp.zeros((edge_rows, accbuf.shape[2]), F32)
    for c in up_weight_copies(0, 1):
        c.start()
    for c in up_weight_copies(0, 1):
        c.wait()
    up_chunk(0, 1)
    start_inputs(coords(0), 0)
    start_inputs(coords(1), 1)
    wait_inputs(0)

    def step(_, carry):
        t, j, slot = carry
        tj = (t, j)
        cur = t % 2
        other = 1 - cur

        @pl.when(j == 0)
        def _():
            def zero(zrows):
                accbuf[cur, zrows, :] = jnp.zeros((edge_rows, accbuf.shape[2]), F32)
            _for_row_chunks(tm, edge_rows, zero)

        abuf[...] = _silu_mul(gbuf[...], ubuf[...]).astype(BF16)
        rows = pl.ds(pl.multiple_of(jnp.minimum(j, n_chunks - 1) * rc, rc), rc)
        hbuf[other, rows, :] = _rms(xpbuf[slot], gpre_ref[...]).astype(BF16)
        finish_rows(accbuf[other, rows, :], xebuf[slot], obuf.at[slot], nbuf.at[slot])
        accbuf[cur] += jnp.dot(abuf[...], wdbuf[slot], preferred_element_type=F32)
        up_chunk(jnp.where(j == n_ff - 1, other, cur), slot)

        after = step_after(tj)
        start_inputs(step_after(after), slot)

        @pl.when(finishes_rows(step_before(tj)))
        def _():
            wait_results_out(rc, obuf.at[1 - slot], osem.at[1 - slot], nbuf.at[1 - slot], nsem.at[1 - slot])

        @pl.when(finishes_rows(tj))
        def _():
            start_results_out(stage_inputs(tj)[3], rc, obuf.at[slot], osem.at[slot], nbuf.at[slot], nsem.at[slot])

        wait_inputs(1 - slot)
        return after + (1 - slot,)

    lax.fori_loop(0, n_steps, step, coords(0) + (jnp.int32(0),))

    wait_inputs(1 - n_steps % 2)

    @pl.when(finishes_rows(coords(n_steps - 1)))
    def _():
        last = (n_steps - 1) % 2
        wait_results_out(rc, obuf.at[last], osem.at[last], nbuf.at[last], nsem.at[last])

    last_row0 = (n_tiles - 1) * tm

    def edge_bufs(c):
        return oedge.at[c % 2], edge_out_sem.at[c % 2], nedge.at[c % 2], edge_next_sem.at[c % 2]

    start_rows_in(last_row0, edge_rows, xedge.at[0], edge_in_sem.at[0])
    for c in range(n_edge):
        wait_rows_in(edge_rows, xedge.at[c % 2], edge_in_sem.at[c % 2])
        if c + 1 < n_edge:
            start_rows_in(last_row0 + (c + 1) * edge_rows, edge_rows, xedge.at[(c + 1) % 2],
                          edge_in_sem.at[(c + 1) % 2])
        if c >= 2:
            wait_results_out(edge_rows, *edge_bufs(c))
        finish_rows(accbuf[(n_tiles - 1) % 2, pl.ds(c * edge_rows, edge_rows), :], xedge[c % 2],
                    oedge.at[c % 2], nedge.at[c % 2])
        start_results_out(last_row0 + c * edge_rows, edge_rows, *edge_bufs(c))
    for c in range(max(n_edge - 2, 0), n_edge):
        wait_results_out(edge_rows, *edge_bufs(c))


def _ffn_pair(xa, xb, g_pre, wg, wu, wd, g_post, g_fin, g_next, *, final_norm, emit_h,
              tm=768, tf=FFN_CHUNK, rc=32, edge_rows=32):
    (n_a, d), n_b = xa.shape, xb.shape[0]
    d_ff = wg.shape[1]
    n_tok = n_a + n_b
    tm = min(tm, n_tok // 2)
    tf = min(tf, d_ff)
    edge_rows = min(edge_rows, tm)
    n_tiles, n_ff = n_tok // tm, d_ff // tf
    assert n_tok % tm == 0 and d_ff % tf == 0 and tm % rc == 0 and tm % edge_rows == 0
    assert n_a % edge_rows == 0 and n_a % rc == 0 and tm // rc <= n_ff and tm // edge_rows >= 2 and n_tiles >= 2
    hbm = pl.BlockSpec(memory_space=pl.ANY)
    vec = pl.BlockSpec(memory_space=pltpu.VMEM)
    out_shape = [jax.ShapeDtypeStruct((n_a, d), F32), jax.ShapeDtypeStruct((n_b, d), F32)]
    if emit_h:
        out_shape += [jax.ShapeDtypeStruct((n_a, d), BF16), jax.ShapeDtypeStruct((n_b, d), BF16)]
    next_rows = (rc, edge_rows) if emit_h else (8, 8)
    return pl.pallas_call(
        functools.partial(_ffn_pair_body, tm=tm, tf=tf, rc=rc, edge_rows=edge_rows, n_a=n_a, n_tiles=n_tiles,
                          n_ff=n_ff, final_norm=final_norm, emit_h=emit_h),
        in_specs=[hbm, hbm, vec, hbm, hbm, hbm, vec, vec, vec],
        out_specs=[hbm] * len(out_shape),
        out_shape=out_shape,
        scratch_shapes=[
            pltpu.VMEM((2, tm, d), BF16),
            pltpu.VMEM((2, tm, d), F32),
            pltpu.VMEM((tm, tf), F32),
            pltpu.VMEM((tm, tf), F32),
            pltpu.VMEM((tm, tf), BF16),
            pltpu.VMEM((2, d, tf), BF16),
            pltpu.VMEM((2, d, tf), BF16),
            pltpu.VMEM((2, tf, d), BF16),
            pltpu.VMEM((2, rc, d), F32),
            pltpu.VMEM((2, rc, d), F32),
            pltpu.VMEM((2, rc, d), F32),
            pltpu.VMEM((2, next_rows[0], d), BF16),
            pltpu.VMEM((2, edge_rows, d), F32),
            pltpu.VMEM((2, edge_rows, d), F32),
            pltpu.VMEM((2, next_rows[1], d), BF16),
            pltpu.SemaphoreType.DMA((2, 3)),
            pltpu.SemaphoreType.DMA((2,)),
            pltpu.SemaphoreType.DMA((2,)),
            pltpu.SemaphoreType.DMA((2,)),
            pltpu.SemaphoreType.DMA((2,)),
            pltpu.SemaphoreType.DMA((2,)),
            pltpu.SemaphoreType.DMA((2,)),
            pltpu.SemaphoreType.DMA((2,)),
        ],
        compiler_params=pltpu.CompilerParams(vmem_limit_bytes=VMEM_LIMIT),
        name="ffn_pair",
    )(xa, xb, g_pre, wg, wu, wd, g_post, g_fin, g_next)


def _inproj_body(h_ref, w_ref, qkv_ref, u_ref, qc_ref, *, n_qkv, n_u):
    j = pl.program_id(1)

    def project_into(o_ref):
        for c in range(0, w_ref.shape[1], MXU_COLS):
            z = jnp.dot(h_ref[...], w_ref[:, c:c + MXU_COLS], preferred_element_type=F32)
            o_ref[:, c:c + MXU_COLS] = z.astype(o_ref.dtype)

    @pl.when(j < n_qkv)
    def _():
        project_into(qkv_ref)

    @pl.when(jnp.logical_and(j >= n_qkv, j < n_qkv + n_u))
    def _():
        project_into(u_ref)

    @pl.when(j >= n_qkv + n_u)
    def _():
        project_into(qc_ref)


def _inproj(h, w_in, *, na_width, pool_width, mem_width, tm=1024, tn=1024):
    n_tok, d = h.shape
    tm = min(tm, n_tok)
    tn = min(tn, pool_width, mem_width)
    assert n_tok % tm == 0 and na_width % tn == 0 and pool_width % tn == 0 and mem_width % tn == 0
    n_qkv = 3 * na_width // tn
    n_u = pool_width // tn
    n_qc = mem_width // tn
    return pl.pallas_call(
        functools.partial(_inproj_body, n_qkv=n_qkv, n_u=n_u),
        grid=(n_tok // tm, n_qkv + n_u + n_qc),
        in_specs=[
            pl.BlockSpec((tm, d), lambda i, j: (i, 0)),
            pl.BlockSpec((d, tn), lambda i, j: (0, j)),
        ],
        out_specs=[
            pl.BlockSpec((tm, tn), lambda i, j: (i, jnp.minimum(j, n_qkv - 1))),
            pl.BlockSpec((tm, tn), lambda i, j: (i, jnp.clip(j - n_qkv, 0, n_u - 1))),
            pl.BlockSpec((tm, tn), lambda i, j: (i, jnp.clip(j - n_qkv - n_u, 0, n_qc - 1))),
        ],
        out_shape=[
            jax.ShapeDtypeStruct((n_tok, 3 * na_width), BF16),
            jax.ShapeDtypeStruct((n_tok, pool_width), F32),
            jax.ShapeDtypeStruct((n_tok, mem_width), BF16),
        ],
        compiler_params=_params(("parallel", "arbitrary")),
        name="inproj",
    )(h, w_in)


def _natten_bias_tables(rpb):
    n_heads, n_dr, n_dc = rpb.shape
    kh, kw = (n_dr + 1) // 2, (n_dc + 1) // 2
    cols = np.arange(GRID_W)
    col_start = np.clip(cols - kw // 2, 0, GRID_W - kw)
    col_valid = (cols[None, :] >= col_start[:, None]) & (cols[None, :] < col_start[:, None] + kw)
    period = 2 * GRID_W
    lpad = GRID_W - kw
    padded = jnp.pad(rpb.astype(F32) * LOG2_E, ((0, 0), (0, 0), (lpad, period - n_dc - lpad)))
    skew = jnp.broadcast_to(padded[:, :, None, :], (n_heads, n_dr, GRID_W, period))
    skew = skew.reshape(n_heads, n_dr, GRID_W * period)[:, :, :GRID_W * (period - 1)]
    by_col = skew.reshape(n_heads, n_dr, GRID_W, period - 1)[:, :, :, GRID_W - 1:]
    by_col = jnp.where(jnp.asarray(col_valid)[None, None], by_col, MASK_VALUE).transpose(0, 1, 3, 2)
    rpad = NAT_K_ROWS - kh
    wide = jnp.concatenate([by_col, by_col], axis=-1)
    wide = jnp.pad(wide, ((0, 0), (rpad, 2 * NAT_K_ROWS - n_dr - rpad), (0, 0), (0, 0)), constant_values=MASK_VALUE)
    n_sub = NAT_Q_ROWS // NAT_SUB_ROWS
    block = (n_sub, _natten_window_rows(kh) * GRID_W, NAT_SUB_ROWS * GRID_W)
    return pl.pallas_call(
        functools.partial(_natten_bias_body, kh=kh),
        grid=(n_heads,),
        in_specs=[pl.BlockSpec((1,) + wide.shape[1:], lambda h: (h, 0, 0, 0))],
        out_specs=pl.BlockSpec((3, 1) + block, lambda h: (0, h, 0, 0, 0)),
        out_shape=jax.ShapeDtypeStruct((3, n_heads) + block, F32),
        compiler_params=_params(("parallel",)),
        name="natten_bias",
    )(wide)


def _natten_row_start(r, kh):
    return min(max(r - kh // 2, 0), NAT_K_ROWS - kh)


def _natten_window_rows(kh):
    return NAT_SUB_ROWS - 1 + kh


def _natten_window_start(t, sub, kh):
    r_first = (NAT_K_ROWS - NAT_Q_ROWS) // 2 * t + sub * NAT_SUB_ROWS
    return min(_natten_row_start(r_first, kh), NAT_K_ROWS - _natten_window_rows(kh))


def _natten_bias_body(wide_ref, o_ref, *, kh):
    low_half = lax.broadcasted_iota(jnp.int32, (GRID_W, 2 * GRID_W), 1) < GRID_W
    masked = jnp.full((GRID_W, 2 * GRID_W), MASK_VALUE, F32)
    for t in range(3):
        for sub in range(NAT_Q_ROWS // NAT_SUB_ROWS):
            first_key_row = _natten_window_start(t, sub, kh)
            for kl in range(_natten_window_rows(kh)):
                rj = first_key_row + kl
                for pair in range(NAT_SUB_ROWS // 2):
                    halves = []
                    for ri in (sub * NAT_SUB_ROWS + 2 * pair, sub * NAT_SUB_ROWS + 2 * pair + 1):
                        r = (NAT_K_ROWS - NAT_Q_ROWS) // 2 * t + ri
                        valid = _natten_row_start(r, kh) <= rj < _natten_row_start(r, kh) + kh
                        halves.append(wide_ref[0, rj - r + NAT_K_ROWS - 1] if valid else masked)
                    o_ref[t, 0, sub, pl.ds(kl * GRID_W, GRID_W), pl.ds(pair * 2 * GRID_W, 2 * GRID_W)] = (
                        jnp.where(low_half, halves[0], halves[1]))


def _natten_body(q_ref, k0_ref, k1_ref, k2_ref, k3_ref, v0_ref, v1_ref, v2_ref, v3_ref, bias_ref, o_ref,
                 kbuf, vbuf, *, log2_scale, head_dim, kh):
    r = pl.program_id(2)
    table = jnp.where(r == 0, 0, jnp.where(r == pl.num_programs(2) - 1, 2, 1))
    chunk = k0_ref.shape[1]
    for c, (k_ref, v_ref) in enumerate(((k0_ref, v0_ref), (k1_ref, v1_ref), (k2_ref, v2_ref), (k3_ref, v3_ref))):
        kbuf[pl.ds(c * chunk, chunk), :] = k_ref[0]
        vbuf[pl.ds(c * chunk, chunk), :] = v_ref[0]
    n_win = _natten_window_rows(kh) * GRID_W
    n_q = NAT_SUB_ROWS * GRID_W
    blocks = []
    for sub in range(NAT_Q_ROWS // NAT_SUB_ROWS):
        starts = [_natten_window_start(t, sub, kh) * GRID_W for t in range(3)]
        start = jnp.where(table == 0, starts[0], jnp.where(table == 1, starts[1], starts[2]))
        keys = pl.ds(pl.multiple_of(start, GRID_W), n_win)
        for h in range(q_ref.shape[2] // head_dim):
            blocks.append((sub, h, keys, pl.ds(sub * n_q, n_q), slice(h * head_dim, (h + 1) * head_dim)))
    scores = [lax.dot_general(kbuf[keys, cols], q_ref[0, queries, cols], (((1,), (1,)), ((), ())),
                              preferred_element_type=F32)
              for _, _, keys, queries, cols in blocks]
    probs = []
    for (sub, h, _, _, _), s in zip(blocks, scores):
        s = s * log2_scale + bias_ref[0, h, sub]
        p = jnp.exp2(s - jnp.max(s, axis=0, keepdims=True))
        probs.append((p.astype(BF16), jnp.sum(p, axis=0, keepdims=True)))
    for (_, _, keys, queries, cols), (p, l) in zip(blocks, probs):
        o = lax.dot_general(vbuf[keys, cols], p, (((0,), (0,)), ((), ())), preferred_element_type=F32)
        o_ref[0, queries, cols] = (o / l).T.astype(o_ref.dtype)


def _natten(qkv, bias, *, n_heads, head_dim, kh, heads_per_step=8):
    b, seq, _ = qkv.shape
    tq = NAT_Q_ROWS * GRID_W
    tk = NAT_K_CHUNK_ROWS * GRID_W
    n_kc = NAT_K_ROWS // NAT_K_CHUNK_ROWS
    hp = min(heads_per_step, n_heads)
    assert n_kc == 4 and seq % (NAT_K_ROWS * GRID_W) == 0 and n_heads % hp == 0
    n_hg = n_heads // hp
    n_qb = seq // tq
    n_kb = seq // tk
    lead = (NAT_K_ROWS - NAT_Q_ROWS) // 2 // NAT_K_CHUNK_ROWS

    def kv_spec(which, c):
        def index(bi, h, r):
            start = jnp.clip(r * (tq // tk) - lead, 0, n_kb - n_kc)
            return (bi, start + c, which * n_hg + h)
        return pl.BlockSpec((1, tk, hp * head_dim), index)

    def bias_index(bi, h, r):
        t = jnp.where(r == 0, 0, jnp.where(r == n_qb - 1, 2, 1))
        return (t, h, 0, 0, 0)

    return pl.pallas_call(
        functools.partial(_natten_body, log2_scale=head_dim ** -0.5 * LOG2_E, head_dim=head_dim, kh=kh),
        grid=(b, n_hg, n_qb),
        in_specs=[pl.BlockSpec((1, tq, hp * head_dim), lambda bi, h, r: (bi, r, h))]
        + [kv_spec(1, c) for c in range(n_kc)]
        + [kv_spec(2, c) for c in range(n_kc)]
        + [pl.BlockSpec((1, hp) + bias.shape[2:], bias_index)],
        out_specs=pl.BlockSpec((1, tq, hp * head_dim), lambda bi, h, r: (bi, r, h)),
        out_shape=jax.ShapeDtypeStruct((b, seq, n_heads * head_dim), BF16),
        scratch_shapes=[pltpu.VMEM((NAT_K_ROWS * GRID_W, hp * head_dim), BF16),
                        pltpu.VMEM((NAT_K_ROWS * GRID_W, hp * head_dim), BF16)],
        compiler_params=_params(("parallel", "parallel", "arbitrary")),
        name="natten",
    )(qkv, *([qkv] * (2 * n_kc)), bias)


def _pool_body(u_ref, prev_ref, next_ref, w_ref, scale_ref, o_ref, ext_ref, *, seq, halo, group_dim):
    t = pl.program_id(1)
    tt = u_ref.shape[1]
    u = u_ref[0]
    ext_ref[pl.ds(halo, tt), :] = u
    ext_ref[pl.ds(0, halo), :] = jnp.where(t > 0, prev_ref[0], 0.0)
    ext_ref[pl.ds(halo + tt, halo), :] = jnp.where(t < pl.num_programs(1) - 1, next_ref[0], 0.0)
    pos = t * tt + lax.broadcasted_iota(jnp.int32, (tt, 1), 0)
    for g, w in enumerate(POOL_WINDOWS):
        cols = pl.ds(g * group_dim, group_dim)
        total = ext_ref[pl.ds(halo - w // 2, tt), cols]
        for d in range(-w // 2 + 1, w // 2):
            total = total + ext_ref[pl.ds(halo + d, tt), cols]
        count = jnp.minimum(pos + w // 2, seq) - jnp.maximum(pos - w // 2, 0)
        pooled = total / count.astype(F32) - u[:, g * group_dim:(g + 1) * group_dim]
        mixed = jnp.dot(pooled.astype(BF16), w_ref[g], preferred_element_type=F32)
        o_ref[0, :, cols] = (mixed * scale_ref[:, cols]).astype(o_ref.dtype)


def _pool(u, w_pool, pool_scale, *, tt=512):
    b, seq, width = u.shape
    n_groups, group_dim, _ = w_pool.shape
    assert n_groups == len(POOL_WINDOWS)
    halo = 8
    assert max(POOL_WINDOWS) // 2 <= halo
    tt = min(tt, seq)
    assert seq % tt == 0 and tt % halo == 0
    hb = tt // halo
    return pl.pallas_call(
        functools.partial(_pool_body, seq=seq, halo=halo, group_dim=group_dim),
        grid=(b, seq // tt),
        in_specs=[
            pl.BlockSpec((1, tt, width), lambda bi, t: (bi, t, 0)),
            pl.BlockSpec((1, halo, width), lambda bi, t: (bi, jnp.maximum(t * hb - 1, 0), 0)),
            pl.BlockSpec((1, halo, width), lambda bi, t: (bi, jnp.minimum((t + 1) * hb, seq // halo - 1), 0)),
            pl.BlockSpec((n_groups, group_dim, group_dim), lambda bi, t: (0, 0, 0)),
            pl.BlockSpec((1, width), lambda bi, t: (0, 0)),
        ],
        out_specs=pl.BlockSpec((1, tt, width), lambda bi, t: (bi, t, 0)),
        out_shape=jax.ShapeDtypeStruct((b, seq, width), BF16),
        scratch_shapes=[pltpu.VMEM((tt + 2 * halo, width), F32)],
        compiler_params=_params(("parallel", "arbitrary")),
        name="pool",
    )(u, u, u, w_pool, pool_scale)


def _memkv_body(m_ref, g_ref, w_ref, o_ref):
    h = _rms(m_ref[...], g_ref[...]).astype(BF16)
    o_ref[...] = jnp.dot(h, w_ref[...], preferred_element_type=F32).astype(o_ref.dtype)


def _memkv(mem, g_mem, w_kv, *, tr=256, tn=512):
    n_rows, d = mem.shape
    width = w_kv.shape[1]
    tr = min(tr, n_rows)
    tn = min(tn, width)
    assert n_rows % tr == 0 and width % tn == 0
    return pl.pallas_call(
        _memkv_body,
        grid=(n_rows // tr, width // tn),
        in_specs=[
            pl.BlockSpec((tr, d), lambda i, j: (i, 0)),
            pl.BlockSpec((1, d), lambda i, j: (0, 0)),
            pl.BlockSpec((d, tn), lambda i, j: (0, j)),
        ],
        out_specs=pl.BlockSpec((tr, tn), lambda i, j: (i, j)),
        out_shape=jax.ShapeDtypeStruct((n_rows, width), BF16),
        compiler_params=_params(("parallel", "arbitrary")),
        name="memkv",
    )(mem, g_mem, w_kv)


def _memattn_body(q_ref, kv_ref, o_ref, *, n_heads, head_dim):
    scale = head_dim ** -0.5
    width = n_heads * head_dim
    for h in range(n_heads):
        cols = slice(h * head_dim, (h + 1) * head_dim)
        q = q_ref[0, :, cols]
        k = kv_ref[0, :, cols]
        v = kv_ref[0, :, width + h * head_dim: width + (h + 1) * head_dim]
        s = lax.dot_general(q, k, (((1,), (1,)), ((), ())), preferred_element_type=F32) * scale
        m = jnp.max(s, axis=-1, keepdims=True)
        p = jnp.exp(s - m)
        l = jnp.sum(p, axis=-1, keepdims=True)
        o = jnp.dot(p.astype(BF16), v, preferred_element_type=F32)
        o_ref[0, :, cols] = (o / l).astype(o_ref.dtype)


def _memattn(q, kv, *, n_heads, tq=512):
    b, seq, width = q.shape
    n_mem = kv.shape[1]
    tq = min(tq, seq)
    assert seq % tq == 0
    return pl.pallas_call(
        functools.partial(_memattn_body, n_heads=n_heads, head_dim=width // n_heads),
        grid=(b, seq // tq),
        in_specs=[
            pl.BlockSpec((1, tq, width), lambda bi, t: (bi, t, 0)),
            pl.BlockSpec((1, n_mem, 2 * width), lambda bi, t: (bi, 0, 0)),
        ],
        out_specs=pl.BlockSpec((1, tq, width), lambda bi, t: (bi, t, 0)),
        out_shape=jax.ShapeDtypeStruct((b, seq, width), BF16),
        compiler_params=_params(("parallel", "arbitrary")),
        name="memattn",
    )(q, kv)


def _merge_body(h_ref, a_ref, b_ref, c_ref, wga_ref, wgb_ref, wgc_ref, bg_ref, wa_ref, wb_ref, wc_ref, o_ref):
    h = h_ref[...]
    merged = None
    for i, (x_ref, wg_ref, w_ref) in enumerate(
            ((a_ref, wga_ref, wa_ref), (b_ref, wgb_ref, wb_ref), (c_ref, wgc_ref, wc_ref))):
        logits = jnp.dot(h, wg_ref[...], preferred_element_type=F32) + bg_ref[i:i + 1, :]
        gate = 1.0 / (1.0 + jnp.exp(-logits))
        y = jnp.dot(x_ref[...], w_ref[...], preferred_element_type=F32)
        merged = gate * y if merged is None else merged + gate * y
    o_ref[...] = merged.astype(o_ref.dtype)


def _merge(h, att_a, mix_b, att_c, w_in, gate_col0, b_gate, w_a, w_b, w_c, *, tm=512, tn=512):
    n_tok, d = h.shape
    tm = min(tm, n_tok)
    tn = min(tn, d)
    assert n_tok % tm == 0 and d % tn == 0 and gate_col0 % tn == 0
    nj = d // tn
    g0 = gate_col0 // tn

    def act(x):
        return pl.BlockSpec((tm, x.shape[1]), lambda i, j: (i, 0))

    def gate_w(branch):
        return pl.BlockSpec((d, tn), lambda i, j: (0, g0 + branch * nj + j))

    def out_w(w):
        return pl.BlockSpec((w.shape[0], tn), lambda i, j: (0, j))

    return pl.pallas_call(
        _merge_body,
        grid=(n_tok // tm, nj),
        in_specs=[act(h), act(att_a), act(mix_b), act(att_c),
                  gate_w(0), gate_w(1), gate_w(2),
                  pl.BlockSpec((b_gate.shape[0], tn), lambda i, j: (0, j)),
                  out_w(w_a), out_w(w_b), out_w(w_c)],
        out_specs=pl.BlockSpec((tm, tn), lambda i, j: (i, j)),
        out_shape=jax.ShapeDtypeStruct((n_tok, d), BF16),
        compiler_params=_params(("parallel", "arbitrary")),
        name="merge",
    )(h, att_a, mix_b, att_c, w_in, w_in, w_in, b_gate, w_a, w_b, w_c)


def _wo_body(x_ref, m_ref, w_ref, g_ref, o_ref, *, row_chunk):
    j = pl.program_id(1)
    tm = x_ref.shape[0]

    @pl.when(j == 0)
    def _():
        def prologue(rows):
            o_ref[rows, :] = jnp.zeros((row_chunk, o_ref.shape[1]), F32)
        _for_row_chunks(tm, row_chunk, prologue)

    o_ref[...] += jnp.dot(m_ref[...], w_ref[...], preferred_element_type=F32)

    @pl.when(j == pl.num_programs(1) - 1)
    def _():
        def epilogue(rows):
            o_ref[rows, :] = x_ref[rows, :] + _rms(o_ref[rows, :], g_ref[...])
        _for_row_chunks(tm, row_chunk, epilogue)


def _wo(x, merged, w_o, g_post, *, tm=512, tk=1024):
    n_tok, d = x.shape
    tm = min(tm, n_tok)
    tk = min(tk, d)
    assert n_tok % tm == 0 and d % tk == 0
    return pl.pallas_call(
        functools.partial(_wo_body, row_chunk=min(64, tm)),
        grid=(n_tok // tm, d // tk),
        in_specs=[
            pl.BlockSpec((tm, d), lambda i, j: (i, 0)),
            pl.BlockSpec((tm, tk), lambda i, j: (i, j)),
            pl.BlockSpec((tk, d), lambda i, j: (j, 0)),
            pl.BlockSpec((1, d), lambda i, j: (0, 0)),
        ],
        out_specs=pl.BlockSpec((tm, d), lambda i, j: (i, 0)),
        out_shape=jax.ShapeDtypeStruct((n_tok, d), F32),
        compiler_params=_params(("parallel", "arbitrary")),
        name="wo",
    )(x, merged, w_o, g_post)


def _mixer(x1, h, mem, p, bias_tables, *, b, seq):
    n_tok, d = x1.shape
    na_heads = p["rpb"].shape[0]
    n_groups, group_dim, _ = p["w_pool"].shape
    pool_width = n_groups * group_dim
    mem_width = p["w_mem_kv"].shape[1] // 2
    na_width = p["w_a_out"].shape[0]
    gate_col0 = 3 * na_width + pool_width + mem_width

    qkv, u, qc = _inproj(h, p["w_in"], na_width=na_width, pool_width=pool_width, mem_width=mem_width)
    att_a = _natten(qkv.reshape(b, seq, 3 * na_width), bias_tables,
                    n_heads=na_heads, head_dim=na_width // na_heads, kh=(p["rpb"].shape[1] + 1) // 2)
    mix_b = _pool(u.reshape(b, seq, pool_width), p["w_pool"], p["pool_scale"])
    kv = _memkv(mem.reshape(b * mem.shape[1], d), p["g_mem"], p["w_mem_kv"])
    att_c = _memattn(qc.reshape(b, seq, mem_width), kv.reshape(b, mem.shape[1], 2 * mem_width),
                     n_heads=MEM_HEADS)
    merged = _merge(h, att_a.reshape(n_tok, na_width), mix_b.reshape(n_tok, pool_width),
                    att_c.reshape(n_tok, mem_width), p["w_in"], gate_col0, p["b_gate"],
                    p["w_a_out"], p["w_b_out"], p["w_c_out"])
    return _wo(x1, merged, p["w_o"], p["g_mix_post"])


_MATMUL_WEIGHTS = ("w1_gate", "w1_up", "w1_down", "w_in", "w_pool", "w_mem_kv",
                   "w_a_out", "w_b_out", "w_c_out", "w_o", "w2_gate", "w2_up", "w2_down")
_ROW_VECTORS = ("g_ffn1_pre", "g_ffn1_post", "g_mix_pre", "pool_scale", "g_mem", "g_mix_post",
                "g_ffn2_pre", "g_ffn2_post", "g_final")


def kernel(x_prompt, x_sample, mem_prompt, mem_sample, g_ffn1_pre, w1_gate, w1_up, w1_down, g_ffn1_post, g_mix_pre, w_in, rpb, w_pool, pool_scale, g_mem, w_mem_kv, w_a_out, w_b_out, w_c_out, b_gate, w_o, g_mix_post, g_ffn2_pre, w2_gate, w2_up, w2_down, g_ffn2_post, g_final):
    stacked = dict(g_ffn1_pre=g_ffn1_pre, w1_gate=w1_gate, w1_up=w1_up, w1_down=w1_down,
                   g_ffn1_post=g_ffn1_post, g_mix_pre=g_mix_pre, w_in=w_in, rpb=rpb, w_pool=w_pool,
                   pool_scale=pool_scale, g_mem=g_mem, w_mem_kv=w_mem_kv, w_a_out=w_a_out,
                   w_b_out=w_b_out, w_c_out=w_c_out, b_gate=b_gate, w_o=w_o, g_mix_post=g_mix_post,
                   g_ffn2_pre=g_ffn2_pre, w2_gate=w2_gate, w2_up=w2_up, w2_down=w2_down,
                   g_ffn2_post=g_ffn2_post, g_final=g_final)
    y_prompt, y_sample = x_prompt, x_sample
    for layer in range(g_final.shape[0]):
        p = {name: a[layer] for name, a in stacked.items()}
        for name in _MATMUL_WEIGHTS:
            p[name] = p[name].astype(BF16)
        for name in _ROW_VECTORS:
            p[name] = p[name].reshape(1, -1)
        bias_tables = _natten_bias_tables(p["rpb"])
        (bp, sp, d), (bs, ss, _) = y_prompt.shape, y_sample.shape
        x1p, x1s, hp, hs = _ffn_pair(
            y_prompt.reshape(bp * sp, d), y_sample.reshape(bs * ss, d),
            p["g_ffn1_pre"], p["w1_gate"], p["w1_up"], p["w1_down"], p["g_ffn1_post"], p["g_final"],
            p["g_mix_pre"], final_norm=False, emit_h=True)
        x2p = _mixer(x1p, hp, mem_prompt, p, bias_tables, b=bp, seq=sp)
        x2s = _mixer(x1s, hs, mem_sample, p, bias_tables, b=bs, seq=ss)
        yp, ys = _ffn_pair(
            x2p, x2s, p["g_ffn2_pre"], p["w2_gate"], p["w2_up"], p["w2_down"], p["g_ffn2_post"], p["g_final"],
            p["g_final"], final_norm=True, emit_h=False)
        y_prompt, y_sample = yp.reshape(bp, sp, d), ys.reshape(bs, ss, d)
    return (y_prompt, y_sample)
```

```python
import functools

import numpy as np
import jax
import jax.numpy as jnp
from jax import lax
from jax.experimental import pallas as pl
from jax.experimental.pallas import tpu as pltpu

GRID_W = 64
POOL_WINDOWS = (2, 4, 8, 16)
MEM_HEADS = 4
RMS_EPS = 1e-6
MASK_VALUE = -1e30
LOG2_E = 1.4426950408889634

FFN_CHUNK = 256
NAT_Q_ROWS = 8
NAT_K_ROWS = 16
NAT_K_CHUNK_ROWS = 4
NAT_SUB_ROWS = 4

MXU_COLS = 256
V7X_VMEM_BYTES = 64 * 1024 * 1024
VMEM_LIMIT = V7X_VMEM_BYTES - 6 * 1024 * 1024

F32 = jnp.float32
BF16 = jnp.bfloat16


def _rms(x, g):
    ms = jnp.mean(x * x, axis=-1, keepdims=True)
    return (x * lax.rsqrt(ms + RMS_EPS)) * g


def _silu_mul(g, u):
    return (g * (1.0 / (1.0 + jnp.exp(-g)))) * u


def _for_row_chunks(n_rows, chunk, fn):
    def body(c, carry):
        fn(pl.ds(pl.multiple_of(c * chunk, chunk), chunk))
        return carry
    lax.fori_loop(0, n_rows // chunk, body, 0)


def _params(sem):
    return pltpu.CompilerParams(dimension_semantics=sem, vmem_limit_bytes=VMEM_LIMIT)


def _ffn_pair_body(*refs, tm, tf, rc, edge_rows, n_a, n_tiles, n_ff, final_norm, emit_h):
    (xa_hbm, xb_hbm, gpre_ref, wg_hbm, wu_hbm, wd_hbm, gpost_ref, gfin_ref, gnext_ref), refs = refs[:9], refs[9:]
    if emit_h:
        (oa_hbm, ob_hbm, na_hbm, nb_hbm), refs = refs[:4], refs[4:]
    else:
        (oa_hbm, ob_hbm), refs = refs[:2], refs[2:]
        na_hbm = nb_hbm = None
    (hbuf, accbuf, gbuf, ubuf, abuf, wgbuf, wubuf, wdbuf, xpbuf, xebuf, obuf, nbuf, xedge, oedge, nedge,
     wsem, xpsem, xesem, osem, nsem, edge_in_sem, edge_out_sem, edge_next_sem) = refs

    n_chunks = tm // rc
    n_steps = n_tiles * n_ff
    n_edge = tm // edge_rows

    def step_after(tj):
        t, j = tj
        wrap = j == n_ff - 1
        return jnp.where(wrap, t + 1, t), jnp.where(wrap, 0, j + 1)

    def step_before(tj):
        t, j = tj
        wrap = j == 0
        return jnp.where(wrap, t - 1, t), jnp.where(wrap, n_ff - 1, j - 1)

    def coords(s):
        return jnp.int32(s // n_ff), jnp.int32(s % n_ff)

    def up_weight_copies(j_up, slot):
        cols = pl.ds(pl.multiple_of(j_up * tf, tf), tf)
        return [pltpu.make_async_copy(wg_hbm.at[:, cols], wgbuf.at[slot], wsem.at[slot, 0]),
                pltpu.make_async_copy(wu_hbm.at[:, cols], wubuf.at[slot], wsem.at[slot, 1])]

    def weight_copies(j_down, j_up, slot):
        rows = pl.ds(pl.multiple_of(j_down * tf, tf), tf)
        return up_weight_copies(j_up, slot) + [
            pltpu.make_async_copy(wd_hbm.at[rows, :], wdbuf.at[slot], wsem.at[slot, 2])]

    def start_rows_in(row0, n_rows, dst, sem):
        row0 = jnp.asarray(row0, jnp.int32)

        @pl.when(row0 < n_a)
        def _():
            pltpu.make_async_copy(xa_hbm.at[pl.ds(pl.multiple_of(row0, n_rows), n_rows), :], dst, sem).start()

        @pl.when(row0 >= n_a)
        def _():
            pltpu.make_async_copy(xb_hbm.at[pl.ds(pl.multiple_of(row0 - n_a, n_rows), n_rows), :], dst, sem).start()

    def wait_rows_in(n_rows, dst, sem):
        pltpu.make_async_copy(xa_hbm.at[pl.ds(0, n_rows), :], dst, sem).wait()

    def start_rows_out(row0, n_rows, src, sem, dst_a, dst_b):
        row0 = jnp.asarray(row0, jnp.int32)

        @pl.when(row0 < n_a)
        def _():
            pltpu.make_async_copy(src, dst_a.at[pl.ds(pl.multiple_of(row0, n_rows), n_rows), :], sem).start()

        @pl.when(row0 >= n_a)
        def _():
            pltpu.make_async_copy(src, dst_b.at[pl.ds(pl.multiple_of(row0 - n_a, n_rows), n_rows), :], sem).start()

    def wait_rows_out(n_rows, src, sem, dst_a):
        pltpu.make_async_copy(src, dst_a.at[pl.ds(0, n_rows), :], sem).wait()

    def start_results_out(row0, n_rows, out_src, out_sem, next_src, next_sem):
        start_rows_out(row0, n_rows, out_src, out_sem, oa_hbm, ob_hbm)
        if emit_h:
            start_rows_out(row0, n_rows, next_src, next_sem, na_hbm, nb_hbm)

    def wait_results_out(n_rows, out_src, out_sem, next_src, next_sem):
        wait_rows_out(n_rows, out_src, out_sem, oa_hbm)
        if emit_h:
            wait_rows_out(n_rows, next_src, next_sem, na_hbm)

    def pre_chunk(j):
        return jnp.minimum(j, n_chunks - 1)

    def epi_chunk(j):
        return jnp.maximum(j - (n_ff - n_chunks), 0)

    def stage_inputs(tj):
        t, j = tj
        j_up = jnp.where(j == n_ff - 1, 0, j + 1)
        row_pre = jnp.minimum(t + 1, n_tiles - 1) * tm + pre_chunk(j) * rc
        row_epi = jnp.clip(t - 1, 0, n_tiles - 1) * tm + epi_chunk(j) * rc
        return j, j_up, row_pre, row_epi

    def start_inputs(tj, slot):
        j_down, j_up, row_pre, row_epi = stage_inputs(tj)
        for c in weight_copies(j_down, j_up, slot):
            c.start()
        start_rows_in(row_pre, rc, xpbuf.at[slot], xpsem.at[slot])
        start_rows_in(row_epi, rc, xebuf.at[slot], xesem.at[slot])

    def wait_inputs(slot):
        for c in weight_copies(0, 0, slot):
            c.wait()
        wait_rows_in(rc, xpbuf.at[slot], xpsem.at[slot])
        wait_rows_in(rc, xebuf.at[slot], xesem.at[slot])

    def finishes_rows(tj):
        t, j = tj
        return jnp.logical_and(t >= 1, j >= n_ff - n_chunks)

    def finish_rows(acc_rows, x_rows, out_ref, next_ref):
        out = x_rows + 0.5 * _rms(acc_rows, gpost_ref[...])
        if final_norm:
            out = _rms(out, gfin_ref[...])
        out_ref[...] = out
        if emit_h:
            next_ref[...] = _rms(out, gnext_ref[...]).astype(BF16)

    def up_chunk(h_sel, slot):
        h = hbuf[h_sel]
        gbuf[...] = jnp.dot(h, wgbuf[slot], preferred_element_type=F32)
        ubuf[...] = jnp.dot(h, wubuf[slot], preferred_element_type=F32)

    start_rows_in(0, edge_rows, xedge.at[0], edge_in_sem.at[0])
    for c in range(n_edge):
        wait_rows_in(edge_rows, xedge.at[c % 2], edge_in_sem.at[c % 2])
        if c + 1 < n_edge:
            start_rows_in((c + 1) * edge_rows, edge_rows, xedge.at[(c + 1) % 2], edge_in_sem.at[(c + 1) % 2])
        hbuf[0, pl.ds(c * edge_rows, edge_rows), :] = _rms(xedge[c % 2], gpre_ref[...]).astype(BF16)
        accbuf[1, pl.ds(c * edge_rows, edge_rows), :] = jnp.zeros((edge_rows, accbuf.shape[2]), F32)
    for c in up_weight_copies(0, 1):
        c.start()
    for c in up_weight_copies(0, 1):
        c.wait()
    up_chunk(0, 1)
    start_inputs(coords(0), 0)
    start_inputs(coords(1), 1)
    wait_inputs(0)

    def step(_, carry):
        t, j, slot = carry
        tj = (t, j)
        cur = t % 2
        other = 1 - cur

        @pl.when(j == 0)
        def _():
            def zero(zrows):
                accbuf[cur, zrows, :] = jnp.zeros((edge_rows, accbuf.shape[2]), F32)
            _for_row_chunks(tm, edge_rows, zero)

        abuf[...] = _silu_mul(gbuf[...], ubuf[...]).astype(BF16)
        pre_rows = pl.ds(pl.multiple_of(pre_chunk(j) * rc, rc), rc)
        epi_rows = pl.ds(pl.multiple_of(epi_chunk(j) * rc, rc), rc)
        hbuf[other, pre_rows, :] = _rms(xpbuf[slot], gpre_ref[...]).astype(BF16)
        finish_rows(accbuf[other, epi_rows, :], xebuf[slot], obuf.at[slot], nbuf.at[slot])
        accbuf[cur] += jnp.dot(abuf[...], wdbuf[slot], preferred_element_type=F32)
        up_chunk(jnp.where(j == n_ff - 1, other, cur), slot)

        after = step_after(tj)
        start_inputs(step_after(after), slot)

        @pl.when(finishes_rows(step_before(tj)))
        def _():
            wait_results_out(rc, obuf.at[1 - slot], osem.at[1 - slot], nbuf.at[1 - slot], nsem.at[1 - slot])

        @pl.when(finishes_rows(tj))
        def _():
            start_results_out(stage_inputs(tj)[3], rc, obuf.at[slot], osem.at[slot], nbuf.at[slot], nsem.at[slot])

        wait_inputs(1 - slot)
        return after + (1 - slot,)

    lax.fori_loop(0, n_steps, step, coords(0) + (jnp.int32(0),))

    wait_inputs(1 - n_steps % 2)

    @pl.when(finishes_rows(coords(n_steps - 1)))
    def _():
        last = (n_steps - 1) % 2
        wait_results_out(rc, obuf.at[last], osem.at[last], nbuf.at[last], nsem.at[last])

    last_row0 = (n_tiles - 1) * tm

    def edge_bufs(c):
        return oedge.at[c % 2], edge_out_sem.at[c % 2], nedge.at[c % 2], edge_next_sem.at[c % 2]

    start_rows_in(last_row0, edge_rows, xedge.at[0], edge_in_sem.at[0])
    for c in range(n_edge):
        wait_rows_in(edge_rows, xedge.at[c % 2], edge_in_sem.at[c % 2])
        if c + 1 < n_edge:
            start_rows_in(last_row0 + (c + 1) * edge_rows, edge_rows, xedge.at[(c + 1) % 2],
                          edge_in_sem.at[(c + 1) % 2])
        if c >= 2:
            wait_results_out(edge_rows, *edge_bufs(c))
        finish_rows(accbuf[(n_tiles - 1) % 2, pl.ds(c * edge_rows, edge_rows), :], xedge[c % 2],
                    oedge.at[c % 2], nedge.at[c % 2])
        start_results_out(last_row0 + c * edge_rows, edge_rows, *edge_bufs(c))
    for c in range(max(n_edge - 2, 0), n_edge):
        wait_results_out(edge_rows, *edge_bufs(c))


def _ffn_pair(xa, xb, g_pre, wg, wu, wd, g_post, g_fin, g_next, *, final_norm, emit_h,
              tm=768, tf=FFN_CHUNK, rc=32, edge_rows=32):
    (n_a, d), n_b = xa.shape, xb.shape[0]
    d_ff = wg.shape[1]
    n_tok = n_a + n_b
    tm = min(tm, n_tok // 2)
    tf = min(tf, d_ff)
    edge_rows = min(edge_rows, tm)
    n_tiles, n_ff = n_tok // tm, d_ff // tf
    assert n_tok % tm == 0 and d_ff % tf == 0 and tm % rc == 0 and tm % edge_rows == 0
    assert n_a % edge_rows == 0 and n_a % rc == 0 and tm // rc <= n_ff and tm // edge_rows >= 2 and n_tiles >= 2
    hbm = pl.BlockSpec(memory_space=pl.ANY)
    vec = pl.BlockSpec(memory_space=pltpu.VMEM)
    out_shape = [jax.ShapeDtypeStruct((n_a, d), F32), jax.ShapeDtypeStruct((n_b, d), F32)]
    if emit_h:
        out_shape += [jax.ShapeDtypeStruct((n_a, d), BF16), jax.ShapeDtypeStruct((n_b, d), BF16)]
    next_rows = (rc, edge_rows) if emit_h else (8, 8)
    return pl.pallas_call(
        functools.partial(_ffn_pair_body, tm=tm, tf=tf, rc=rc, edge_rows=edge_rows, n_a=n_a, n_tiles=n_tiles,
                          n_ff=n_ff, final_norm=final_norm, emit_h=emit_h),
        in_specs=[hbm, hbm, vec, hbm, hbm, hbm, vec, vec, vec],
        out_specs=[hbm] * len(out_shape),
        out_shape=out_shape,
        scratch_shapes=[
            pltpu.VMEM((2, tm, d), BF16),
            pltpu.VMEM((2, tm, d), F32),
            pltpu.VMEM((tm, tf), F32),
            pltpu.VMEM((tm, tf), F32),
            pltpu.VMEM((tm, tf), BF16),
            pltpu.VMEM((2, d, tf), BF16),
            pltpu.VMEM((2, d, tf), BF16),
            pltpu.VMEM((2, tf, d), BF16),
            pltpu.VMEM((2, rc, d), F32),
            pltpu.VMEM((2, rc, d), F32),
            pltpu.VMEM((2, rc, d), F32),
            pltpu.VMEM((2, next_rows[0], d), BF16),
            pltpu.VMEM((2, edge_rows, d), F32),
            pltpu.VMEM((2, edge_rows, d), F32),
            pltpu.VMEM((2, next_rows[1], d), BF16),
            pltpu.SemaphoreType.DMA((2, 3)),
            pltpu.SemaphoreType.DMA((2,)),
            pltpu.SemaphoreType.DMA((2,)),
            pltpu.SemaphoreType.DMA((2,)),
            pltpu.SemaphoreType.DMA((2,)),
            pltpu.SemaphoreType.DMA((2,)),
            pltpu.SemaphoreType.DMA((2,)),
            pltpu.SemaphoreType.DMA((2,)),
        ],
        compiler_params=pltpu.CompilerParams(vmem_limit_bytes=VMEM_LIMIT),
        name="ffn_pair",
    )(xa, xb, g_pre, wg, wu, wd, g_post, g_fin, g_next)


def _inproj_body(h_ref, w_ref, qkv_ref, u_ref, qc_ref, *, n_qkv, n_u):
    j = pl.program_id(1)

    def project_into(o_ref):
        for c in range(0, w_ref.shape[1], MXU_COLS):
            z = jnp.dot(h_ref[...], w_ref[:, c:c + MXU_COLS], preferred_element_type=F32)
            o_ref[:, c:c + MXU_COLS] = z.astype(o_ref.dtype)

    @pl.when(j < n_qkv)
    def _():
        project_into(qkv_ref)

    @pl.when(jnp.logical_and(j >= n_qkv, j < n_qkv + n_u))
    def _():
        project_into(u_ref)

    @pl.when(j >= n_qkv + n_u)
    def _():
        project_into(qc_ref)


def _inproj(h, w_in, *, na_width, pool_width, mem_width, tm=1024, tn=1024):
    n_tok, d = h.shape
    tm = min(tm, n_tok)
    tn = min(tn, pool_width, mem_width)
    assert n_tok % tm == 0 and na_width % tn == 0 and pool_width % tn == 0 and mem_width % tn == 0
    n_qkv = 3 * na_width // tn
    n_u = pool_width // tn
    n_qc = mem_width // tn
    return pl.pallas_call(
        functools.partial(_inproj_body, n_qkv=n_qkv, n_u=n_u),
        grid=(n_tok // tm, n_qkv + n_u + n_qc),
        in_specs=[
            pl.BlockSpec((tm, d), lambda i, j: (i, 0)),
            pl.BlockSpec((d, tn), lambda i, j: (0, j)),
        ],
        out_specs=[
            pl.BlockSpec((tm, tn), lambda i, j: (i, jnp.minimum(j, n_qkv - 1))),
            pl.BlockSpec((tm, tn), lambda i, j: (i, jnp.clip(j - n_qkv, 0, n_u - 1))),
            pl.BlockSpec((tm, tn), lambda i, j: (i, jnp.clip(j - n_qkv - n_u, 0, n_qc - 1))),
        ],
        out_shape=[
            jax.ShapeDtypeStruct((n_tok, 3 * na_width), BF16),
            jax.ShapeDtypeStruct((n_tok, pool_width), F32),
            jax.ShapeDtypeStruct((n_tok, mem_width), BF16),
        ],
        compiler_params=_params(("parallel", "arbitrary")),
        name="inproj",
    )(h, w_in)


def _natten_bias_tables(rpb):
    n_heads, n_dr, n_dc = rpb.shape
    kh, kw = (n_dr + 1) // 2, (n_dc + 1) // 2
    cols = np.arange(GRID_W)
    col_start = np.clip(cols - kw // 2, 0, GRID_W - kw)
    col_valid = (cols[None, :] >= col_start[:, None]) & (cols[None, :] < col_start[:, None] + kw)
    period = 2 * GRID_W
    lpad = GRID_W - kw
    padded = jnp.pad(rpb.astype(F32) * LOG2_E, ((0, 0), (0, 0), (lpad, period - n_dc - lpad)))
    skew = jnp.broadcast_to(padded[:, :, None, :], (n_heads, n_dr, GRID_W, period))
    skew = skew.reshape(n_heads, n_dr, GRID_W * period)[:, :, :GRID_W * (period - 1)]
    by_col = skew.reshape(n_heads, n_dr, GRID_W, period - 1)[:, :, :, GRID_W - 1:]
    by_col = jnp.where(jnp.asarray(col_valid)[None, None], by_col, MASK_VALUE).transpose(0, 1, 3, 2)
    rpad = NAT_K_ROWS - kh
    wide = jnp.concatenate([by_col, by_col], axis=-1)
    wide = jnp.pad(wide, ((0, 0), (rpad, 2 * NAT_K_ROWS - n_dr - rpad), (0, 0), (0, 0)), constant_values=MASK_VALUE)
    n_sub = NAT_Q_ROWS // NAT_SUB_ROWS
    block = (n_sub, _natten_window_rows(kh) * GRID_W, NAT_SUB_ROWS * GRID_W)
    return pl.pallas_call(
        functools.partial(_natten_bias_body, kh=kh),
        grid=(n_heads,),
        in_specs=[pl.BlockSpec((1,) + wide.shape[1:], lambda h: (h, 0, 0, 0))],
        out_specs=pl.BlockSpec((3, 1) + block, lambda h: (0, h, 0, 0, 0)),
        out_shape=jax.ShapeDtypeStruct((3, n_heads) + block, F32),
        compiler_params=_params(("parallel",)),
        name="natten_bias",
    )(wide)


def _natten_row_start(r, kh):
    return min(max(r - kh // 2, 0), NAT_K_ROWS - kh)


def _natten_window_rows(kh):
    return NAT_SUB_ROWS - 1 + kh


def _natten_window_start(t, sub, kh):
    r_first = (NAT_K_ROWS - NAT_Q_ROWS) // 2 * t + sub * NAT_SUB_ROWS
    return min(_natten_row_start(r_first, kh), NAT_K_ROWS - _natten_window_rows(kh))


def _natten_bias_body(wide_ref, o_ref, *, kh):
    low_half = lax.broadcasted_iota(jnp.int32, (GRID_W, 2 * GRID_W), 1) < GRID_W
    masked = jnp.full((GRID_W, 2 * GRID_W), MASK_VALUE, F32)
    for t in range(3):
        for sub in range(NAT_Q_ROWS // NAT_SUB_ROWS):
            first_key_row = _natten_window_start(t, sub, kh)
            for kl in range(_natten_window_rows(kh)):
                rj = first_key_row + kl
                for pair in range(NAT_SUB_ROWS // 2):
                    halves = []
                    for ri in (sub * NAT_SUB_ROWS + 2 * pair, sub * NAT_SUB_ROWS + 2 * pair + 1):
                        r = (NAT_K_ROWS - NAT_Q_ROWS) // 2 * t + ri
                        valid = _natten_row_start(r, kh) <= rj < _natten_row_start(r, kh) + kh
                        halves.append(wide_ref[0, rj - r + NAT_K_ROWS - 1] if valid else masked)
                    o_ref[t, 0, sub, pl.ds(kl * GRID_W, GRID_W), pl.ds(pair * 2 * GRID_W, 2 * GRID_W)] = (
                        jnp.where(low_half, halves[0], halves[1]))


def _natten_body(q_ref, k0_ref, k1_ref, k2_ref, k3_ref, v0_ref, v1_ref, v2_ref, v3_ref, bias_ref, o_ref,
                 kbuf, vbuf, *, log2_scale, head_dim, kh):
    r = pl.program_id(2)
    table = jnp.where(r == 0, 0, jnp.where(r == pl.num_programs(2) - 1, 2, 1))
    chunk = k0_ref.shape[1]
    for c, (k_ref, v_ref) in enumerate(((k0_ref, v0_ref), (k1_ref, v1_ref), (k2_ref, v2_ref), (k3_ref, v3_ref))):
        kbuf[pl.ds(c * chunk, chunk), :] = k_ref[0]
        vbuf[pl.ds(c * chunk, chunk), :] = v_ref[0]
    n_win = _natten_window_rows(kh) * GRID_W
    n_q = NAT_SUB_ROWS * GRID_W
    blocks = []
    for sub in range(NAT_Q_ROWS // NAT_SUB_ROWS):
        starts = [_natten_window_start(t, sub, kh) * GRID_W for t in range(3)]
        start = jnp.where(table == 0, starts[0], jnp.where(table == 1, starts[1], starts[2]))
        keys = pl.ds(pl.multiple_of(start, GRID_W), n_win)
        for h in range(q_ref.shape[2] // head_dim):
            blocks.append((sub, h, keys, pl.ds(sub * n_q, n_q), slice(h * head_dim, (h + 1) * head_dim)))
    scores = [lax.dot_general(kbuf[keys, cols], q_ref[0, queries, cols], (((1,), (1,)), ((), ())),
                              preferred_element_type=F32)
              for _, _, keys, queries, cols in blocks]
    probs = []
    for (sub, h, _, _, _), s in zip(blocks, scores):
        s = s * log2_scale + bias_ref[0, h, sub]
        p = jnp.exp2(s - jnp.max(s, axis=0, keepdims=True))
        probs.append((p.astype(BF16), jnp.sum(p, axis=0, keepdims=True)))
    for (_, _, keys, queries, cols), (p, l) in zip(blocks, probs):
        o = lax.dot_general(vbuf[keys, cols], p, (((0,), (0,)), ((), ())), preferred_element_type=F32)
        o_ref[0, queries, cols] = (o / l).T.astype(o_ref.dtype)


def _natten(qkv, bias, *, n_heads, head_dim, kh, heads_per_step=8):
    b, seq, _ = qkv.shape
    tq = NAT_Q_ROWS * GRID_W
    tk = NAT_K_CHUNK_ROWS * GRID_W
    n_kc = NAT_K_ROWS // NAT_K_CHUNK_ROWS
    hp = min(heads_per_step, n_heads)
    assert n_kc == 4 and seq % (NAT_K_ROWS * GRID_W) == 0 and n_heads % hp == 0
    n_hg = n_heads // hp
    n_qb = seq // tq
    n_kb = seq // tk
    lead = (NAT_K_ROWS - NAT_Q_ROWS) // 2 // NAT_K_CHUNK_ROWS

    def kv_spec(which, c):
        def index(bi, h, r):
            start = jnp.clip(r * (tq // tk) - lead, 0, n_kb - n_kc)
            return (bi, start + c, which * n_hg + h)
        return pl.BlockSpec((1, tk, hp * head_dim), index)

    def bias_index(bi, h, r):
        t = jnp.where(r == 0, 0, jnp.where(r == n_qb - 1, 2, 1))
        return (t, h, 0, 0, 0)

    return pl.pallas_call(
        functools.partial(_natten_body, log2_scale=head_dim ** -0.5 * LOG2_E, head_dim=head_dim, kh=kh),
        grid=(b, n_hg, n_qb),
        in_specs=[pl.BlockSpec((1, tq, hp * head_dim), lambda bi, h, r: (bi, r, h))]
        + [kv_spec(1, c) for c in range(n_kc)]
        + [kv_spec(2, c) for c in range(n_kc)]
        + [pl.BlockSpec((1, hp) + bias.shape[2:], bias_index)],
        out_specs=pl.BlockSpec((1, tq, hp * head_dim), lambda bi, h, r: (bi, r, h)),
        out_shape=jax.ShapeDtypeStruct((b, seq, n_heads * head_dim), BF16),
        scratch_shapes=[pltpu.VMEM((NAT_K_ROWS * GRID_W, hp * head_dim), BF16),
                        pltpu.VMEM((NAT_K_ROWS * GRID_W, hp * head_dim), BF16)],
        compiler_params=_params(("parallel", "parallel", "arbitrary")),
        name="natten",
    )(qkv, *([qkv] * (2 * n_kc)), bias)


def _pool_body(u_ref, prev_ref, next_ref, w_ref, scale_ref, o_ref, ext_ref, *, seq, halo, group_dim):
    t = pl.program_id(1)
    tt = u_ref.shape[1]
    u = u_ref[0]
    ext_ref[pl.ds(halo, tt), :] = u
    ext_ref[pl.ds(0, halo), :] = jnp.where(t > 0, prev_ref[0], 0.0)
    ext_ref[pl.ds(halo + tt, halo), :] = jnp.where(t < pl.num_programs(1) - 1, next_ref[0], 0.0)
    pos = t * tt + lax.broadcasted_iota(jnp.int32, (tt, 1), 0)
    for g, w in enumerate(POOL_WINDOWS):
        cols = pl.ds(g * group_dim, group_dim)
        total = ext_ref[pl.ds(halo - w // 2, tt), cols]
        for d in range(-w // 2 + 1, w // 2):
            total = total + ext_ref[pl.ds(halo + d, tt), cols]
        count = jnp.minimum(pos + w // 2, seq) - jnp.maximum(pos - w // 2, 0)
        pooled = total / count.astype(F32) - u[:, g * group_dim:(g + 1) * group_dim]
        mixed = jnp.dot(pooled.astype(BF16), w_ref[g], preferred_element_type=F32)
        o_ref[0, :, cols] = (mixed * scale_ref[:, cols]).astype(o_ref.dtype)


def _pool(u, w_pool, pool_scale, *, tt=512):
    b, seq, width = u.shape
    n_groups, group_dim, _ = w_pool.shape
    assert n_groups == len(POOL_WINDOWS)
    halo = 8
    assert max(POOL_WINDOWS) // 2 <= halo
    tt = min(tt, seq)
    assert seq % tt == 0 and tt % halo == 0
    hb = tt // halo
    return pl.pallas_call(
        functools.partial(_pool_body, seq=seq, halo=halo, group_dim=group_dim),
        grid=(b, seq // tt),
        in_specs=[
            pl.BlockSpec((1, tt, width), lambda bi, t: (bi, t, 0)),
            pl.BlockSpec((1, halo, width), lambda bi, t: (bi, jnp.maximum(t * hb - 1, 0), 0)),
            pl.BlockSpec((1, halo, width), lambda bi, t: (bi, jnp.minimum((t + 1) * hb, seq // halo - 1), 0)),
            pl.BlockSpec((n_groups, group_dim, group_dim), lambda bi, t: (0, 0, 0)),
            pl.BlockSpec((1, width), lambda bi, t: (0, 0)),
        ],
        out_specs=pl.BlockSpec((1, tt, width), lambda bi, t: (bi, t, 0)),
        out_shape=jax.ShapeDtypeStruct((b, seq, width), BF16),
        scratch_shapes=[pltpu.VMEM((tt + 2 * halo, width), F32)],
        compiler_params=_params(("parallel", "arbitrary")),
        name="pool",
    )(u, u, u, w_pool, pool_scale)


def _memkv_body(m_ref, g_ref, w_ref, o_ref):
    h = _rms(m_ref[...], g_ref[...]).astype(BF16)
    o_ref[...] = jnp.dot(h, w_ref[...], preferred_element_type=F32).astype(o_ref.dtype)


def _memkv(mem, g_mem, w_kv, *, tr=256, tn=512):
    n_rows, d = mem.shape
    width = w_kv.shape[1]
    tr = min(tr, n_rows)
    tn = min(tn, width)
    assert n_rows % tr == 0 and width % tn == 0
    return pl.pallas_call(
        _memkv_body,
        grid=(n_rows // tr, width // tn),
        in_specs=[
            pl.BlockSpec((tr, d), lambda i, j: (i, 0)),
            pl.BlockSpec((1, d), lambda i, j: (0, 0)),
            pl.BlockSpec((d, tn), lambda i, j: (0, j)),
        ],
        out_specs=pl.BlockSpec((tr, tn), lambda i, j: (i, j)),
        out_shape=jax.ShapeDtypeStruct((n_rows, width), BF16),
        compiler_params=_params(("parallel", "arbitrary")),
        name="memkv",
    )(mem, g_mem, w_kv)


def _memattn_body(q_ref, kv_ref, o_ref, *, n_heads, head_dim):
    scale = head_dim ** -0.5
    width = n_heads * head_dim
    for h in range(n_heads):
        cols = slice(h * head_dim, (h + 1) * head_dim)
        q = q_ref[0, :, cols]
        k = kv_ref[0, :, cols]
        v = kv_ref[0, :, width + h * head_dim: width + (h + 1) * head_dim]
        s = lax.dot_general(q, k, (((1,), (1,)), ((), ())), preferred_element_type=F32) * scale
        m = jnp.max(s, axis=-1, keepdims=True)
        p = jnp.exp(s - m)
        l = jnp.sum(p, axis=-1, keepdims=True)
        o = jnp.dot(p.astype(BF16), v, preferred_element_type=F32)
        o_ref[0, :, cols] = (o / l).astype(o_ref.dtype)


def _memattn(q, kv, *, n_heads, tq=512):
    b, seq, width = q.shape
    n_mem = kv.shape[1]
    tq = min(tq, seq)
    assert seq % tq == 0
    return pl.pallas_call(
        functools.partial(_memattn_body, n_heads=n_heads, head_dim=width // n_heads),
        grid=(b, seq // tq),
        in_specs=[
            pl.BlockSpec((1, tq, width), lambda bi, t: (bi, t, 0)),
            pl.BlockSpec((1, n_mem, 2 * width), lambda bi, t: (bi, 0, 0)),
        ],
        out_specs=pl.BlockSpec((1, tq, width), lambda bi, t: (bi, t, 0)),
        out_shape=jax.ShapeDtypeStruct((b, seq, width), BF16),
        compiler_params=_params(("parallel", "arbitrary")),
        name="memattn",
    )(q, kv)


def _merge_body(h_ref, a_ref, b_ref, c_ref, wga_ref, wgb_ref, wgc_ref, bg_ref, wa_ref, wb_ref, wc_ref, o_ref):
    h = h_ref[...]
    merged = None
    for i, (x_ref, wg_ref, w_ref) in enumerate(
            ((a_ref, wga_ref, wa_ref), (b_ref, wgb_ref, wb_ref), (c_ref, wgc_ref, wc_ref))):
        logits = jnp.dot(h, wg_ref[...], preferred_element_type=F32) + bg_ref[i:i + 1, :]
        gate = 1.0 / (1.0 + jnp.exp(-logits))
        y = jnp.dot(x_ref[...], w_ref[...], preferred_element_type=F32)
        merged = gate * y if merged is None else merged + gate * y
    o_ref[...] = merged.astype(o_ref.dtype)


def _merge(h, att_a, mix_b, att_c, w_in, gate_col0, b_gate, w_a, w_b, w_c, *, tm=512, tn=512):
    n_tok, d = h.shape
    tm = min(tm, n_tok)
    tn = min(tn, d)
    assert n_tok % tm == 0 and d % tn == 0 and gate_col0 % tn == 0
    nj = d // tn
    g0 = gate_col0 // tn

    def act(x):
        return pl.BlockSpec((tm, x.shape[1]), lambda i, j: (i, 0))

    def gate_w(branch):
        return pl.BlockSpec((d, tn), lambda i, j: (0, g0 + branch * nj + j))

    def out_w(w):
        return pl.BlockSpec((w.shape[0], tn), lambda i, j: (0, j))

    return pl.pallas_call(
        _merge_body,
        grid=(n_tok // tm, nj),
        in_specs=[act(h), act(att_a), act(mix_b), act(att_c),
                  gate_w(0), gate_w(1), gate_w(2),
                  pl.BlockSpec((b_gate.shape[0], tn), lambda i, j: (0, j)),
                  out_w(w_a), out_w(w_b), out_w(w_c)],
        out_specs=pl.BlockSpec((tm, tn), lambda i, j: (i, j)),
        out_shape=jax.ShapeDtypeStruct((n_tok, d), BF16),
        compiler_params=_params(("parallel", "arbitrary")),
        name="merge",
    )(h, att_a, mix_b, att_c, w_in, w_in, w_in, b_gate, w_a, w_b, w_c)


def _wo_body(x_ref, m_ref, w_ref, g_ref, o_ref, *, row_chunk):
    j = pl.program_id(1)
    tm = x_ref.shape[0]

    @pl.when(j == 0)
    def _():
        def prologue(rows):
            o_ref[rows, :] = jnp.zeros((row_chunk, o_ref.shape[1]), F32)
        _for_row_chunks(tm, row_chunk, prologue)

    o_ref[...] += jnp.dot(m_ref[...], w_ref[...], preferred_element_type=F32)

    @pl.when(j == pl.num_programs(1) - 1)
    def _():
        def epilogue(rows):
            o_ref[rows, :] = x_ref[rows, :] + _rms(o_ref[rows, :], g_ref[...])
        _for_row_chunks(tm, row_chunk, epilogue)


def _wo(x, merged, w_o, g_post, *, tm=512, tk=1024):
    n_tok, d = x.shape
    tm = min(tm, n_tok)
    tk = min(tk, d)
    assert n_tok % tm == 0 and d % tk == 0
    return pl.pallas_call(
        functools.partial(_wo_body, row_chunk=min(64, tm)),
        grid=(n_tok // tm, d // tk),
        in_specs=[
            pl.BlockSpec((tm, d), lambda i, j: (i, 0)),
            pl.BlockSpec((tm, tk), lambda i, j: (i, j)),
            pl.BlockSpec((tk, d), lambda i, j: (j, 0)),
            pl.BlockSpec((1, d), lambda i, j: (0, 0)),
        ],
        out_specs=pl.BlockSpec((tm, d), lambda i, j: (i, 0)),
        out_shape=jax.ShapeDtypeStruct((n_tok, d), F32),
        compiler_params=_params(("parallel", "arbitrary")),
        name="wo",
    )(x, merged, w_o, g_post)


def _mixer(x1, h, mem, p, bias_tables, *, b, seq):
    n_tok, d = x1.shape
    na_heads = p["rpb"].shape[0]
    n_groups, group_dim, _ = p["w_pool"].shape
    pool_width = n_groups * group_dim
    mem_width = p["w_mem_kv"].shape[1] // 2
    na_width = p["w_a_out"].shape[0]
    gate_col0 = 3 * na_width + pool_width + mem_width

    qkv, u, qc = _inproj(h, p["w_in"], na_width=na_width, pool_width=pool_width, mem_width=mem_width)
    att_a = _natten(qkv.reshape(b, seq, 3 * na_width), bias_tables,
                    n_heads=na_heads, head_dim=na_width // na_heads, kh=(p["rpb"].shape[1] + 1) // 2)
    mix_b = _pool(u.reshape(b, seq, pool_width), p["w_pool"], p["pool_scale"])
    kv = _memkv(mem.reshape(b * mem.shape[1], d), p["g_mem"], p["w_mem_kv"])
    att_c = _memattn(qc.reshape(b, seq, mem_width), kv.reshape(b, mem.shape[1], 2 * mem_width),
                     n_heads=MEM_HEADS)
    merged = _merge(h, att_a.reshape(n_tok, na_width), mix_b.reshape(n_tok, pool_width),
                    att_c.reshape(n_tok, mem_width), p["w_in"], gate_col0, p["b_gate"],
                    p["w_a_out"], p["w_b_out"], p["w_c_out"])
    return _wo(x1, merged, p["w_o"], p["g_mix_post"])


_MATMUL_WEIGHTS = ("w1_gate", "w1_up", "w1_down", "w_in", "w_pool", "w_mem_kv",
                   "w_a_out", "w_b_out", "w_c_out", "w_o", "w2_gate", "w2_up", "w2_down")
_ROW_VECTORS = ("g_ffn1_pre", "g_ffn1_post", "g_mix_pre", "pool_scale", "g_mem", "g_mix_post",
                "g_ffn2_pre", "g_ffn2_post", "g_final")


def kernel(x_prompt, x_sample, mem_prompt, mem_sample, g_ffn1_pre, w1_gate, w1_up, w1_down, g_ffn1_post, g_mix_pre, w_in, rpb, w_pool, pool_scale, g_mem, w_mem_kv, w_a_out, w_b_out, w_c_out, b_gate, w_o, g_mix_post, g_ffn2_pre, w2_gate, w2_up, w2_down, g_ffn2_post, g_final):
    stacked = dict(g_ffn1_pre=g_ffn1_pre, w1_gate=w1_gate, w1_up=w1_up, w1_down=w1_down,
                   g_ffn1_post=g_ffn1_post, g_mix_pre=g_mix_pre, w_in=w_in, rpb=rpb, w_pool=w_pool,
                   pool_scale=pool_scale, g_mem=g_mem, w_mem_kv=w_mem_kv, w_a_out=w_a_out,
                   w_b_out=w_b_out, w_c_out=w_c_out, b_gate=b_gate, w_o=w_o, g_mix_post=g_mix_post,
                   g_ffn2_pre=g_ffn2_pre, w2_gate=w2_gate, w2_up=w2_up, w2_down=w2_down,
                   g_ffn2_post=g_ffn2_post, g_final=g_final)
    y_prompt, y_sample = x_prompt, x_sample
    for layer in range(g_final.shape[0]):
        p = {name: a[layer] for name, a in stacked.items()}
        for name in _MATMUL_WEIGHTS:
            p[name] = p[name].astype(BF16)
        for name in _ROW_VECTORS:
            p[name] = p[name].reshape(1, -1)
        bias_tables = _natten_bias_tables(p["rpb"])
        (bp, sp, d), (bs, ss, _) = y_prompt.shape, y_sample.shape
        x1p, x1s, hp, hs = _ffn_pair(
            y_prompt.reshape(bp * sp, d), y_sample.reshape(bs * ss, d),
            p["g_ffn1_pre"], p["w1_gate"], p["w1_up"], p["w1_down"], p["g_ffn1_post"], p["g_final"],
            p["g_mix_pre"], final_norm=False, emit_h=True)
        x2p = _mixer(x1p, hp, mem_prompt, p, bias_tables, b=bp, seq=sp)
        x2s = _mixer(x1s, hs, mem_sample, p, bias_tables, b=bs, seq=ss)
        yp, ys = _ffn_pair(
            x2p, x2s, p["g_ffn2_pre"], p["w2_gate"], p["w2_up"], p["w2_down"], p["g_ffn2_post"], p["g_final"],
            p["g_final"], final_norm=True, emit_h=False)
        y_prompt, y_sample = yp.reshape(bp, sp, d), ys.reshape(bs, ss, d)
    return (y_prompt, y_sample)
```

```python
import functools

import numpy as np
import jax
import jax.numpy as jnp
from jax import lax
from jax.experimental import pallas as pl
from jax.experimental.pallas import tpu as pltpu

GRID_W = 64
POOL_WINDOWS = (2, 4, 8, 16)
MEM_HEADS = 4
RMS_EPS = 1e-6
MASK_VALUE = -1e30
LOG2_E = 1.4426950408889634

FFN_CHUNK = 256
NAT_Q_ROWS = 8
NAT_K_ROWS = 16
NAT_K_CHUNK_ROWS = 4
NAT_SUB_ROWS = 4

MXU_COLS = 256
V7X_VMEM_BYTES = 64 * 1024 * 1024
VMEM_LIMIT = V7X_VMEM_BYTES - 6 * 1024 * 1024

F32 = jnp.float32
BF16 = jnp.bfloat16


def _rms(x, g):
    ms = jnp.mean(x * x, axis=-1, keepdims=True)
    return (x * lax.rsqrt(ms + RMS_EPS)) * g


def _silu_mul(g, u):
    return (g * (1.0 / (1.0 + jnp.exp(-g)))) * u


def _for_row_chunks(n_rows, chunk, fn):
    def body(c, carry):
        fn(pl.ds(pl.multiple_of(c * chunk, chunk), chunk))
        return carry
    lax.fori_loop(0, n_rows // chunk, body, 0)


def _params(sem):
    return pltpu.CompilerParams(dimension_semantics=sem, vmem_limit_bytes=VMEM_LIMIT)


def _ffn_pair_body(*refs, tm, tf, rc, edge_rows, n_a, n_tiles, n_ff, final_norm, emit_h):
    (xa_hbm, xb_hbm, gpre_ref, wg_hbm, wu_hbm, wd_hbm, gpost_ref, gfin_ref, gnext_ref), refs = refs[:9], refs[9:]
    if emit_h:
        (oa_hbm, ob_hbm, na_hbm, nb_hbm), refs = refs[:4], refs[4:]
    else:
        (oa_hbm, ob_hbm), refs = refs[:2], refs[2:]
        na_hbm = nb_hbm = None
    (hbuf, accbuf, gbuf, ubuf, abuf, wgbuf, wubuf, wdbuf, xpbuf, xebuf, obuf, nbuf, xedge, oedge, nedge,
     wsem, xpsem, xesem, osem, nsem, edge_in_sem, edge_out_sem, edge_next_sem) = refs

    n_chunks = tm // rc
    n_steps = n_tiles * n_ff
    n_edge = tm // edge_rows

    def step_after(tj):
        t, j = tj
        wrap = j == n_ff - 1
        return jnp.where(wrap, t + 1, t), jnp.where(wrap, 0, j + 1)

    def step_before(tj):
        t, j = tj
        wrap = j == 0
        return jnp.where(wrap, t - 1, t), jnp.where(wrap, n_ff - 1, j - 1)

    def coords(s):
        return jnp.int32(s // n_ff), jnp.int32(s % n_ff)

    def up_weight_copies(j_up, slot):
        cols = pl.ds(pl.multiple_of(j_up * tf, tf), tf)
        return [pltpu.make_async_copy(wg_hbm.at[:, cols], wgbuf.at[slot], wsem.at[slot, 0]),
                pltpu.make_async_copy(wu_hbm.at[:, cols], wubuf.at[slot], wsem.at[slot, 1])]

    def weight_copies(j_down, j_up, slot):
        rows = pl.ds(pl.multiple_of(j_down * tf, tf), tf)
        return up_weight_copies(j_up, slot) + [
            pltpu.make_async_copy(wd_hbm.at[rows, :], wdbuf.at[slot], wsem.at[slot, 2])]

    def start_rows_in(row0, n_rows, dst, sem):
        row0 = jnp.asarray(row0, jnp.int32)

        @pl.when(row0 < n_a)
        def _():
            pltpu.make_async_copy(xa_hbm.at[pl.ds(pl.multiple_of(row0, n_rows), n_rows), :], dst, sem).start()

        @pl.when(row0 >= n_a)
        def _():
            pltpu.make_async_copy(xb_hbm.at[pl.ds(pl.multiple_of(row0 - n_a, n_rows), n_rows), :], dst, sem).start()

    def wait_rows_in(n_rows, dst, sem):
        pltpu.make_async_copy(xa_hbm.at[pl.ds(0, n_rows), :], dst, sem).wait()

    def start_rows_out(row0, n_rows, src, sem, dst_a, dst_b):
        row0 = jnp.asarray(row0, jnp.int32)

        @pl.when(row0 < n_a)
        def _():
            pltpu.make_async_copy(src, dst_a.at[pl.ds(pl.multiple_of(row0, n_rows), n_rows), :], sem).start()

        @pl.when(row0 >= n_a)
        def _():
            pltpu.make_async_copy(src, dst_b.at[pl.ds(pl.multiple_of(row0 - n_a, n_rows), n_rows), :], sem).start()

    def wait_rows_out(n_rows, src, sem, dst_a):
        pltpu.make_async_copy(src, dst_a.at[pl.ds(0, n_rows), :], sem).wait()

    def start_results_out(row0, n_rows, out_src, out_sem, next_src, next_sem):
        start_rows_out(row0, n_rows, out_src, out_sem, oa_hbm, ob_hbm)
        if emit_h:
            start_rows_out(row0, n_rows, next_src, next_sem, na_hbm, nb_hbm)

    def wait_results_out(n_rows, out_src, out_sem, next_src, next_sem):
        wait_rows_out(n_rows, out_src, out_sem, oa_hbm)
        if emit_h:
            wait_rows_out(n_rows, next_src, next_sem, na_hbm)

    def pre_chunk(j):
        return jnp.minimum(j, n_chunks - 1)

    def epi_chunk(j):
        return jnp.maximum(j - (n_ff - n_chunks), 0)

    def stage_inputs(tj):
        t, j = tj
        j_up = jnp.where(j == n_ff - 1, 0, j + 1)
        row_pre = jnp.minimum(t + 1, n_tiles - 1) * tm + pre_chunk(j) * rc
        row_epi = jnp.clip(t - 1, 0, n_tiles - 1) * tm + epi_chunk(j) * rc
        return j, j_up, row_pre, row_epi

    def start_inputs(tj, slot):
        j_down, j_up, row_pre, row_epi = stage_inputs(tj)
        for c in weight_copies(j_down, j_up, slot):
            c.start()
        start_rows_in(row_pre, rc, xpbuf.at[slot], xpsem.at[slot])
        start_rows_in(row_epi, rc, xebuf.at[slot], xesem.at[slot])

    def wait_inputs(slot):
        for c in weight_copies(0, 0, slot):
            c.wait()
        wait_rows_in(rc, xpbuf.at[slot], xpsem.at[slot])
        wait_rows_in(rc, xebuf.at[slot], xesem.at[slot])

    def finishes_rows(tj):
        t, j = tj
        return jnp.logical_and(t >= 1, j >= n_ff - n_chunks)

    def finish_rows(acc_rows, x_rows, out_ref, next_ref):
        out = x_rows + 0.5 * _rms(acc_rows, gpost_ref[...])
        if final_norm:
            out = _rms(out, gfin_ref[...])
        out_ref[...] = out
        if emit_h:
            next_ref[...] = _rms(out, gnext_ref[...]).astype(BF16)

    def up_chunk(h_sel, slot):
        h = hbuf[h_sel]
        gbuf[...] = jnp.dot(h, wgbuf[slot], preferred_element_type=F32)
        ubuf[...] = jnp.dot(h, wubuf[slot], preferred_element_type=F32)

    start_rows_in(0, edge_rows, xedge.at[0], edge_in_sem.at[0])
    for c in range(n_edge):
        wait_rows_in(edge_rows, xedge.at[c % 2], edge_in_sem.at[c % 2])
        if c + 1 < n_edge:
            start_rows_in((c + 1) * edge_rows, edge_rows, xedge.at[(c + 1) % 2], edge_in_sem.at[(c + 1) % 2])
        hbuf[0, pl.ds(c * edge_rows, edge_rows), :] = _rms(xedge[c % 2], gpre_ref[...]).astype(BF16)
        accbuf[1, pl.ds(c * edge_rows, edge_rows), :] = jnp.zeros((edge_rows, accbuf.shape[2]), F32)
    for c in up_weight_copies(0, 1):
        c.start()
    for c in up_weight_copies(0, 1):
        c.wait()
    up_chunk(0, 1)
    start_inputs(coords(0), 0)
    start_inputs(coords(1), 1)
    wait_inputs(0)

    def step(_, carry):
        t, j, slot = carry
        tj = (t, j)
        cur = t % 2
        other = 1 - cur

        @pl.when(j == 0)
        def _():
            def zero(zrows):
                accbuf[cur, zrows, :] = jnp.zeros((edge_rows, accbuf.shape[2]), F32)
            _for_row_chunks(tm, edge_rows, zero)

        abuf[...] = _silu_mul(gbuf[...], ubuf[...]).astype(BF16)
        pre_rows = pl.ds(pl.multiple_of(pre_chunk(j) * rc, rc), rc)
        epi_rows = pl.ds(pl.multiple_of(epi_chunk(j) * rc, rc), rc)
        hbuf[other, pre_rows, :] = _rms(xpbuf[slot], gpre_ref[...]).astype(BF16)
        finish_rows(accbuf[other, epi_rows, :], xebuf[slot], obuf.at[slot], nbuf.at[slot])
        accbuf[cur] += jnp.dot(abuf[...], wdbuf[slot], preferred_element_type=F32)
        up_chunk(jnp.where(j == n_ff - 1, other, cur), slot)

        after = step_after(tj)
        start_inputs(step_after(after), slot)

        @pl.when(finishes_rows(step_before(tj)))
        def _():
            wait_results_out(rc, obuf.at[1 - slot], osem.at[1 - slot], nbuf.at[1 - slot], nsem.at[1 - slot])

        @pl.when(finishes_rows(tj))
        def _():
            start_results_out(stage_inputs(tj)[3], rc, obuf.at[slot], osem.at[slot], nbuf.at[slot], nsem.at[slot])

        wait_inputs(1 - slot)
        return after + (1 - slot,)

    lax.fori_loop(0, n_steps, step, coords(0) + (jnp.int32(0),))

    wait_inputs(1 - n_steps % 2)

    @pl.when(finishes_rows(coords(n_steps - 1)))
    def _():
        last = (n_steps - 1) % 2
        wait_results_out(rc, obuf.at[last], osem.at[last], nbuf.at[last], nsem.at[last])

    last_row0 = (n_tiles - 1) * tm

    def edge_bufs(c):
        return oedge.at[c % 2], edge_out_sem.at[c % 2], nedge.at[c % 2], edge_next_sem.at[c % 2]

    start_rows_in(last_row0, edge_rows, xedge.at[0], edge_in_sem.at[0])
    for c in range(n_edge):
        wait_rows_in(edge_rows, xedge.at[c % 2], edge_in_sem.at[c % 2])
        if c + 1 < n_edge:
            start_rows_in(last_row0 + (c + 1) * edge_rows, edge_rows, xedge.at[(c + 1) % 2],
                          edge_in_sem.at[(c + 1) % 2])
        if c >= 2:
            wait_results_out(edge_rows, *edge_bufs(c))
        finish_rows(accbuf[(n_tiles - 1) % 2, pl.ds(c * edge_rows, edge_rows), :], xedge[c % 2],
                    oedge.at[c % 2], nedge.at[c % 2])
        start_results_out(last_row0 + c * edge_rows, edge_rows, *edge_bufs(c))
    for c in range(max(n_edge - 2, 0), n_edge):
        wait_results_out(edge_rows, *edge_bufs(c))


def _ffn_pair(xa, xb, g_pre, wg, wu, wd, g_post, g_fin, g_next, *, final_norm, emit_h,
              tm=768, tf=FFN_CHUNK, rc=32, edge_rows=32):
    (n_a, d), n_b = xa.shape, xb.shape[0]
    d_ff = wg.shape[1]
    n_tok = n_a + n_b
    tm = min(tm, n_tok // 2)
    tf = min(tf, d_ff)
    edge_rows = min(edge_rows, tm)
    n_tiles, n_ff = n_tok // tm, d_ff // tf
    assert n_tok % tm == 0 and d_ff % tf == 0 and tm % rc == 0 and tm % edge_rows == 0
    assert n_a % edge_rows == 0 and n_a % rc == 0 and tm // rc <= n_ff and tm // edge_rows >= 2 and n_tiles >= 2
    hbm = pl.BlockSpec(memory_space=pl.ANY)
    vec = pl.BlockSpec(memory_space=pltpu.VMEM)
    out_shape = [jax.ShapeDtypeStruct((n_a, d), F32), jax.ShapeDtypeStruct((n_b, d), F32)]
    if emit_h:
        out_shape += [jax.ShapeDtypeStruct((n_a, d), BF16), jax.ShapeDtypeStruct((n_b, d), BF16)]
    next_rows = (rc, edge_rows) if emit_h else (8, 8)
    return pl.pallas_call(
        functools.partial(_ffn_pair_body, tm=tm, tf=tf, rc=rc, edge_rows=edge_rows, n_a=n_a, n_tiles=n_tiles,
                          n_ff=n_ff, final_norm=final_norm, emit_h=emit_h),
        in_specs=[hbm, hbm, vec, hbm, hbm, hbm, vec, vec, vec],
        out_specs=[hbm] * len(out_shape),
        out_shape=out_shape,
        scratch_shapes=[
            pltpu.VMEM((2, tm, d), BF16),
            pltpu.VMEM((2, tm, d), F32),
            pltpu.VMEM((tm, tf), F32),
            pltpu.VMEM((tm, tf), F32),
            pltpu.VMEM((tm, tf), BF16),
            pltpu.VMEM((2, d, tf), BF16),
            pltpu.VMEM((2, d, tf), BF16),
            pltpu.VMEM((2, tf, d), BF16),
            pltpu.VMEM((2, rc, d), F32),
            pltpu.VMEM((2, rc, d), F32),
            pltpu.VMEM((2, rc, d), F32),
            pltpu.VMEM((2, next_rows[0], d), BF16),
            pltpu.VMEM((2, edge_rows, d), F32),
            pltpu.VMEM((2, edge_rows, d), F32),
            pltpu.VMEM((2, next_rows[1], d), BF16),
            pltpu.SemaphoreType.DMA((2, 3)),
            pltpu.SemaphoreType.DMA((2,)),
            pltpu.SemaphoreType.DMA((2,)),
            pltpu.SemaphoreType.DMA((2,)),
            pltpu.SemaphoreType.DMA((2,)),
            pltpu.SemaphoreType.DMA((2,)),
            pltpu.SemaphoreType.DMA((2,)),
            pltpu.SemaphoreType.DMA((2,)),
        ],
        compiler_params=pltpu.CompilerParams(vmem_limit_bytes=VMEM_LIMIT),
        name="ffn_pair",
    )(xa, xb, g_pre, wg, wu, wd, g_post, g_fin, g_next)


def _inproj_body(h_ref, w_ref, qkv_ref, u_ref, qc_ref, *, n_qkv, n_u):
    j = pl.program_id(1)

    def project_into(o_ref):
        for c in range(0, w_ref.shape[1], MXU_COLS):
            z = jnp.dot(h_ref[...], w_ref[:, c:c + MXU_COLS], preferred_element_type=F32)
            o_ref[:, c:c + MXU_COLS] = z.astype(o_ref.dtype)

    @pl.when(j < n_qkv)
    def _():
        project_into(qkv_ref)

    @pl.when(jnp.logical_and(j >= n_qkv, j < n_qkv + n_u))
    def _():
        project_into(u_ref)

    @pl.when(j >= n_qkv + n_u)
    def _():
        project_into(qc_ref)


def _inproj(h, w_in, *, na_width, pool_width, mem_width, tm=1024, tn=1024):
    n_tok, d = h.shape
    tm = min(tm, n_tok)
    tn = min(tn, pool_width, mem_width)
    assert n_tok % tm == 0 and na_width % tn == 0 and pool_width % tn == 0 and mem_width % tn == 0
    n_qkv = 3 * na_width // tn
    n_u = pool_width // tn
    n_qc = mem_width // tn
    return pl.pallas_call(
        functools.partial(_inproj_body, n_qkv=n_qkv, n_u=n_u),
        grid=(n_tok // tm, n_qkv + n_u + n_qc),
        in_specs=[
            pl.BlockSpec((tm, d), lambda i, j: (i, 0)),
            pl.BlockSpec((d, tn), lambda i, j: (0, j)),
        ],
        out_specs=[
            pl.BlockSpec((tm, tn), lambda i, j: (i, jnp.minimum(j, n_qkv - 1))),
            pl.BlockSpec((tm, tn), lambda i, j: (i, jnp.clip(j - n_qkv, 0, n_u - 1))),
            pl.BlockSpec((tm, tn), lambda i, j: (i, jnp.clip(j - n_qkv - n_u, 0, n_qc - 1))),
        ],
        out_shape=[
            jax.ShapeDtypeStruct((n_tok, 3 * na_width), BF16),
            jax.ShapeDtypeStruct((n_tok, pool_width), F32),
            jax.ShapeDtypeStruct((n_tok, mem_width), BF16),
        ],
        compiler_params=_params(("parallel", "arbitrary")),
        name="inproj",
    )(h, w_in)


def _natten_bias_tables(rpb):
    n_heads, n_dr, n_dc = rpb.shape
    kh, kw = (n_dr + 1) // 2, (n_dc + 1) // 2
    cols = np.arange(GRID_W)
    col_start = np.clip(cols - kw // 2, 0, GRID_W - kw)
    col_valid = (cols[None, :] >= col_start[:, None]) & (cols[None, :] < col_start[:, None] + kw)
    period = 2 * GRID_W
    lpad = GRID_W - kw
    padded = jnp.pad(rpb.astype(F32) * LOG2_E, ((0, 0), (0, 0), (lpad, period - n_dc - lpad)))
    skew = jnp.broadcast_to(padded[:, :, None, :], (n_heads, n_dr, GRID_W, period))
    skew = skew.reshape(n_heads, n_dr, GRID_W * period)[:, :, :GRID_W * (period - 1)]
    by_col = skew.reshape(n_heads, n_dr, GRID_W, period - 1)[:, :, :, GRID_W - 1:]
    by_col = jnp.where(jnp.asarray(col_valid)[None, None], by_col, MASK_VALUE).transpose(0, 1, 3, 2)
    rpad = NAT_K_ROWS - kh
    wide = jnp.concatenate([by_col, by_col], axis=-1)
    wide = jnp.pad(wide, ((0, 0), (rpad, 2 * NAT_K_ROWS - n_dr - rpad), (0, 0), (0, 0)), constant_values=MASK_VALUE)
    n_sub = NAT_Q_ROWS // NAT_SUB_ROWS
    block = (n_sub, _natten_window_rows(kh) * GRID_W, NAT_SUB_ROWS * GRID_W)
    return pl.pallas_call(
        functools.partial(_natten_bias_body, kh=kh),
        grid=(n_heads,),
        in_specs=[pl.BlockSpec((1,) + wide.shape[1:], lambda h: (h, 0, 0, 0))],
        out_specs=pl.BlockSpec((3, 1) + block, lambda h: (0, h, 0, 0, 0)),
        out_shape=jax.ShapeDtypeStruct((3, n_heads) + block, F32),
        compiler_params=_params(("parallel",)),
        name="natten_bias",
    )(wide)


def _natten_row_start(r, kh):
    return min(max(r - kh // 2, 0), NAT_K_ROWS - kh)


def _natten_window_rows(kh):
    return NAT_SUB_ROWS - 1 + kh


def _natten_window_start(t, sub, kh):
    r_first = (NAT_K_ROWS - NAT_Q_ROWS) // 2 * t + sub * NAT_SUB_ROWS
    return min(_natten_row_start(r_first, kh), NAT_K_ROWS - _natten_window_rows(kh))


def _natten_bias_body(wide_ref, o_ref, *, kh):
    low_half = lax.broadcasted_iota(jnp.int32, (GRID_W, 2 * GRID_W), 1) < GRID_W
    masked = jnp.full((GRID_W, 2 * GRID_W), MASK_VALUE, F32)
    for t in range(3):
        for sub in range(NAT_Q_ROWS // NAT_SUB_ROWS):
            first_key_row = _natten_window_start(t, sub, kh)
            for kl in range(_natten_window_rows(kh)):
                rj = first_key_row + kl
                for pair in range(NAT_SUB_ROWS // 2):
                    halves = []
                    for ri in (sub * NAT_SUB_ROWS + 2 * pair, sub * NAT_SUB_ROWS + 2 * pair + 1):
                        r = (NAT_K_ROWS - NAT_Q_ROWS) // 2 * t + ri
                        valid = _natten_row_start(r, kh) <= rj < _natten_row_start(r, kh) + kh
                        halves.append(wide_ref[0, rj - r + NAT_K_ROWS - 1] if valid else masked)
                    o_ref[t, 0, sub, pl.ds(kl * GRID_W, GRID_W), pl.ds(pair * 2 * GRID_W, 2 * GRID_W)] = (
                        jnp.where(low_half, halves[0], halves[1]))


def _natten_body(q_ref, k0_ref, k1_ref, k2_ref, k3_ref, v0_ref, v1_ref, v2_ref, v3_ref, bias_ref, o_ref,
                 kbuf, vbuf, *, log2_scale, head_dim, kh):
    r = pl.program_id(2)
    table = jnp.where(r == 0, 0, jnp.where(r == pl.num_programs(2) - 1, 2, 1))
    chunk = k0_ref.shape[1]
    for c, (k_ref, v_ref) in enumerate(((k0_ref, v0_ref), (k1_ref, v1_ref), (k2_ref, v2_ref), (k3_ref, v3_ref))):
        kbuf[pl.ds(c * chunk, chunk), :] = k_ref[0]
        vbuf[pl.ds(c * chunk, chunk), :] = v_ref[0]
    n_win = _natten_window_rows(kh) * GRID_W
    n_q = NAT_SUB_ROWS * GRID_W
    blocks = []
    for sub in range(NAT_Q_ROWS // NAT_SUB_ROWS):
        starts = [_natten_window_start(t, sub, kh) * GRID_W for t in range(3)]
        start = jnp.where(table == 0, starts[0], jnp.where(table == 1, starts[1], starts[2]))
        keys = pl.ds(pl.multiple_of(start, GRID_W), n_win)
        for h in range(q_ref.shape[2] // head_dim):
            blocks.append((sub, h, keys, pl.ds(sub * n_q, n_q), slice(h * head_dim, (h + 1) * head_dim)))
    scores = [lax.dot_general(kbuf[keys, cols], q_ref[0, queries, cols], (((1,), (1,)), ((), ())),
                              preferred_element_type=F32)
              for _, _, keys, queries, cols in blocks]
    probs = []
    for (sub, h, _, _, _), s in zip(blocks, scores):
        s = s * log2_scale + bias_ref[0, h, sub]
        p = jnp.exp2(s - jnp.max(s, axis=0, keepdims=True))
        probs.append((p.astype(BF16), jnp.sum(p, axis=0, keepdims=True)))
    for (_, _, keys, queries, cols), (p, l) in zip(blocks, probs):
        o = lax.dot_general(vbuf[keys, cols], p, (((0,), (0,)), ((), ())), preferred_element_type=F32)
        o_ref[0, queries, cols] = (o / l).T.astype(o_ref.dtype)


def _natten(qkv, bias, *, n_heads, head_dim, kh, heads_per_step=8):
    b, seq, _ = qkv.shape
    tq = NAT_Q_ROWS * GRID_W
    tk = NAT_K_CHUNK_ROWS * GRID_W
    n_kc = NAT_K_ROWS // NAT_K_CHUNK_ROWS
    hp = min(heads_per_step, n_heads)
    assert n_kc == 4 and seq % (NAT_K_ROWS * GRID_W) == 0 and n_heads % hp == 0
    n_hg = n_heads // hp
    n_qb = seq // tq
    n_kb = seq // tk
    lead = (NAT_K_ROWS - NAT_Q_ROWS) // 2 // NAT_K_CHUNK_ROWS

    def kv_spec(which, c):
        def index(bi, h, r):
            start = jnp.clip(r * (tq // tk) - lead, 0, n_kb - n_kc)
            return (bi, start + c, which * n_hg + h)
        return pl.BlockSpec((1, tk, hp * head_dim), index)

    def bias_index(bi, h, r):
        t = jnp.where(r == 0, 0, jnp.where(r == n_qb - 1, 2, 1))
        return (t, h, 0, 0, 0)

    return pl.pallas_call(
        functools.partial(_natten_body, log2_scale=head_dim ** -0.5 * LOG2_E, head_dim=head_dim, kh=kh),
        grid=(b, n_hg, n_qb),
        in_specs=[pl.BlockSpec((1, tq, hp * head_dim), lambda bi, h, r: (bi, r, h))]
        + [kv_spec(1, c) for c in range(n_kc)]
        + [kv_spec(2, c) for c in range(n_kc)]
        + [pl.BlockSpec((1, hp) + bias.shape[2:], bias_index)],
        out_specs=pl.BlockSpec((1, tq, hp * head_dim), lambda bi, h, r: (bi, r, h)),
        out_shape=jax.ShapeDtypeStruct((b, seq, n_heads * head_dim), BF16),
        scratch_shapes=[pltpu.VMEM((NAT_K_ROWS * GRID_W, hp * head_dim), BF16),
                        pltpu.VMEM((NAT_K_ROWS * GRID_W, hp * head_dim), BF16)],
        compiler_params=_params(("parallel", "parallel", "arbitrary")),
        name="natten",
    )(qkv, *([qkv] * (2 * n_kc)), bias)


def _pool_body(u_ref, prev_ref, next_ref, w_ref, scale_ref, o_ref, ext_ref, *, seq, halo, group_dim):
    t = pl.program_id(1)
    tt = u_ref.shape[1]
    u = u_ref[0]
    ext_ref[pl.ds(halo, tt), :] = u
    ext_ref[pl.ds(0, halo), :] = jnp.where(t > 0, prev_ref[0], 0.0)
    ext_ref[pl.ds(halo + tt, halo), :] = jnp.where(t < pl.num_programs(1) - 1, next_ref[0], 0.0)
    pos = t * tt + lax.broadcasted_iota(jnp.int32, (tt, 1), 0)
    for g, w in enumerate(POOL_WINDOWS):
        cols = pl.ds(g * group_dim, group_dim)
        total = ext_ref[pl.ds(halo - w // 2, tt), cols]
        for d in range(-w // 2 + 1, w // 2):
            total = total + ext_ref[pl.ds(halo + d, tt), cols]
        count = jnp.minimum(pos + w // 2, seq) - jnp.maximum(pos - w // 2, 0)
        pooled = total / count.astype(F32) - u[:, g * group_dim:(g + 1) * group_dim]
        mixed = jnp.dot(pooled.astype(BF16), w_ref[g], preferred_element_type=F32)
        o_ref[0, :, cols] = (mixed * scale_ref[:, cols]).astype(o_ref.dtype)


def _pool(u, w_pool, pool_scale, *, tt=512):
    b, seq, width = u.shape
    n_groups, group_dim, _ = w_pool.shape
    assert n_groups == len(POOL_WINDOWS)
    halo = 8
    assert max(POOL_WINDOWS) // 2 <= halo
    tt = min(tt, seq)
    assert seq % tt == 0 and tt % halo == 0
    hb = tt // halo
    return pl.pallas_call(
        functools.partial(_pool_body, seq=seq, halo=halo, group_dim=group_dim),
        grid=(b, seq // tt),
        in_specs=[
            pl.BlockSpec((1, tt, width), lambda bi, t: (bi, t, 0)),
            pl.BlockSpec((1, halo, width), lambda bi, t: (bi, jnp.maximum(t * hb - 1, 0), 0)),
            pl.BlockSpec((1, halo, width), lambda bi, t: (bi, jnp.minimum((t + 1) * hb, seq // halo - 1), 0)),
            pl.BlockSpec((n_groups, group_dim, group_dim), lambda bi, t: (0, 0, 0)),
            pl.BlockSpec((1, width), lambda bi, t: (0, 0)),
        ],
        out_specs=pl.BlockSpec((1, tt, width), lambda bi, t: (bi, t, 0)),
        out_shape=jax.ShapeDtypeStruct((b, seq, width), BF16),
        scratch_shapes=[pltpu.VMEM((tt + 2 * halo, width), F32)],
        compiler_params=_params(("parallel", "arbitrary")),
        name="pool",
    )(u, u, u, w_pool, pool_scale)


def _memkv_body(m_ref, g_ref, w_ref, o_ref):
    h = _rms(m_ref[...], g_ref[...]).astype(BF16)
    o_ref[...] = jnp.dot(h, w_ref[...], preferred_element_type=F32).astype(o_ref.dtype)


def _memkv(mem, g_mem, w_kv, *, tr=256, tn=512):
    n_rows, d = mem.shape
    width = w_kv.shape[1]
    tr = min(tr, n_rows)
    tn = min(tn, width)
    assert n_rows % tr == 0 and width % tn == 0
    return pl.pallas_call(
        _memkv_body,
        grid=(n_rows // tr, width // tn),
        in_specs=[
            pl.BlockSpec((tr, d), lambda i, j: (i, 0)),
            pl.BlockSpec((1, d), lambda i, j: (0, 0)),
            pl.BlockSpec((d, tn), lambda i, j: (0, j)),
        ],
        out_specs=pl.BlockSpec((tr, tn), lambda i, j: (i, j)),
        out_shape=jax.ShapeDtypeStruct((n_rows, width), BF16),
        compiler_params=_params(("parallel", "arbitrary")),
        name="memkv",
    )(mem, g_mem, w_kv)


def _memattn_body(q_ref, kv_ref, o_ref, *, n_heads, head_dim):
    scale = head_dim ** -0.5
    width = n_heads * head_dim
    for h in range(n_heads):
        cols = slice(h * head_dim, (h + 1) * head_dim)
        q = q_ref[0, :, cols]
        k = kv_ref[0, :, cols]
        v = kv_ref[0, :, width + h * head_dim: width + (h + 1) * head_dim]
        s = lax.dot_general(q, k, (((1,), (1,)), ((), ())), preferred_element_type=F32) * scale
        m = jnp.max(s, axis=-1, keepdims=True)
        p = jnp.exp(s - m)
        l = jnp.sum(p, axis=-1, keepdims=True)
        o = jnp.dot(p.astype(BF16), v, preferred_element_type=F32)
        o_ref[0, :, cols] = (o / l).astype(o_ref.dtype)


def _memattn(q, kv, *, n_heads, tq=512):
    b, seq, width = q.shape
    n_mem = kv.shape[1]
    tq = min(tq, seq)
    assert seq % tq == 0
    return pl.pallas_call(
        functools.partial(_memattn_body, n_heads=n_heads, head_dim=width // n_heads),
        grid=(b, seq // tq),
        in_specs=[
            pl.BlockSpec((1, tq, width), lambda bi, t: (bi, t, 0)),
            pl.BlockSpec((1, n_mem, 2 * width), lambda bi, t: (bi, 0, 0)),
        ],
        out_specs=pl.BlockSpec((1, tq, width), lambda bi, t: (bi, t, 0)),
        out_shape=jax.ShapeDtypeStruct((b, seq, width), BF16),
        compiler_params=_params(("parallel", "arbitrary")),
        name="memattn",
    )(q, kv)


def _merge_body(h_ref, a_ref, b_ref, c_ref, wga_ref, wgb_ref, wgc_ref, bg_ref, wa_ref, wb_ref, wc_ref, o_ref):
    h = h_ref[...]
    merged = None
    for i, (x_ref, wg_ref, w_ref) in enumerate(
            ((a_ref, wga_ref, wa_ref), (b_ref, wgb_ref, wb_ref), (c_ref, wgc_ref, wc_ref))):
        logits = jnp.dot(h, wg_ref[...], preferred_element_type=F32) + bg_ref[i:i + 1, :]
        gate = 1.0 / (1.0 + jnp.exp(-logits))
        y = jnp.dot(x_ref[...], w_ref[...], preferred_element_type=F32)
        merged = gate * y if merged is None else merged + gate * y
    o_ref[...] = merged.astype(o_ref.dtype)


def _merge(h, att_a, mix_b, att_c, w_in, gate_col0, b_gate, w_a, w_b, w_c, *, tm=512, tn=512):
    n_tok, d = h.shape
    tm = min(tm, n_tok)
    tn = min(tn, d)
    assert n_tok % tm == 0 and d % tn == 0 and gate_col0 % tn == 0
    nj = d // tn
    g0 = gate_col0 // tn

    def act(x):
        return pl.BlockSpec((tm, x.shape[1]), lambda j, i: (i, 0))

    def gate_w(branch):
        return pl.BlockSpec((d, tn), lambda j, i: (0, g0 + branch * nj + j))

    def out_w(w):
        return pl.BlockSpec((w.shape[0], tn), lambda j, i: (0, j))

    return pl.pallas_call(
        _merge_body,
        grid=(nj, n_tok // tm),
        in_specs=[act(h), act(att_a), act(mix_b), act(att_c),
                  gate_w(0), gate_w(1), gate_w(2),
                  pl.BlockSpec((b_gate.shape[0], tn), lambda j, i: (0, j)),
                  out_w(w_a), out_w(w_b), out_w(w_c)],
        out_specs=pl.BlockSpec((tm, tn), lambda j, i: (i, j)),
        out_shape=jax.ShapeDtypeStruct((n_tok, d), BF16),
        compiler_params=_params(("parallel", "parallel")),
        name="merge",
    )(h, att_a, mix_b, att_c, w_in, w_in, w_in, b_gate, w_a, w_b, w_c)


def _wo_body(x_ref, m_ref, w_ref, g_ref, o_ref, *, row_chunk):
    j = pl.program_id(1)
    tm = x_ref.shape[0]

    @pl.when(j == 0)
    def _():
        def prologue(rows):
            o_ref[rows, :] = jnp.zeros((row_chunk, o_ref.shape[1]), F32)
        _for_row_chunks(tm, row_chunk, prologue)

    o_ref[...] += jnp.dot(m_ref[...], w_ref[...], preferred_element_type=F32)

    @pl.when(j == pl.num_programs(1) - 1)
    def _():
        def epilogue(rows):
            o_ref[rows, :] = x_ref[rows, :] + _rms(o_ref[rows, :], g_ref[...])
        _for_row_chunks(tm, row_chunk, epilogue)


def _wo(x, merged, w_o, g_post, *, tm=512, tk=1024):
    n_tok, d = x.shape
    tm = min(tm, n_tok)
    tk = min(tk, d)
    assert n_tok % tm == 0 and d % tk == 0
    return pl.pallas_call(
        functools.partial(_wo_body, row_chunk=min(64, tm)),
        grid=(n_tok // tm, d // tk),
        in_specs=[
            pl.BlockSpec((tm, d), lambda i, j: (i, 0)),
            pl.BlockSpec((tm, tk), lambda i, j: (i, j)),
            pl.BlockSpec((tk, d), lambda i, j: (j, 0)),
            pl.BlockSpec((1, d), lambda i, j: (0, 0)),
        ],
        out_specs=pl.BlockSpec((tm, d), lambda i, j: (i, 0)),
        out_shape=jax.ShapeDtypeStruct((n_tok, d), F32),
        compiler_params=_params(("parallel", "arbitrary")),
        name="wo",
    )(x, merged, w_o, g_post)


def _mixer(x1, h, mem, p, bias_tables, *, b, seq):
    n_tok, d = x1.shape
    na_heads = p["rpb"].shape[0]
    n_groups, group_dim, _ = p["w_pool"].shape
    pool_width = n_groups * group_dim
    mem_width = p["w_mem_kv"].shape[1] // 2
    na_width = p["w_a_out"].shape[0]
    gate_col0 = 3 * na_width + pool_width + mem_width

    qkv, u, qc = _inproj(h, p["w_in"], na_width=na_width, pool_width=pool_width, mem_width=mem_width)
    att_a = _natten(qkv.reshape(b, seq, 3 * na_width), bias_tables,
                    n_heads=na_heads, head_dim=na_width // na_heads, kh=(p["rpb"].shape[1] + 1) // 2)
    mix_b = _pool(u.reshape(b, seq, pool_width), p["w_pool"], p["pool_scale"])
    kv = _memkv(mem.reshape(b * mem.shape[1], d), p["g_mem"], p["w_mem_kv"])
    att_c = _memattn(qc.reshape(b, seq, mem_width), kv.reshape(b, mem.shape[1], 2 * mem_width),
                     n_heads=MEM_HEADS)
    merged = _merge(h, att_a.reshape(n_tok, na_width), mix_b.reshape(n_tok, pool_width),
                    att_c.reshape(n_tok, mem_width), p["w_in"], gate_col0, p["b_gate"],
                    p["w_a_out"], p["w_b_out"], p["w_c_out"])
    return _wo(x1, merged, p["w_o"], p["g_mix_post"])


_MATMUL_WEIGHTS = ("w1_gate", "w1_up", "w1_down", "w_in", "w_pool", "w_mem_kv",
                   "w_a_out", "w_b_out", "w_c_out", "w_o", "w2_gate", "w2_up", "w2_down")
_ROW_VECTORS = ("g_ffn1_pre", "g_ffn1_post", "g_mix_pre", "pool_scale", "g_mem", "g_mix_post",
                "g_ffn2_pre", "g_ffn2_post", "g_final")


def kernel(x_prompt, x_sample, mem_prompt, mem_sample, g_ffn1_pre, w1_gate, w1_up, w1_down, g_ffn1_post, g_mix_pre, w_in, rpb, w_pool, pool_scale, g_mem, w_mem_kv, w_a_out, w_b_out, w_c_out, b_gate, w_o, g_mix_post, g_ffn2_pre, w2_gate, w2_up, w2_down, g_ffn2_post, g_final):
    stacked = dict(g_ffn1_pre=g_ffn1_pre, w1_gate=w1_gate, w1_up=w1_up, w1_down=w1_down,
                   g_ffn1_post=g_ffn1_post, g_mix_pre=g_mix_pre, w_in=w_in, rpb=rpb, w_pool=w_pool,
                   pool_scale=pool_scale, g_mem=g_mem, w_mem_kv=w_mem_kv, w_a_out=w_a_out,
                   w_b_out=w_b_out, w_c_out=w_c_out, b_gate=b_gate, w_o=w_o, g_mix_post=g_mix_post,
                   g_ffn2_pre=g_ffn2_pre, w2_gate=w2_gate, w2_up=w2_up, w2_down=w2_down,
                   g_ffn2_post=g_ffn2_post, g_final=g_final)
    y_prompt, y_sample = x_prompt, x_sample
    for layer in range(g_final.shape[0]):
        p = {name: a[layer] for name, a in stacked.items()}
        for name in _MATMUL_WEIGHTS:
            p[name] = p[name].astype(BF16)
        for name in _ROW_VECTORS:
            p[name] = p[name].reshape(1, -1)
        bias_tables = _natten_bias_tables(p["rpb"])
        (bp, sp, d), (bs, ss, _) = y_prompt.shape, y_sample.shape
        x1p, x1s, hp, hs = _ffn_pair(
            y_prompt.reshape(bp * sp, d), y_sample.reshape(bs * ss, d),
            p["g_ffn1_pre"], p["w1_gate"], p["w1_up"], p["w1_down"], p["g_ffn1_post"], p["g_final"],
            p["g_mix_pre"], final_norm=False, emit_h=True)
        x2p = _mixer(x1p, hp, mem_prompt, p, bias_tables, b=bp, seq=sp)
        x2s = _mixer(x1s, hs, mem_sample, p, bias_tables, b=bs, seq=ss)
        yp, ys = _ffn_pair(
            x2p, x2s, p["g_ffn2_pre"], p["w2_gate"], p["w2_up"], p["w2_down"], p["g_ffn2_post"], p["g_final"],
            p["g_final"], final_norm=True, emit_h=False)
        y_prompt, y_sample = yp.reshape(bp, sp, d), ys.reshape(bs, ss, d)
    return (y_prompt, y_sample)
```

```python
import functools

import numpy as np
import jax
import jax.numpy as jnp
from jax import lax
from jax.experimental import pallas as pl
from jax.experimental.pallas import tpu as pltpu

GRID_W = 64
POOL_WINDOWS = (2, 4, 8, 16)
MEM_HEADS = 4
RMS_EPS = 1e-6
MASK_VALUE = -1e30
LOG2_E = 1.4426950408889634

FFN_CHUNK = 256
NAT_Q_ROWS = 8
NAT_K_ROWS = 16
NAT_K_CHUNK_ROWS = 4
NAT_SUB_ROWS = 4

MXU_COLS = 256
V7X_VMEM_BYTES = 64 * 1024 * 1024
VMEM_LIMIT = V7X_VMEM_BYTES - 6 * 1024 * 1024

F32 = jnp.float32
BF16 = jnp.bfloat16


def _rms(x, g):
    ms = jnp.mean(x * x, axis=-1, keepdims=True)
    return (x * lax.rsqrt(ms + RMS_EPS)) * g


def _silu_mul(g, u):
    return (g * (1.0 / (1.0 + jnp.exp(-g)))) * u


def _for_row_chunks(n_rows, chunk, fn):
    def body(c, carry):
        fn(pl.ds(pl.multiple_of(c * chunk, chunk), chunk))
        return carry
    lax.fori_loop(0, n_rows // chunk, body, 0)


def _params(sem):
    return pltpu.CompilerParams(dimension_semantics=sem, vmem_limit_bytes=VMEM_LIMIT)


def _ffn_pair_body(*refs, tm, tf, rc, edge_rows, n_a, n_tiles, n_ff, final_norm, emit_h):
    (xa_hbm, xb_hbm, gpre_ref, wg_hbm, wu_hbm, wd_hbm, gpost_ref, gfin_ref, gnext_ref), refs = refs[:9], refs[9:]
    if emit_h:
        (oa_hbm, ob_hbm, na_hbm, nb_hbm), refs = refs[:4], refs[4:]
    else:
        (oa_hbm, ob_hbm), refs = refs[:2], refs[2:]
        na_hbm = nb_hbm = None
    (hbuf, accbuf, gbuf, ubuf, abuf, wgbuf, wubuf, wdbuf, xpbuf, xebuf, obuf, nbuf, xedge, oedge, nedge,
     wsem, xpsem, xesem, osem, nsem, edge_in_sem, edge_out_sem, edge_next_sem) = refs

    n_chunks = tm // rc
    n_steps = n_tiles * n_ff
    n_edge = tm // edge_rows

    def step_after(tj):
        t, j = tj
        wrap = j == n_ff - 1
        return jnp.where(wrap, t + 1, t), jnp.where(wrap, 0, j + 1)

    def step_before(tj):
        t, j = tj
        wrap = j == 0
        return jnp.where(wrap, t - 1, t), jnp.where(wrap, n_ff - 1, j - 1)

    def coords(s):
        return jnp.int32(s // n_ff), jnp.int32(s % n_ff)

    def up_weight_copies(j_up, slot):
        cols = pl.ds(pl.multiple_of(j_up * tf, tf), tf)
        return [pltpu.make_async_copy(wg_hbm.at[:, cols], wgbuf.at[slot], wsem.at[slot, 0]),
                pltpu.make_async_copy(wu_hbm.at[:, cols], wubuf.at[slot], wsem.at[slot, 1])]

    def weight_copies(j_down, j_up, slot):
        rows = pl.ds(pl.multiple_of(j_down * tf, tf), tf)
        return up_weight_copies(j_up, slot) + [
            pltpu.make_async_copy(wd_hbm.at[rows, :], wdbuf.at[slot], wsem.at[slot, 2])]

    def start_rows_in(row0, n_rows, dst, sem):
        row0 = jnp.asarray(row0, jnp.int32)

        @pl.when(row0 < n_a)
        def _():
            pltpu.make_async_copy(xa_hbm.at[pl.ds(pl.multiple_of(row0, n_rows), n_rows), :], dst, sem).start()

        @pl.when(row0 >= n_a)
        def _():
            pltpu.make_async_copy(xb_hbm.at[pl.ds(pl.multiple_of(row0 - n_a, n_rows), n_rows), :], dst, sem).start()

    def wait_rows_in(n_rows, dst, sem):
        pltpu.make_async_copy(xa_hbm.at[pl.ds(0, n_rows), :], dst, sem).wait()

    def start_rows_out(row0, n_rows, src, sem, dst_a, dst_b):
        row0 = jnp.asarray(row0, jnp.int32)

        @pl.when(row0 < n_a)
        def _():
            pltpu.make_async_copy(src, dst_a.at[pl.ds(pl.multiple_of(row0, n_rows), n_rows), :], sem).start()

        @pl.when(row0 >= n_a)
        def _():
            pltpu.make_async_copy(src, dst_b.at[pl.ds(pl.multiple_of(row0 - n_a, n_rows), n_rows), :], sem).start()

    def wait_rows_out(n_rows, src, sem, dst_a):
        pltpu.make_async_copy(src, dst_a.at[pl.ds(0, n_rows), :], sem).wait()

    def start_results_out(row0, n_rows, out_src, out_sem, next_src, next_sem):
        start_rows_out(row0, n_rows, out_src, out_sem, oa_hbm, ob_hbm)
        if emit_h:
            start_rows_out(row0, n_rows, next_src, next_sem, na_hbm, nb_hbm)

    def wait_results_out(n_rows, out_src, out_sem, next_src, next_sem):
        wait_rows_out(n_rows, out_src, out_sem, oa_hbm)
        if emit_h:
            wait_rows_out(n_rows, next_src, next_sem, na_hbm)

    def pre_chunk(j):
        return jnp.minimum(j, n_chunks - 1)

    def epi_chunk(j):
        return jnp.maximum(j - (n_ff - n_chunks), 0)

    def stage_inputs(tj):
        t, j = tj
        j_up = jnp.where(j == n_ff - 1, 0, j + 1)
        row_pre = jnp.minimum(t + 1, n_tiles - 1) * tm + pre_chunk(j) * rc
        row_epi = jnp.clip(t - 1, 0, n_tiles - 1) * tm + epi_chunk(j) * rc
        return j, j_up, row_pre, row_epi

    def start_inputs(tj, slot):
        j_down, j_up, row_pre, row_epi = stage_inputs(tj)
        for c in weight_copies(j_down, j_up, slot):
            c.start()
        start_rows_in(row_pre, rc, xpbuf.at[slot], xpsem.at[slot])
        start_rows_in(row_epi, rc, xebuf.at[slot], xesem.at[slot])

    def wait_inputs(slot):
        for c in weight_copies(0, 0, slot):
            c.wait()
        wait_rows_in(rc, xpbuf.at[slot], xpsem.at[slot])
        wait_rows_in(rc, xebuf.at[slot], xesem.at[slot])

    def finishes_rows(tj):
        t, j = tj
        return jnp.logical_and(t >= 1, j >= n_ff - n_chunks)

    def finish_rows(acc_rows, x_rows, out_ref, next_ref):
        out = x_rows + 0.5 * _rms(acc_rows, gpost_ref[...])
        if final_norm:
            out = _rms(out, gfin_ref[...])
        out_ref[...] = out
        if emit_h:
            next_ref[...] = _rms(out, gnext_ref[...]).astype(BF16)

    def up_chunk(h_sel, slot):
        h = hbuf[h_sel]
        gbuf[...] = jnp.dot(h, wgbuf[slot], preferred_element_type=F32)
        ubuf[...] = jnp.dot(h, wubuf[slot], preferred_element_type=F32)

    start_rows_in(0, edge_rows, xedge.at[0], edge_in_sem.at[0])
    for c in range(n_edge):
        wait_rows_in(edge_rows, xedge.at[c % 2], edge_in_sem.at[c % 2])
        if c + 1 < n_edge:
            start_rows_in((c + 1) * edge_rows, edge_rows, xedge.at[(c + 1) % 2], edge_in_sem.at[(c + 1) % 2])
        hbuf[0, pl.ds(c * edge_rows, edge_rows), :] = _rms(xedge[c % 2], gpre_ref[...]).astype(BF16)
        accbuf[1, pl.ds(c * edge_rows, edge_rows), :] = jnp.zeros((edge_rows, accbuf.shape[2]), F32)
    for c in up_weight_copies(0, 1):
        c.start()
    for c in up_weight_copies(0, 1):
        c.wait()
    up_chunk(0, 1)
    start_inputs(coords(0), 0)
    start_inputs(coords(1), 1)
    wait_inputs(0)

    def step(_, carry):
        t, j, slot = carry
        tj = (t, j)
        cur = t % 2
        other = 1 - cur

        @pl.when(j == 0)
        def _():
            def zero(zrows):
                accbuf[cur, zrows, :] = jnp.zeros((edge_rows, accbuf.shape[2]), F32)
            _for_row_chunks(tm, edge_rows, zero)

        abuf[...] = _silu_mul(gbuf[...], ubuf[...]).astype(BF16)
        pre_rows = pl.ds(pl.multiple_of(pre_chunk(j) * rc, rc), rc)
        epi_rows = pl.ds(pl.multiple_of(epi_chunk(j) * rc, rc), rc)
        hbuf[other, pre_rows, :] = _rms(xpbuf[slot], gpre_ref[...]).astype(BF16)
        finish_rows(accbuf[other, epi_rows, :], xebuf[slot], obuf.at[slot], nbuf.at[slot])
        accbuf[cur] += jnp.dot(abuf[...], wdbuf[slot], preferred_element_type=F32)
        up_chunk(jnp.where(j == n_ff - 1, other, cur), slot)

        after = step_after(tj)
        start_inputs(step_after(after), slot)

        @pl.when(finishes_rows(step_before(tj)))
        def _():
            wait_results_out(rc, obuf.at[1 - slot], osem.at[1 - slot], nbuf.at[1 - slot], nsem.at[1 - slot])

        @pl.when(finishes_rows(tj))
        def _():
            start_results_out(stage_inputs(tj)[3], rc, obuf.at[slot], osem.at[slot], nbuf.at[slot], nsem.at[slot])

        wait_inputs(1 - slot)
        return after + (1 - slot,)

    lax.fori_loop(0, n_steps, step, coords(0) + (jnp.int32(0),))

    wait_inputs(1 - n_steps % 2)

    @pl.when(finishes_rows(coords(n_steps - 1)))
    def _():
        last = (n_steps - 1) % 2
        wait_results_out(rc, obuf.at[last], osem.at[last], nbuf.at[last], nsem.at[last])

    last_row0 = (n_tiles - 1) * tm

    def edge_bufs(c):
        return oedge.at[c % 2], edge_out_sem.at[c % 2], nedge.at[c % 2], edge_next_sem.at[c % 2]

    start_rows_in(last_row0, edge_rows, xedge.at[0], edge_in_sem.at[0])
    for c in range(n_edge):
        wait_rows_in(edge_rows, xedge.at[c % 2], edge_in_sem.at[c % 2])
        if c + 1 < n_edge:
            start_rows_in(last_row0 + (c + 1) * edge_rows, edge_rows, xedge.at[(c + 1) % 2],
                          edge_in_sem.at[(c + 1) % 2])
        if c >= 2:
            wait_results_out(edge_rows, *edge_bufs(c))
        finish_rows(accbuf[(n_tiles - 1) % 2, pl.ds(c * edge_rows, edge_rows), :], xedge[c % 2],
                    oedge.at[c % 2], nedge.at[c % 2])
        start_results_out(last_row0 + c * edge_rows, edge_rows, *edge_bufs(c))
    for c in range(max(n_edge - 2, 0), n_edge):
        wait_results_out(edge_rows, *edge_bufs(c))


def _ffn_pair(xa, xb, g_pre, wg, wu, wd, g_post, g_fin, g_next, *, final_norm, emit_h,
              tm=768, tf=FFN_CHUNK, rc=32, edge_rows=32):
    (n_a, d), n_b = xa.shape, xb.shape[0]
    d_ff = wg.shape[1]
    n_tok = n_a + n_b
    tm = min(tm, n_tok // 2)
    tf = min(tf, d_ff)
    edge_rows = min(edge_rows, tm)
    n_tiles, n_ff = n_tok // tm, d_ff // tf
    assert n_tok % tm == 0 and d_ff % tf == 0 and tm % rc == 0 and tm % edge_rows == 0
    assert n_a % edge_rows == 0 and n_a % rc == 0 and tm // rc <= n_ff and tm // edge_rows >= 2 and n_tiles >= 2
    hbm = pl.BlockSpec(memory_space=pl.ANY)
    vec = pl.BlockSpec(memory_space=pltpu.VMEM)
    out_shape = [jax.ShapeDtypeStruct((n_a, d), F32), jax.ShapeDtypeStruct((n_b, d), F32)]
    if emit_h:
        out_shape += [jax.ShapeDtypeStruct((n_a, d), BF16), jax.ShapeDtypeStruct((n_b, d), BF16)]
    next_rows = (rc, edge_rows) if emit_h else (8, 8)
    return pl.pallas_call(
        functools.partial(_ffn_pair_body, tm=tm, tf=tf, rc=rc, edge_rows=edge_rows, n_a=n_a, n_tiles=n_tiles,
                          n_ff=n_ff, final_norm=final_norm, emit_h=emit_h),
        in_specs=[hbm, hbm, vec, hbm, hbm, hbm, vec, vec, vec],
        out_specs=[hbm] * len(out_shape),
        out_shape=out_shape,
        scratch_shapes=[
            pltpu.VMEM((2, tm, d), BF16),
            pltpu.VMEM((2, tm, d), F32),
            pltpu.VMEM((tm, tf), F32),
            pltpu.VMEM((tm, tf), F32),
            pltpu.VMEM((tm, tf), BF16),
            pltpu.VMEM((2, d, tf), BF16),
            pltpu.VMEM((2, d, tf), BF16),
            pltpu.VMEM((2, tf, d), BF16),
            pltpu.VMEM((2, rc, d), F32),
            pltpu.VMEM((2, rc, d), F32),
            pltpu.VMEM((2, rc, d), F32),
            pltpu.VMEM((2, next_rows[0], d), BF16),
            pltpu.VMEM((2, edge_rows, d), F32),
            pltpu.VMEM((2, edge_rows, d), F32),
            pltpu.VMEM((2, next_rows[1], d), BF16),
            pltpu.SemaphoreType.DMA((2, 3)),
            pltpu.SemaphoreType.DMA((2,)),
            pltpu.SemaphoreType.DMA((2,)),
            pltpu.SemaphoreType.DMA((2,)),
            pltpu.SemaphoreType.DMA((2,)),
            pltpu.SemaphoreType.DMA((2,)),
            pltpu.SemaphoreType.DMA((2,)),
            pltpu.SemaphoreType.DMA((2,)),
        ],
        compiler_params=pltpu.CompilerParams(vmem_limit_bytes=VMEM_LIMIT),
        name="ffn_pair",
    )(xa, xb, g_pre, wg, wu, wd, g_post, g_fin, g_next)


def _inproj_body(h_ref, w_ref, qkv_ref, u_ref, qc_ref, *, n_qkv, n_u):
    j = pl.program_id(1)

    def project_into(o_ref):
        for c in range(0, w_ref.shape[1], MXU_COLS):
            z = jnp.dot(h_ref[...], w_ref[:, c:c + MXU_COLS], preferred_element_type=F32)
            o_ref[:, c:c + MXU_COLS] = z.astype(o_ref.dtype)

    @pl.when(j < n_qkv)
    def _():
        project_into(qkv_ref)

    @pl.when(jnp.logical_and(j >= n_qkv, j < n_qkv + n_u))
    def _():
        project_into(u_ref)

    @pl.when(j >= n_qkv + n_u)
    def _():
        project_into(qc_ref)


def _inproj(h, w_in, *, na_width, pool_width, mem_width, tm=1024, tn=1024):
    n_tok, d = h.shape
    tm = min(tm, n_tok)
    tn = min(tn, pool_width, mem_width)
    assert n_tok % tm == 0 and na_width % tn == 0 and pool_width % tn == 0 and mem_width % tn == 0
    n_qkv = 3 * na_width // tn
    n_u = pool_width // tn
    n_qc = mem_width // tn
    return pl.pallas_call(
        functools.partial(_inproj_body, n_qkv=n_qkv, n_u=n_u),
        grid=(n_tok // tm, n_qkv + n_u + n_qc),
        in_specs=[
            pl.BlockSpec((tm, d), lambda i, j: (i, 0)),
            pl.BlockSpec((d, tn), lambda i, j: (0, j)),
        ],
        out_specs=[
            pl.BlockSpec((tm, tn), lambda i, j: (i, jnp.minimum(j, n_qkv - 1))),
            pl.BlockSpec((tm, tn), lambda i, j: (i, jnp.clip(j - n_qkv, 0, n_u - 1))),
            pl.BlockSpec((tm, tn), lambda i, j: (i, jnp.clip(j - n_qkv - n_u, 0, n_qc - 1))),
        ],
        out_shape=[
            jax.ShapeDtypeStruct((n_tok, 3 * na_width), BF16),
            jax.ShapeDtypeStruct((n_tok, pool_width), F32),
            jax.ShapeDtypeStruct((n_tok, mem_width), BF16),
        ],
        compiler_params=_params(("parallel", "arbitrary")),
        name="inproj",
    )(h, w_in)


def _natten_bias_tables(rpb):
    n_heads, n_dr, n_dc = rpb.shape
    kh, kw = (n_dr + 1) // 2, (n_dc + 1) // 2
    cols = np.arange(GRID_W)
    col_start = np.clip(cols - kw // 2, 0, GRID_W - kw)
    col_valid = (cols[None, :] >= col_start[:, None]) & (cols[None, :] < col_start[:, None] + kw)
    period = 2 * GRID_W
    lpad = GRID_W - kw
    padded = jnp.pad(rpb.astype(F32) * LOG2_E, ((0, 0), (0, 0), (lpad, period - n_dc - lpad)))
    skew = jnp.broadcast_to(padded[:, :, None, :], (n_heads, n_dr, GRID_W, period))
    skew = skew.reshape(n_heads, n_dr, GRID_W * period)[:, :, :GRID_W * (period - 1)]
    by_col = skew.reshape(n_heads, n_dr, GRID_W, period - 1)[:, :, :, GRID_W - 1:]
    by_col = jnp.where(jnp.asarray(col_valid)[None, None], by_col, MASK_VALUE).transpose(0, 1, 3, 2)
    rpad = NAT_K_ROWS - kh
    wide = jnp.concatenate([by_col, by_col], axis=-1)
    wide = jnp.pad(wide, ((0, 0), (rpad, 2 * NAT_K_ROWS - n_dr - rpad), (0, 0), (0, 0)), constant_values=MASK_VALUE)
    n_sub = NAT_Q_ROWS // NAT_SUB_ROWS
    block = (n_sub, _natten_window_rows(kh) * GRID_W, NAT_SUB_ROWS * GRID_W)
    return pl.pallas_call(
        functools.partial(_natten_bias_body, kh=kh),
        grid=(n_heads,),
        in_specs=[pl.BlockSpec((1,) + wide.shape[1:], lambda h: (h, 0, 0, 0))],
        out_specs=pl.BlockSpec((3, 1) + block, lambda h: (0, h, 0, 0, 0)),
        out_shape=jax.ShapeDtypeStruct((3, n_heads) + block, F32),
        compiler_params=_params(("parallel",)),
        name="natten_bias",
    )(wide)


def _natten_row_start(r, kh):
    return min(max(r - kh // 2, 0), NAT_K_ROWS - kh)


def _natten_window_rows(kh):
    return NAT_SUB_ROWS - 1 + kh


def _natten_window_start(t, sub, kh):
    r_first = (NAT_K_ROWS - NAT_Q_ROWS) // 2 * t + sub * NAT_SUB_ROWS
    return min(_natten_row_start(r_first, kh), NAT_K_ROWS - _natten_window_rows(kh))


def _natten_bias_body(wide_ref, o_ref, *, kh):
    low_half = lax.broadcasted_iota(jnp.int32, (GRID_W, 2 * GRID_W), 1) < GRID_W
    masked = jnp.full((GRID_W, 2 * GRID_W), MASK_VALUE, F32)
    for t in range(3):
        for sub in range(NAT_Q_ROWS // NAT_SUB_ROWS):
            first_key_row = _natten_window_start(t, sub, kh)
            for kl in range(_natten_window_rows(kh)):
                rj = first_key_row + kl
                for pair in range(NAT_SUB_ROWS // 2):
                    halves = []
                    for ri in (sub * NAT_SUB_ROWS + 2 * pair, sub * NAT_SUB_ROWS + 2 * pair + 1):
                        r = (NAT_K_ROWS - NAT_Q_ROWS) // 2 * t + ri
                        valid = _natten_row_start(r, kh) <= rj < _natten_row_start(r, kh) + kh
                        halves.append(wide_ref[0, rj - r + NAT_K_ROWS - 1] if valid else masked)
                    o_ref[t, 0, sub, pl.ds(kl * GRID_W, GRID_W), pl.ds(pair * 2 * GRID_W, 2 * GRID_W)] = (
                        jnp.where(low_half, halves[0], halves[1]))


def _natten_body(q_ref, k0_ref, k1_ref, k2_ref, k3_ref, v0_ref, v1_ref, v2_ref, v3_ref, bias_ref, o_ref,
                 kbuf, vbuf, *, log2_scale, head_dim, kh):
    r = pl.program_id(2)
    table = jnp.where(r == 0, 0, jnp.where(r == pl.num_programs(2) - 1, 2, 1))
    chunk = k0_ref.shape[1]
    for c, (k_ref, v_ref) in enumerate(((k0_ref, v0_ref), (k1_ref, v1_ref), (k2_ref, v2_ref), (k3_ref, v3_ref))):
        kbuf[pl.ds(c * chunk, chunk), :] = k_ref[0]
        vbuf[pl.ds(c * chunk, chunk), :] = v_ref[0]
    n_win = _natten_window_rows(kh) * GRID_W
    n_q = NAT_SUB_ROWS * GRID_W
    blocks = []
    for sub in range(NAT_Q_ROWS // NAT_SUB_ROWS):
        starts = [_natten_window_start(t, sub, kh) * GRID_W for t in range(3)]
        start = jnp.where(table == 0, starts[0], jnp.where(table == 1, starts[1], starts[2]))
        keys = pl.ds(pl.multiple_of(start, GRID_W), n_win)
        for h in range(q_ref.shape[2] // head_dim):
            blocks.append((sub, h, keys, pl.ds(sub * n_q, n_q), slice(h * head_dim, (h + 1) * head_dim)))
    scores = [lax.dot_general(kbuf[keys, cols], q_ref[0, queries, cols], (((1,), (1,)), ((), ())),
                              preferred_element_type=F32)
              for _, _, keys, queries, cols in blocks]
    probs = []
    for (sub, h, _, _, _), s in zip(blocks, scores):
        s = s * log2_scale + bias_ref[0, h, sub]
        p = jnp.exp2(s - jnp.max(s, axis=0, keepdims=True))
        probs.append((p.astype(BF16), jnp.sum(p, axis=0, keepdims=True)))
    for (_, _, keys, queries, cols), (p, l) in zip(blocks, probs):
        o = lax.dot_general(vbuf[keys, cols], p, (((0,), (0,)), ((), ())), preferred_element_type=F32)
        o_ref[0, queries, cols] = (o / l).T.astype(o_ref.dtype)


def _natten(qkv, bias, *, n_heads, head_dim, kh, heads_per_step=8):
    b, seq, _ = qkv.shape
    tq = NAT_Q_ROWS * GRID_W
    tk = NAT_K_CHUNK_ROWS * GRID_W
    n_kc = NAT_K_ROWS // NAT_K_CHUNK_ROWS
    hp = min(heads_per_step, n_heads)
    assert n_kc == 4 and seq % (NAT_K_ROWS * GRID_W) == 0 and n_heads % hp == 0
    n_hg = n_heads // hp
    n_qb = seq // tq
    n_kb = seq // tk
    lead = (NAT_K_ROWS - NAT_Q_ROWS) // 2 // NAT_K_CHUNK_ROWS

    def kv_spec(which, c):
        def index(bi, h, r):
            start = jnp.clip(r * (tq // tk) - lead, 0, n_kb - n_kc)
            return (bi, start + c, which * n_hg + h)
        return pl.BlockSpec((1, tk, hp * head_dim), index)

    def bias_index(bi, h, r):
        t = jnp.where(r == 0, 0, jnp.where(r == n_qb - 1, 2, 1))
        return (t, h, 0, 0, 0)

    return pl.pallas_call(
        functools.partial(_natten_body, log2_scale=head_dim ** -0.5 * LOG2_E, head_dim=head_dim, kh=kh),
        grid=(b, n_hg, n_qb),
        in_specs=[pl.BlockSpec((1, tq, hp * head_dim), lambda bi, h, r: (bi, r, h))]
        + [kv_spec(1, c) for c in range(n_kc)]
        + [kv_spec(2, c) for c in range(n_kc)]
        + [pl.BlockSpec((1, hp) + bias.shape[2:], bias_index)],
        out_specs=pl.BlockSpec((1, tq, hp * head_dim), lambda bi, h, r: (bi, r, h)),
        out_shape=jax.ShapeDtypeStruct((b, seq, n_heads * head_dim), BF16),
        scratch_shapes=[pltpu.VMEM((NAT_K_ROWS * GRID_W, hp * head_dim), BF16),
                        pltpu.VMEM((NAT_K_ROWS * GRID_W, hp * head_dim), BF16)],
        compiler_params=_params(("parallel", "parallel", "arbitrary")),
        name="natten",
    )(qkv, *([qkv] * (2 * n_kc)), bias)


def _pool_body(u_ref, prev_ref, next_ref, w_ref, scale_ref, o_ref, ext_ref, *, seq, halo, group_dim):
    t = pl.program_id(1)
    tt = u_ref.shape[1]
    u = u_ref[0]
    ext_ref[pl.ds(halo, tt), :] = u
    ext_ref[pl.ds(0, halo), :] = jnp.where(t > 0, prev_ref[0], 0.0)
    ext_ref[pl.ds(halo + tt, halo), :] = jnp.where(t < pl.num_programs(1) - 1, next_ref[0], 0.0)
    pos = t * tt + lax.broadcasted_iota(jnp.int32, (tt, 1), 0)
    for g, w in enumerate(POOL_WINDOWS):
        cols = pl.ds(g * group_dim, group_dim)
        total = ext_ref[pl.ds(halo - w // 2, tt), cols]
        for d in range(-w // 2 + 1, w // 2):
            total = total + ext_ref[pl.ds(halo + d, tt), cols]
        count = jnp.minimum(pos + w // 2, seq) - jnp.maximum(pos - w // 2, 0)
        pooled = total / count.astype(F32) - u[:, g * group_dim:(g + 1) * group_dim]
        mixed = jnp.dot(pooled.astype(BF16), w_ref[g], preferred_element_type=F32)
        o_ref[0, :, cols] = (mixed * scale_ref[:, cols]).astype(o_ref.dtype)


def _pool(u, w_pool, pool_scale, *, tt=512):
    b, seq, width = u.shape
    n_groups, group_dim, _ = w_pool.shape
    assert n_groups == len(POOL_WINDOWS)
    halo = 8
    assert max(POOL_WINDOWS) // 2 <= halo
    tt = min(tt, seq)
    assert seq % tt == 0 and tt % halo == 0
    hb = tt // halo
    return pl.pallas_call(
        functools.partial(_pool_body, seq=seq, halo=halo, group_dim=group_dim),
        grid=(b, seq // tt),
        in_specs=[
            pl.BlockSpec((1, tt, width), lambda bi, t: (bi, t, 0)),
            pl.BlockSpec((1, halo, width), lambda bi, t: (bi, jnp.maximum(t * hb - 1, 0), 0)),
            pl.BlockSpec((1, halo, width), lambda bi, t: (bi, jnp.minimum((t + 1) * hb, seq // halo - 1), 0)),
            pl.BlockSpec((n_groups, group_dim, group_dim), lambda bi, t: (0, 0, 0)),
            pl.BlockSpec((1, width), lambda bi, t: (0, 0)),
        ],
        out_specs=pl.BlockSpec((1, tt, width), lambda bi, t: (bi, t, 0)),
        out_shape=jax.ShapeDtypeStruct((b, seq, width), BF16),
        scratch_shapes=[pltpu.VMEM((tt + 2 * halo, width), F32)],
        compiler_params=_params(("parallel", "arbitrary")),
        name="pool",
    )(u, u, u, w_pool, pool_scale)


def _memkv_body(m_ref, g_ref, w_ref, o_ref):
    h = _rms(m_ref[...], g_ref[...]).astype(BF16)
    o_ref[...] = jnp.dot(h, w_ref[...], preferred_element_type=F32).astype(o_ref.dtype)


def _memkv(mem, g_mem, w_kv, *, tr=256, tn=512):
    n_rows, d = mem.shape
    width = w_kv.shape[1]
    tr = min(tr, n_rows)
    tn = min(tn, width)
    assert n_rows % tr == 0 and width % tn == 0
    return pl.pallas_call(
        _memkv_body,
        grid=(n_rows // tr, width // tn),
        in_specs=[
            pl.BlockSpec((tr, d), lambda i, j: (i, 0)),
            pl.BlockSpec((1, d), lambda i, j: (0, 0)),
            pl.BlockSpec((d, tn), lambda i, j: (0, j)),
        ],
        out_specs=pl.BlockSpec((tr, tn), lambda i, j: (i, j)),
        out_shape=jax.ShapeDtypeStruct((n_rows, width), BF16),
        compiler_params=_params(("parallel", "arbitrary")),
        name="memkv",
    )(mem, g_mem, w_kv)


def _memattn_body(q_ref, kv_ref, o_ref, *, n_heads, head_dim):
    scale = head_dim ** -0.5
    width = n_heads * head_dim
    for h in range(n_heads):
        cols = slice(h * head_dim, (h + 1) * head_dim)
        q = q_ref[0, :, cols]
        k = kv_ref[0, :, cols]
        v = kv_ref[0, :, width + h * head_dim: width + (h + 1) * head_dim]
        s = lax.dot_general(q, k, (((1,), (1,)), ((), ())), preferred_element_type=F32) * scale
        m = jnp.max(s, axis=-1, keepdims=True)
        p = jnp.exp(s - m)
        l = jnp.sum(p, axis=-1, keepdims=True)
        o = jnp.dot(p.astype(BF16), v, preferred_element_type=F32)
        o_ref[0, :, cols] = (o / l).astype(o_ref.dtype)


def _memattn(q, kv, *, n_heads, tq=512):
    b, seq, width = q.shape
    n_mem = kv.shape[1]
    tq = min(tq, seq)
    assert seq % tq == 0
    return pl.pallas_call(
        functools.partial(_memattn_body, n_heads=n_heads, head_dim=width // n_heads),
        grid=(b, seq // tq),
        in_specs=[
            pl.BlockSpec((1, tq, width), lambda bi, t: (bi, t, 0)),
            pl.BlockSpec((1, n_mem, 2 * width), lambda bi, t: (bi, 0, 0)),
        ],
        out_specs=pl.BlockSpec((1, tq, width), lambda bi, t: (bi, t, 0)),
        out_shape=jax.ShapeDtypeStruct((b, seq, width), BF16),
        compiler_params=_params(("parallel", "arbitrary")),
        name="memattn",
    )(q, kv)


def _merge_body(h_ref, a_ref, b_ref, c_ref, wga_ref, wgb_ref, wgc_ref, bg_ref, wa_ref, wb_ref, wc_ref, o_ref):
    h = h_ref[...]
    merged = None
    for i, (x_ref, wg_ref, w_ref) in enumerate(
            ((a_ref, wga_ref, wa_ref), (b_ref, wgb_ref, wb_ref), (c_ref, wgc_ref, wc_ref))):
        logits = jnp.dot(h, wg_ref[...], preferred_element_type=F32) + bg_ref[i:i + 1, :]
        gate = 1.0 / (1.0 + jnp.exp(-logits))
        y = jnp.dot(x_ref[...], w_ref[...], preferred_element_type=F32)
        merged = gate * y if merged is None else merged + gate * y
    o_ref[...] = merged.astype(o_ref.dtype)


def _merge(h, att_a, mix_b, att_c, w_in, gate_col0, b_gate, w_a, w_b, w_c, *, tm=512, tn=512):
    n_tok, d = h.shape
    tm = min(tm, n_tok)
    tn = min(tn, d)
    assert n_tok % tm == 0 and d % tn == 0 and gate_col0 % tn == 0
    nj = d // tn
    g0 = gate_col0 // tn

    def act(x):
        return pl.BlockSpec((tm, x.shape[1]), lambda j, i: (i, 0))

    def gate_w(branch):
        return pl.BlockSpec((d, tn), lambda j, i: (0, g0 + branch * nj + j))

    def out_w(w):
        return pl.BlockSpec((w.shape[0], tn), lambda j, i: (0, j))

    return pl.pallas_call(
        _merge_body,
        grid=(nj, n_tok // tm),
        in_specs=[act(h), act(att_a), act(mix_b), act(att_c),
                  gate_w(0), gate_w(1), gate_w(2),
                  pl.BlockSpec((b_gate.shape[0], tn), lambda j, i: (0, j)),
                  out_w(w_a), out_w(w_b), out_w(w_c)],
        out_specs=pl.BlockSpec((tm, tn), lambda j, i: (i, j)),
        out_shape=jax.ShapeDtypeStruct((n_tok, d), BF16),
        compiler_params=_params(("parallel", "parallel")),
        name="merge",
    )(h, att_a, mix_b, att_c, w_in, w_in, w_in, b_gate, w_a, w_b, w_c)


def _wo_body(x_ref, m_ref, w_ref, g_ref, o_ref, *, row_chunk):
    j = pl.program_id(1)
    tm = x_ref.shape[0]

    @pl.when(j == 0)
    def _():
        o_ref[...] = jnp.dot(m_ref[...], w_ref[...], preferred_element_type=F32)

    @pl.when(j > 0)
    def _():
        o_ref[...] += jnp.dot(m_ref[...], w_ref[...], preferred_element_type=F32)

    @pl.when(j == pl.num_programs(1) - 1)
    def _():
        def epilogue(rows):
            o_ref[rows, :] = x_ref[rows, :] + _rms(o_ref[rows, :], g_ref[...])
        _for_row_chunks(tm, row_chunk, epilogue)


def _wo(x, merged, w_o, g_post, *, tm=512, tk=1024):
    n_tok, d = x.shape
    tm = min(tm, n_tok)
    tk = min(tk, d)
    assert n_tok % tm == 0 and d % tk == 0
    return pl.pallas_call(
        functools.partial(_wo_body, row_chunk=min(64, tm)),
        grid=(n_tok // tm, d // tk),
        in_specs=[
            pl.BlockSpec((tm, d), lambda i, j: (i, 0)),
            pl.BlockSpec((tm, tk), lambda i, j: (i, j)),
            pl.BlockSpec((tk, d), lambda i, j: (j, 0)),
            pl.BlockSpec((1, d), lambda i, j: (0, 0)),
        ],
        out_specs=pl.BlockSpec((tm, d), lambda i, j: (i, 0)),
        out_shape=jax.ShapeDtypeStruct((n_tok, d), F32),
        compiler_params=_params(("parallel", "arbitrary")),
        name="wo",
    )(x, merged, w_o, g_post)


def _mixer(x1, h, mem, p, bias_tables, *, b, seq):
    n_tok, d = x1.shape
    na_heads = p["rpb"].shape[0]
    n_groups, group_dim, _ = p["w_pool"].shape
    pool_width = n_groups * group_dim
    mem_width = p["w_mem_kv"].shape[1] // 2
    na_width = p["w_a_out"].shape[0]
    gate_col0 = 3 * na_width + pool_width + mem_width

    qkv, u, qc = _inproj(h, p["w_in"], na_width=na_width, pool_width=pool_width, mem_width=mem_width)
    att_a = _natten(qkv.reshape(b, seq, 3 * na_width), bias_tables,
                    n_heads=na_heads, head_dim=na_width // na_heads, kh=(p["rpb"].shape[1] + 1) // 2)
    mix_b = _pool(u.reshape(b, seq, pool_width), p["w_pool"], p["pool_scale"])
    kv = _memkv(mem.reshape(b * mem.shape[1], d), p["g_mem"], p["w_mem_kv"])
    att_c = _memattn(qc.reshape(b, seq, mem_width), kv.reshape(b, mem.shape[1], 2 * mem_width),
                     n_heads=MEM_HEADS)
    merged = _merge(h, att_a.reshape(n_tok, na_width), mix_b.reshape(n_tok, pool_width),
                    att_c.reshape(n_tok, mem_width), p["w_in"], gate_col0, p["b_gate"],
                    p["w_a_out"], p["w_b_out"], p["w_c_out"])
    return _wo(x1, merged, p["w_o"], p["g_mix_post"])


_MATMUL_WEIGHTS = ("w1_gate", "w1_up", "w1_down", "w_in", "w_pool", "w_mem_kv",
                   "w_a_out", "w_b_out", "w_c_out", "w_o", "w2_gate", "w2_up", "w2_down")
_ROW_VECTORS = ("g_ffn1_pre", "g_ffn1_post", "g_mix_pre", "pool_scale", "g_mem", "g_mix_post",
                "g_ffn2_pre", "g_ffn2_post", "g_final")


def kernel(x_prompt, x_sample, mem_prompt, mem_sample, g_ffn1_pre, w1_gate, w1_up, w1_down, g_ffn1_post, g_mix_pre, w_in, rpb, w_pool, pool_scale, g_mem, w_mem_kv, w_a_out, w_b_out, w_c_out, b_gate, w_o, g_mix_post, g_ffn2_pre, w2_gate, w2_up, w2_down, g_ffn2_post, g_final):
    stacked = dict(g_ffn1_pre=g_ffn1_pre, w1_gate=w1_gate, w1_up=w1_up, w1_down=w1_down,
                   g_ffn1_post=g_ffn1_post, g_mix_pre=g_mix_pre, w_in=w_in, rpb=rpb, w_pool=w_pool,
                   pool_scale=pool_scale, g_mem=g_mem, w_mem_kv=w_mem_kv, w_a_out=w_a_out,
                   w_b_out=w_b_out, w_c_out=w_c_out, b_gate=b_gate, w_o=w_o, g_mix_post=g_mix_post,
                   g_ffn2_pre=g_ffn2_pre, w2_gate=w2_gate, w2_up=w2_up, w2_down=w2_down,
                   g_ffn2_post=g_ffn2_post, g_final=g_final)
    y_prompt, y_sample = x_prompt, x_sample
    for layer in range(g_final.shape[0]):
        p = {name: a[layer] for name, a in stacked.items()}
        for name in _MATMUL_WEIGHTS:
            p[name] = p[name].astype(BF16)
        for name in _ROW_VECTORS:
            p[name] = p[name].reshape(1, -1)
        bias_tables = _natten_bias_tables(p["rpb"])
        (bp, sp, d), (bs, ss, _) = y_prompt.shape, y_sample.shape
        x1p, x1s, hp, hs = _ffn_pair(
            y_prompt.reshape(bp * sp, d), y_sample.reshape(bs * ss, d),
            p["g_ffn1_pre"], p["w1_gate"], p["w1_up"], p["w1_down"], p["g_ffn1_post"], p["g_final"],
            p["g_mix_pre"], final_norm=False, emit_h=True)
        x2p = _mixer(x1p, hp, mem_prompt, p, bias_tables, b=bp, seq=sp)
        x2s = _mixer(x1s, hs, mem_sample, p, bias_tables, b=bs, seq=ss)
        yp, ys = _ffn_pair(
            x2p, x2s, p["g_ffn2_pre"], p["w2_gate"], p["w2_up"], p["w2_down"], p["g_ffn2_post"], p["g_final"],
            p["g_final"], final_norm=True, emit_h=False)
        y_prompt, y_sample = yp.reshape(bp, sp, d), ys.reshape(bs, ss, d)
    return (y_prompt, y_sample)
```
